```python
import jax, jax.numpy as jnp
from jax import lax
import numpy as np

D_MODEL = 2048
BATCH = 1
SEQ = 8192
DEPTH = 2

GRID_W = 64
CTX_LEN = 256
HEAD_DIM = 128
D_FF = 5632
N_MOD = 9
EPS = 1e-6
ROPE_BASE = 10000.0
NEG_INF = -1e30

ATTN_HEADS = 8
ATTN_KV_HEADS = 2
ATTN_BLOCK = 128
ATTN_WINDOW = 128
CONV_CH = 1024
CONV_W = 3
EVEN_IN = ATTN_HEADS * HEAD_DIM + 2 * ATTN_KV_HEADS * HEAD_DIM + 3 * CONV_CH
EVEN_OUT = ATTN_HEADS * HEAD_DIM + CONV_CH

RET_HEADS = 4
RET_DK = 128
RET_DV = 256
RET_CHUNK = 128
NA_HEADS = 8
NA_ROWS = 8
NA_COLS = 16
NA_BLOCK = 128
ODD_IN = 2 * RET_HEADS * RET_DK + 2 * RET_HEADS * RET_DV + 3 * NA_HEADS * HEAD_DIM
ODD_OUT = RET_HEADS * RET_DV + NA_HEADS * HEAD_DIM

N_EVEN = (DEPTH + 1) // 2
N_ODD = DEPTH // 2

kernel_name = 'hybrid_dit_ctx_prefix_trunk'


def rmsnorm(x, w):
    x32 = x.astype(jnp.float32)
    y = x32 * lax.rsqrt(jnp.mean(x32 * x32, axis=-1, keepdims=True) + EPS)
    return (y * w.astype(jnp.float32)).astype(x.dtype)


def modulate(h, shift, scale):
    return h * (1 + scale) + shift


def swiglu(h, wi, wo):
    g, u = jnp.split(h @ wi, 2, axis=-1)
    return (jax.nn.silu(g) * u) @ wo


def ffn_half(h, norm_w, shift, scale, gate, wi, wo):
    return h + 0.5 * gate * swiglu(modulate(rmsnorm(h, norm_w), shift, scale), wi, wo)


def split_cols(p, sizes):
    return jnp.split(p, [int(s) for s in np.cumsum(sizes)[:-1]], axis=-1)


def axial_rope(length, dim):
    t = jnp.arange(length)
    rows = (t // GRID_W).astype(jnp.float32)
    cols = (t % GRID_W).astype(jnp.float32)
    half = dim // 2
    inv = ROPE_BASE ** (-jnp.arange(0, half, 2, dtype=jnp.float32) / half)
    ang = jnp.concatenate([rows[:, None] * inv, cols[:, None] * inv], axis=-1)
    return jnp.cos(ang), jnp.sin(ang)


def apply_rope(x, cos, sin):
    B, L, H, d = x.shape
    q = d // 4
    xr = x.astype(jnp.float32).reshape(B, L, H, 2, 2, q)
    x1, x2 = xr[..., 0, :], xr[..., 1, :]
    cs = cos.reshape(L, 2, q)[None, :, None]
    sn = sin.reshape(L, 2, q)[None, :, None]
    out = jnp.stack([x1 * cs - x2 * sn, x1 * sn + x2 * cs], axis=-2)
    return out.reshape(B, L, H, d).astype(x.dtype)


def context_attention(q, k, v, sink=None):
    B, L, H, d = q.shape
    G = k.shape[2]
    R = H // G
    s = jnp.einsum('bqgrd,bkgd->bgrqk', q.reshape(B, L, G, R, d), k).astype(jnp.float32) * (d ** -0.5)
    if sink is not None:
        sk = jnp.broadcast_to(sink.astype(jnp.float32).reshape(1, G, R, 1, 1), s.shape[:-1] + (1,))
        s = jnp.concatenate([s, sk], axis=-1)
    p = jax.nn.softmax(s, axis=-1)[..., :L].astype(v.dtype)
    return jnp.einsum('bgrqk,bkgd->bqgrd', p, v).reshape(B, L, H * d)


def windowed_gqa_latent(q, k, v, kc, vc, sink):
    B, S, H, d = q.shape
    G = k.shape[2]
    R = H // G
    T = ATTN_BLOCK
    nb = S // T
    scale = d ** -0.5
    qb = q.reshape(B, nb, T, G, R, d)

    def band(a):
        ap = jnp.pad(a, ((0, 0), (T, T), (0, 0), (0, 0))).reshape(B, nb + 2, T, G, d)
        return jnp.concatenate([ap[:, :-2], ap[:, 1:-1], ap[:, 2:]], axis=2)

    kb, vb = band(k), band(v)
    qi = jnp.arange(T)
    ki = jnp.arange(3 * T)
    rel = ki[None, :] - T - qi[:, None]
    kpos = jnp.arange(nb)[:, None] * T + ki[None, :] - T
    mask = (jnp.abs(rel) <= ATTN_WINDOW)[None] & ((kpos >= 0) & (kpos < S))[:, None, :]
    s_loc = jnp.einsum('bnqgrd,bnkgd->bgrnqk', qb, kb).astype(jnp.float32) * scale
    s_loc = jnp.where(mask, s_loc, NEG_INF)
    s_ctx = jnp.einsum('bnqgrd,bcgd->bgrnqc', qb, kc).astype(jnp.float32) * scale
    s_sink = jnp.broadcast_to(sink.astype(jnp.float32).reshape(1, G, R, 1, 1, 1), s_loc.shape[:-1] + (1,))
    p = jax.nn.softmax(jnp.concatenate([s_loc, s_ctx, s_sink], axis=-1), axis=-1).astype(v.dtype)
    p_loc = p[..., :3 * T]
    p_ctx = p[..., 3 * T:3 * T + kc.shape[1]]
    o = jnp.einsum('bgrnqk,bnkgd->bnqgrd', p_loc, vb) + jnp.einsum('bgrnqc,bcgd->bnqgrd', p_ctx, vc)
    return o.reshape(B, S, H * d)


def short_conv(z, w, b):
    L = z.shape[1]
    zp = jnp.pad(z, ((0, 0), (1, 1), (0, 0)))
    return zp[:, :L] * w[0] + zp[:, 1:L + 1] * w[1] + zp[:, 2:] * w[2] + b


def project_even(h, w_in):
    B, L, _ = h.shape
    q, k, v, bg, cg, u = split_cols(h @ w_in, [ATTN_HEADS * HEAD_DIM, ATTN_KV_HEADS * HEAD_DIM,
                                              ATTN_KV_HEADS * HEAD_DIM, CONV_CH, CONV_CH, CONV_CH])
    return (q.reshape(B, L, ATTN_HEADS, HEAD_DIM), k.reshape(B, L, ATTN_KV_HEADS, HEAD_DIM),
            v.reshape(B, L, ATTN_KV_HEADS, HEAD_DIM), bg, cg, u)


def mixer_attn_conv(h_c, h_x, w_in, w_out, sink, conv_w, conv_b, cos, sin, ctx_out):
    qx, kx, vx, bx, cx, ux = project_even(h_x, w_in)
    qc, kc, vc, bc, cc, uc = project_even(h_c, w_in)
    qx = apply_rope(qx, cos, sin)
    kx = apply_rope(kx, cos, sin)
    att_x = windowed_gqa_latent(qx, kx, vx, kc, vc, sink)
    conv_x = bx * short_conv(cx * ux, conv_w, conv_b)
    y_x = jnp.concatenate([att_x, conv_x], axis=-1) @ w_out
    if not ctx_out:
        return None, y_x
    att_c = context_attention(qc, kc, vc, sink)
    conv_c = bc * short_conv(cc * uc, conv_w, conv_b)
    y_c = jnp.concatenate([att_c, conv_c], axis=-1) @ w_out
    return y_c, y_x


def chunk_retention(q, k, v, log_g, s0, include_diag):
    B, L, H, dk = q.shape
    dv = v.shape[-1]
    C = RET_CHUNK
    n = L // C
    qc = q.astype(jnp.float32).reshape(B, n, C, H, dk)
    kc = k.astype(jnp.float32).reshape(B, n, C, H, dk)
    vc = v.astype(jnp.float32).reshape(B, n, C, H, dv)
    i = jnp.arange(C, dtype=jnp.float32)
    rel = i[:, None] - i[None, :]
    keep = (rel >= 0) if include_diag else (rel > 0)
    dmask = jnp.where(keep[None], jnp.exp(log_g[:, None, None] * jnp.where(keep, rel, 0.0)[None]), 0.0)
    inner = jnp.einsum('bnihd,bnjhd->bnhij', qc, kc) * dmask
    y = jnp.einsum('bnhij,bnjhe->bnihe', inner, vc)
    zeta = jnp.exp(log_g[None, :] * (C - 1 - i)[:, None])
    kv = jnp.einsum('bnjhd,jh,bnjhe->nbhde', kc, zeta, vc)
    g_chunk = jnp.exp(log_g * C)[None, :, None, None]

    def step(state, kv_n):
        return g_chunk * state + kv_n, state

    s_final, s_prev = lax.scan(step, s0, kv)
    xi = jnp.exp(log_g[None, :] * (i + 1.0)[:, None])
    y = y + jnp.einsum('bnihd,nbhde->bnihe', qc, s_prev) * xi[None, None, :, :, None]
    return y.reshape(B, L, H, dv), s_final


def retention_out(y, g, gn_w):
    B, L, H, dv = y.shape
    mu = jnp.mean(y, axis=-1, keepdims=True)
    var = jnp.mean(jnp.square(y - mu), axis=-1, keepdims=True)
    yn = (y - mu) * lax.rsqrt(var + EPS) * gn_w.astype(jnp.float32).reshape(H, dv)
    return (jax.nn.silu(g.astype(jnp.float32)) * yn.reshape(B, L, H * dv)).astype(g.dtype)


def na_indices(length, rows):
    kr = min(NA_ROWS, rows)
    kcn = NA_COLS
    t = jnp.arange(length)
    r = t // GRID_W
    c = t % GRID_W
    rs = jnp.clip(r - kr // 2, 0, rows - kr)
    cs = jnp.clip(c - kcn // 2, 0, GRID_W - kcn)
    kr_idx = rs[:, None] + jnp.arange(kr)[None, :]
    kc_idx = cs[:, None] + jnp.arange(kcn)[None, :]
    kidx = (kr_idx[:, :, None] * GRID_W + kc_idx[:, None, :]).reshape(length, kr * kcn)
    dr = kr_idx - r[:, None] + NA_ROWS - 1
    dc = kc_idx - c[:, None] + NA_COLS - 1
    bidx = (dr[:, :, None] * (2 * NA_COLS - 1) + dc[:, None, :]).reshape(length, kr * kcn)
    return kidx, bidx


def neighbourhood_attention_latent(q, k, v, kc, vc, rpb, kidx, bidx):
    B, S, H, d = q.shape
    nblk = S // NA_BLOCK
    nk = kidx.shape[-1]
    scale = d ** -0.5
    rpb_flat = rpb.reshape(H, -1).astype(jnp.float32)
    qb = q.reshape(B, nblk, NA_BLOCK, H, d).transpose(1, 0, 2, 3, 4)
    kib = kidx.reshape(nblk, NA_BLOCK, nk)
    bib = bidx.reshape(nblk, NA_BLOCK, nk)

    def block(args):
        qq, ki, bi = args
        kg = k[:, ki]
        vg = v[:, ki]
        s_loc = jnp.einsum('bqhd,bqkhd->bhqk', qq, kg).astype(jnp.float32) * scale + rpb_flat[:, bi][None]
        s_ctx = jnp.einsum('bqhd,bchd->bhqc', qq, kc).astype(jnp.float32) * scale
        p = jax.nn.softmax(jnp.concatenate([s_loc, s_ctx], axis=-1), axis=-1).astype(v.dtype)
        return (jnp.einsum('bhqk,bqkhd->bqhd', p[..., :nk], vg)
                + jnp.einsum('bhqc,bchd->bqhd', p[..., nk:], vc))

    o = lax.map(block, (qb, kib, bib))
    return o.transpose(1, 0, 2, 3, 4).reshape(B, S, H * d)


def project_odd(h, w_in):
    B, L, _ = h.shape
    rq, rk, rv, rg, nq, nk, nv = split_cols(h @ w_in, [RET_HEADS * RET_DK, RET_HEADS * RET_DK,
                                                       RET_HEADS * RET_DV, RET_HEADS * RET_DV,
                                                       NA_HEADS * HEAD_DIM, NA_HEADS * HEAD_DIM,
                                                       NA_HEADS * HEAD_DIM])
    return (rq.reshape(B, L, RET_HEADS, RET_DK), rk.reshape(B, L, RET_HEADS, RET_DK) * (RET_DK ** -0.5),
            rv.reshape(B, L, RET_HEADS, RET_DV), rg,
            nq.reshape(B, L, NA_HEADS, HEAD_DIM), nk.reshape(B, L, NA_HEADS, HEAD_DIM),
            nv.reshape(B, L, NA_HEADS, HEAD_DIM))


def mixer_ret_na(h_c, h_x, w_in, w_out, decay_f, decay_b, gn_w, rpb, cos, sin, kidx, bidx, ctx_out):
    rqx, rkx, rvx, rgx, nqx, nkx, nvx = project_odd(h_x, w_in)
    rqc, rkc, rvc, rgc, nqc, nkc, nvc = project_odd(h_c, w_in)
    rqx = apply_rope(rqx, cos, sin)
    rkx = apply_rope(rkx, cos, sin)
    lg_f = jax.nn.log_sigmoid(decay_f.astype(jnp.float32))
    lg_b = jax.nn.log_sigmoid(decay_b.astype(jnp.float32))
    rev = lambda a: jnp.flip(a, axis=1)
    B = h_x.shape[0]
    s0 = jnp.zeros((B, RET_HEADS, RET_DK, RET_DV), jnp.float32)
    yc_f, s_f = chunk_retention(rqc, rkc, rvc, lg_f, s0, True)
    yc_b, s_b = chunk_retention(rev(rqc), rev(rkc), rev(rvc), lg_b, s0, False)
    yx_f, _ = chunk_retention(rqx, rkx, rvx, lg_f, s_f, True)
    yx_b, _ = chunk_retention(rev(rqx), rev(rkx), rev(rvx), lg_b, s_b, False)
    ret_x = retention_out(yx_f + rev(yx_b), rgx, gn_w)
    na_x = neighbourhood_attention_latent(nqx, nkx, nvx, nkc, nvc, rpb, kidx, bidx)
    y_x = jnp.concatenate([ret_x, na_x], axis=-1) @ w_out
    if not ctx_out:
        return None, y_x
    ret_c = retention_out(yc_f + rev(yc_b), rgc, gn_w)
    na_c = context_attention(nqc, nkc, nvc)
    y_c = jnp.concatenate([ret_c, na_c], axis=-1) @ w_out
    return y_c, y_x


def setup_inputs(seed: int = 0) -> dict:
    key = jax.random.key(seed)
    ks = jax.random.split(key, 24)
    f32 = jnp.float32
    D = D_MODEL

    def nrm(k, shape, s):
        return s * jax.random.normal(k, shape, f32)

    dec0 = jnp.log(2.0 ** (5.0 + jnp.arange(RET_HEADS, dtype=f32)) - 1.0)
    return {
        'x': nrm(ks[0], (BATCH, SEQ, D), 1.0),
        'c': nrm(ks[1], (BATCH, D), 1.0),
        'ctx': nrm(ks[2], (BATCH, CTX_LEN, D), 1.0),
        'c_ctx': nrm(ks[3], (D,), 1.0),
        'ada_w': nrm(ks[4], (DEPTH, D, N_MOD * D), D ** -0.5),
        'ada_b': nrm(ks[5], (DEPTH, N_MOD * D), 0.01),
        'norm_w': 1.0 + nrm(ks[6], (DEPTH, 3, D), 0.05),
        'ffn_a_wi': nrm(ks[7], (DEPTH, D, 2 * D_FF), D ** -0.5),
        'ffn_a_wo': nrm(ks[8], (DEPTH, D_FF, D), D_FF ** -0.5),
        'ffn_b_wi': nrm(ks[9], (DEPTH, D, 2 * D_FF), D ** -0.5),
        'ffn_b_wo': nrm(ks[10], (DEPTH, D_FF, D), D_FF ** -0.5),
        'ev_w_in': nrm(ks[11], (N_EVEN, D, EVEN_IN), D ** -0.5),
        'ev_w_out': nrm(ks[12], (N_EVEN, EVEN_OUT, D), EVEN_OUT ** -0.5),
        'ev_sink': nrm(ks[13], (N_EVEN, ATTN_HEADS), 1.0),
        'ev_conv_w': nrm(ks[14], (N_EVEN, CONV_W, CONV_CH), CONV_W ** -0.5),
        'ev_conv_b': nrm(ks[15], (N_EVEN, CONV_CH), 0.02),
        'od_w_in': nrm(ks[16], (N_ODD, D, ODD_IN), D ** -0.5),
        'od_w_out': nrm(ks[17], (N_ODD, ODD_OUT, D), ODD_OUT ** -0.5),
        'od_decay_f': dec0 + nrm(ks[18], (N_ODD, RET_HEADS), 0.1),
        'od_decay_b': dec0 + nrm(ks[19], (N_ODD, RET_HEADS), 0.1),
        'od_gn_w': 1.0 + nrm(ks[20], (N_ODD, RET_HEADS * RET_DV), 0.05),
        'od_rpb': nrm(ks[21], (N_ODD, NA_HEADS, 2 * NA_ROWS - 1, 2 * NA_COLS - 1), 0.02),
        'final_norm_w': 1.0 + nrm(ks[22], (D,), 0.05),
    }


def reference(x, c, ctx, c_ctx, ada_w, ada_b, norm_w, ffn_a_wi, ffn_a_wo, ffn_b_wi, ffn_b_wo,
              ev_w_in, ev_w_out, ev_sink, ev_conv_w, ev_conv_b,
              od_w_in, od_w_out, od_decay_f, od_decay_b, od_gn_w, od_rpb, final_norm_w):
    B, S, D = x.shape
    rows = S // GRID_W
    cos, sin = axial_rope(S, HEAD_DIM)
    na_kidx, na_bidx = na_indices(S, rows)
    hx, hc = x, ctx
    for layer in range(DEPTH):
        last = layer == DEPTH - 1
        j = layer // 2
        mx = (jax.nn.silu(c) @ ada_w[layer] + ada_b[layer]).reshape(B, N_MOD, 1, D).transpose(1, 0, 2, 3)
        mc = (jax.nn.silu(c_ctx) @ ada_w[layer] + ada_b[layer]).reshape(N_MOD, 1, 1, D)
        nw = norm_w[layer]
        hx = ffn_half(hx, nw[0], mx[0], mx[1], mx[2], ffn_a_wi[layer], ffn_a_wo[layer])
        hc = ffn_half(hc, nw[0], mc[0], mc[1], mc[2], ffn_a_wi[layer], ffn_a_wo[layer])
        ux = modulate(rmsnorm(hx, nw[1]), mx[3], mx[4])
        uc = modulate(rmsnorm(hc, nw[1]), mc[3], mc[4])
        if layer % 2 == 0:
            yc, yx = mixer_attn_conv(uc, ux, ev_w_in[j], ev_w_out[j], ev_sink[j], ev_conv_w[j], ev_conv_b[j],
                                     cos, sin, not last)
        else:
            yc, yx = mixer_ret_na(uc, ux, od_w_in[j], od_w_out[j], od_decay_f[j], od_decay_b[j], od_gn_w[j],
                                  od_rpb[j], cos, sin, na_kidx, na_bidx, not last)
        hx = hx + mx[5] * yx
        hx = ffn_half(hx, nw[2], mx[6], mx[7], mx[8], ffn_b_wi[layer], ffn_b_wo[layer])
        if not last:
            hc = hc + mc[5] * yc
            hc = ffn_half(hc, nw[2], mc[6], mc[7], mc[8], ffn_b_wi[layer], ffn_b_wo[layer])
    return rmsnorm(hx, final_norm_w)
```

```python
import functools

import numpy as np
import jax
import jax.numpy as jnp
from jax import lax
from jax.experimental import pallas as pl
from jax.experimental.pallas import tpu as pltpu

F32 = jnp.float32
BF16 = jnp.bfloat16

D = 2048
S = 8192
LC = 256
M = S + LC
GW = 64
GROWS = S // GW
HD = 128
DFF = 5632
NMOD = 9
EPS = 1e-6
ROPE_BASE = 10000.0
NEG = -1e30

AH, AKV, AT = 8, 2, 128
CCH = 1024
EV_IN = AH * HD + 2 * AKV * HD + 3 * CCH
RH, RDK, RDV, RC = 4, 128, 256, 128
NH, NROWS, NCOLS, NQB = 8, 8, 16, 128
NWIN_ROWS = 10
NWIN = NWIN_ROWS * GW
OD_IN = 2 * RH * RDK + 2 * RH * RDV + 3 * NH * HD

VMEM_LIMIT = 56 * 1024 * 1024


def _cparams(sem):
    return pltpu.CompilerParams(dimension_semantics=sem, vmem_limit_bytes=VMEM_LIMIT)


def _row_is_ctx(i, tm):
    rows = i * tm + lax.broadcasted_iota(jnp.int32, (tm, 1), 0)
    return rows >= S


def _ada_kernel(c_ref, w_ref, b_ref, o_ref):
    cv = c_ref[...]
    a = cv * jax.nn.sigmoid(cv)
    acc = jnp.dot(a.astype(BF16), w_ref[0].astype(BF16), preferred_element_type=F32)
    o_ref[0, 0] = acc + b_ref[0]


def _ada(cvec, ada_w, ada_b):
    depth = ada_w.shape[0]
    tn = 1024
    per = D // tn
    return pl.pallas_call(
        _ada_kernel,
        grid=(depth, NMOD * per),
        in_specs=[
            pl.BlockSpec((8, D), lambda l, j: (0, 0)),
            pl.BlockSpec((1, D, tn), lambda l, j: (l, 0, j)),
            pl.BlockSpec((1, 1, tn), lambda l, j: (l, 0, j)),
        ],
        out_specs=pl.BlockSpec((1, 1, 8, tn), lambda l, j: (l, j // per, 0, j % per)),
        out_shape=jax.ShapeDtypeStruct((depth, NMOD, 8, D), F32),
        compiler_params=_cparams(("arbitrary", "arbitrary")),
        name="ada",
    )(cvec, ada_w, ada_b.reshape(depth, 1, NMOD * D))


class RowTile:
    def __init__(self, tm, mm, norm):
        assert tm % mm == 0 and tm % norm == 0 and mm % 16 == 0 and norm % 16 == 0
        self.tm, self.mm, self.norm = tm, mm, norm


def _sub(r, size):
    return pl.ds(pl.multiple_of(r * size, 16), size)


def _normmod_to_scratch(x_ref, nw_ref, mod_ref, hn_ref, rt):
    i = pl.program_id(0)
    nw = nw_ref[...]

    def body(r, carry):
        x = x_ref[_sub(r, rt.norm), :]
        ms = jnp.mean(x * x, axis=-1, keepdims=True)
        y = x * lax.rsqrt(ms + EPS) * nw
        rows = i * rt.tm + r * rt.norm + lax.broadcasted_iota(jnp.int32, (rt.norm, 1), 0)
        isc = rows >= S
        shift = jnp.where(isc, mod_ref[0, 0, 1:2, :], mod_ref[0, 0, 0:1, :])
        scale = jnp.where(isc, mod_ref[0, 1, 1:2, :], mod_ref[0, 1, 0:1, :])
        hn_ref[_sub(r, rt.norm), :] = (y * (1.0 + scale) + shift).astype(BF16)
        return carry

    lax.fori_loop(0, rt.tm // rt.norm, body, 0)


def _g1_swiglu_kernel(x_ref, nw_ref, mod_ref, wg_ref, wu_ref, o_ref, hn_ref, wgb_ref, wub_ref, *, rt):
    @pl.when(pl.program_id(1) == 0)
    def _():
        _normmod_to_scratch(x_ref, nw_ref, mod_ref, hn_ref, rt)

    wgb_ref[...] = wg_ref[...].astype(BF16)
    wub_ref[...] = wu_ref[...].astype(BF16)

    def body(r, carry):
        hn = hn_ref[_sub(r, rt.mm), :]
        g = jnp.dot(hn, wgb_ref[...], preferred_element_type=F32)
        u = jnp.dot(hn, wub_ref[...], preferred_element_type=F32)
        o_ref[_sub(r, rt.mm), :] = (g * jax.nn.sigmoid(g) * u).astype(o_ref.dtype)
        return carry

    lax.fori_loop(0, rt.tm // rt.mm, body, 0)


def _g1_plain_kernel(x_ref, nw_ref, mod_ref, w_ref, o_ref, hn_ref, wb_ref, *, rt):
    @pl.when(pl.program_id(1) == 0)
    def _():
        _normmod_to_scratch(x_ref, nw_ref, mod_ref, hn_ref, rt)

    wb_ref[...] = w_ref[...].astype(BF16)

    def body(r, carry):
        acc = jnp.dot(hn_ref[_sub(r, rt.mm), :], wb_ref[...], preferred_element_type=F32)
        o_ref[_sub(r, rt.mm), :] = acc.astype(o_ref.dtype)
        return carry

    lax.fori_loop(0, rt.tm // rt.mm, body, 0)


def _g1(h, rows, nw, mods, layer, third, w, *, swiglu, rt, tn, out_dtype):
    tm = rt.tm
    n_out = w.shape[1] // 2 if swiglu else w.shape[1]
    nj = n_out // tn
    common = [
        pl.BlockSpec((tm, D), lambda i, j: (i, 0)),
        pl.BlockSpec((1, D), lambda i, j: (0, 0)),
        pl.BlockSpec((1, 3, 8, D), lambda i, j: (layer, third, 0, 0)),
    ]
    if swiglu:
        body = functools.partial(_g1_swiglu_kernel, rt=rt)
        wspecs = [pl.BlockSpec((D, tn), lambda i, j: (0, j)),
                  pl.BlockSpec((D, tn), lambda i, j: (0, j + nj))]
        wargs = (w, w)
    else:
        body = functools.partial(_g1_plain_kernel, rt=rt)
        wspecs = [pl.BlockSpec((D, tn), lambda i, j: (0, j))]
        wargs = (w,)
    return pl.pallas_call(
        body,
        grid=(rows // tm, nj),
        in_specs=common + wspecs,
        out_specs=pl.BlockSpec((tm, tn), lambda i, j: (i, j)),
        out_shape=jax.ShapeDtypeStruct((rows, n_out), out_dtype),
        scratch_shapes=[pltpu.VMEM((tm, D), BF16)] + [pltpu.VMEM((D, tn), BF16)] * len(wargs),
        compiler_params=_cparams(("arbitrary", "arbitrary")),
        name="g1_swiglu" if swiglu else "g1_plain",
    )(h, nw.reshape(1, D), mods, *wargs)


def _g2_kernel(*refs, n_a, coef, rt):
    h_ref, gate_ref = refs[0], refs[1]
    a_refs = refs[2:2 + n_a]
    w_refs = refs[2 + n_a:2 + 2 * n_a]
    o_ref = refs[2 + 2 * n_a]
    wb_refs = refs[3 + 2 * n_a:3 + 3 * n_a]
    i = pl.program_id(0)
    for w_ref, wb_ref in zip(w_refs, wb_refs):
        wb_ref[...] = w_ref[...].astype(BF16)

    def body(r, carry):
        acc = None
        for a_ref, wb_ref in zip(a_refs, wb_refs):
            part = jnp.dot(a_ref[_sub(r, rt.mm), :], wb_ref[...], preferred_element_type=F32)
            acc = part if acc is None else acc + part
        rows = i * rt.tm + r * rt.mm + lax.broadcasted_iota(jnp.int32, (rt.mm, 1), 0)
        gate = jnp.where(rows >= S, gate_ref[0, 0, 1:2, :], gate_ref[0, 0, 0:1, :])
        o_ref[_sub(r, rt.mm), :] = h_ref[_sub(r, rt.mm), :] + (coef * gate) * acc
        return carry

    lax.fori_loop(0, rt.tm // rt.mm, body, 0)


def _g2(h, rows, mods, layer, gate_idx, a_list, w, *, coef, rt, tn):
    tm = rt.tm
    n_a = len(a_list)
    a_specs, w_specs = [], []
    off = 0
    for a in a_list:
        k = a.shape[1]
        a_specs.append(pl.BlockSpec((tm, k), lambda i, j: (i, 0)))
        w_specs.append(pl.BlockSpec((k, tn), lambda i, j, o=off // k: (o, j)))
        assert off % k == 0
        off += k
    return pl.pallas_call(
        functools.partial(_g2_kernel, n_a=n_a, coef=coef, rt=rt),
        grid=(rows // tm, D // tn),
        in_specs=[pl.BlockSpec((tm, tn), lambda i, j: (i, j)),
                  pl.BlockSpec((1, 1, 8, tn), lambda i, j: (layer, gate_idx, 0, j))]
                 + a_specs + w_specs,
        out_specs=pl.BlockSpec((tm, tn), lambda i, j: (i, j)),
        out_shape=jax.ShapeDtypeStruct((rows, D), F32),
        scratch_shapes=[pltpu.VMEM((a.shape[1], tn), BF16) for a in a_list],
        compiler_params=_cparams(("arbitrary", "arbitrary")),
        name="g2_%d" % n_a,
    )(h, mods, *a_list, *([w] * n_a))


def _rope_tables():
    t = np.arange(S)
    rows = (t // GW).astype(np.float32)
    cols = (t % GW).astype(np.float32)
    quarter = HD // 4
    inv = (ROPE_BASE ** (-jnp.arange(0, HD // 2, 2, dtype=F32) / (HD // 2)))
    ar = jnp.asarray(rows)[:, None] * inv
    ac = jnp.asarray(cols)[:, None] * inv
    cos = jnp.concatenate([jnp.cos(ar), jnp.cos(ar), jnp.cos(ac), jnp.cos(ac)], axis=-1)
    sin = jnp.concatenate([-jnp.sin(ar), jnp.sin(ar), -jnp.sin(ac), jnp.sin(ac)], axis=-1)
    assert quarter * 4 == HD
    lat = jnp.concatenate([cos, sin], axis=-1)
    ctx = jnp.concatenate([jnp.ones((LC, HD), F32), jnp.zeros((LC, HD), F32)], axis=-1)
    return jnp.concatenate([lat, ctx], axis=0)


def _rope(x, tab):
    c = tab[:, :HD]
    sg = tab[:, HD:]
    lane = lax.broadcasted_iota(jnp.int32, x.shape, 1)
    first = (lane % (HD // 2)) < (HD // 4)
    partner = jnp.where(first, pltpu.roll(x, HD - HD // 4, 1), pltpu.roll(x, HD // 4, 1))
    return x * c + partner * sg


def _attn_kernel(sink_ref, q_ref, kp_ref, kc_ref, kn_ref, vp_ref, vc_ref, vn_ref, kx_ref, vx_ref,
                 tp_ref, tc_ref, tn_ref, o_ref):
    g = pl.program_id(0)
    n = pl.program_id(1)
    nlat = S // AT
    scale = HD ** -0.5
    k_all = jnp.concatenate([
        _rope(kp_ref[...], tp_ref[...]).astype(BF16),
        _rope(kc_ref[...], tc_ref[...]).astype(BF16),
        _rope(kn_ref[...], tn_ref[...]).astype(BF16),
        kx_ref[...].astype(BF16)], axis=0)
    v_all = jnp.concatenate([vp_ref[...].astype(BF16), vc_ref[...].astype(BF16),
                             vn_ref[...].astype(BF16), vx_ref[...].astype(BF16)], axis=0)
    nk = 3 * AT + LC
    r = lax.broadcasted_iota(jnp.int32, (AT, nk), 0)
    c = lax.broadcasted_iota(jnp.int32, (AT, nk), 1)
    rel = c - AT - r
    pos = (n - 1) * AT + c
    hi = jnp.where(n < nlat, S, 0)
    ok_loc = (jnp.abs(rel) <= AT) & (pos >= 0) & (pos < hi)
    ok = ok_loc | (c >= 3 * AT)
    tq = tc_ref[...]
    for hh in range(AH // AKV):
        q = _rope(q_ref[:, hh * HD:(hh + 1) * HD], tq).astype(BF16)
        s = lax.dot_general(q, k_all, (((1,), (1,)), ((), ())), preferred_element_type=F32) * scale
        s = jnp.where(ok, s, NEG)
        sk = sink_ref[g * (AH // AKV) + hh]
        m = jnp.maximum(jnp.max(s, axis=-1, keepdims=True), sk)
        e = jnp.exp(s - m)
        den = jnp.sum(e, axis=-1, keepdims=True) + jnp.exp(sk - m)
        o = jnp.dot(e.astype(BF16), v_all, preferred_element_type=F32)
        o_ref[:, hh * HD:(hh + 1) * HD] = (o / den).astype(o_ref.dtype)


def _attn(p, sink, tabs):
    nlat = S // AT
    nblk = M // AT
    r = AH // AKV
    qw = r * HD
    kcol = AH * HD // HD
    vcol = kcol + AKV
    ctx_blk = S // LC

    def prev(n):
        return jnp.clip(n - 1, 0, nlat - 1)

    def cur(n):
        return jnp.minimum(n, nlat - 1)

    def nxt(n):
        return jnp.clip(n + 1, 0, nlat - 1)

    return pl.pallas_call(
        _attn_kernel,
        grid=(AKV, nblk),
        in_specs=[
            pl.BlockSpec(memory_space=pltpu.SMEM),
            pl.BlockSpec((AT, qw), lambda g, n: (n, g)),
            pl.BlockSpec((AT, HD), lambda g, n: (prev(n), kcol + g)),
            pl.BlockSpec((AT, HD), lambda g, n: (cur(n), kcol + g)),
            pl.BlockSpec((AT, HD), lambda g, n: (nxt(n), kcol + g)),
            pl.BlockSpec((AT, HD), lambda g, n: (prev(n), vcol + g)),
            pl.BlockSpec((AT, HD), lambda g, n: (cur(n), vcol + g)),
            pl.BlockSpec((AT, HD), lambda g, n: (nxt(n), vcol + g)),
            pl.BlockSpec((LC, HD), lambda g, n: (ctx_blk, kcol + g)),
            pl.BlockSpec((LC, HD), lambda g, n: (ctx_blk, vcol + g)),
            pl.BlockSpec((AT, 2 * HD), lambda g, n: (prev(n), 0)),
            pl.BlockSpec((AT, 2 * HD), lambda g, n: (n, 0)),
            pl.BlockSpec((AT, 2 * HD), lambda g, n: (nxt(n), 0)),
        ],
        out_specs=pl.BlockSpec((AT, qw), lambda g, n: (n, g)),
        out_shape=jax.ShapeDtypeStruct((M, AH * HD), BF16),
        compiler_params=_cparams(("arbitrary", "arbitrary")),
        name="attn",
    )(sink, p, p, p, p, p, p, p, p, p, tabs, tabs, tabs)


def _conv_kernel(b_ref, c_ref, u_ref, cp_ref, up_ref, cn_ref, un_ref, w_ref, bias_ref, o_ref, *, tr):
    i = pl.program_id(0)
    lat_blocks = S // tr
    z = c_ref[...] * u_ref[...]
    zp = cp_ref[7:8, :] * up_ref[7:8, :]
    zn = cn_ref[0:1, :] * un_ref[0:1, :]
    has_prev = jnp.logical_and(i != 0, i != lat_blocks)
    has_next = jnp.logical_and(i != lat_blocks - 1, i != M // tr - 1)
    zp = jnp.where(has_prev, zp, 0.0)
    zn = jnp.where(has_next, zn, 0.0)
    row = lax.broadcasted_iota(jnp.int32, z.shape, 0)
    z_m1 = jnp.where(row == 0, zp, pltpu.roll(z, 1, 0))
    z_p1 = jnp.where(row == tr - 1, zn, pltpu.roll(z, tr - 1, 0))
    conv = z_m1 * w_ref[0:1, :] + z * w_ref[1:2, :] + z_p1 * w_ref[2:3, :] + bias_ref[...]
    o_ref[...] = (b_ref[...] * conv).astype(o_ref.dtype)


def _conv(p, conv_w, conv_b):
    tr, tc = 256, 512
    assert LC % tr == 0 and S % tr == 0
    base = (AH * HD + 2 * AKV * HD) // tc
    per = CCH // tc
    nrb = M // tr
    h8 = tr // 8

    def prev8(i):
        return jnp.maximum(i * h8 - 1, 0)

    def next8(i):
        return jnp.minimum((i + 1) * h8, M // 8 - 1)

    return pl.pallas_call(
        functools.partial(_conv_kernel, tr=tr),
        grid=(nrb, per),
        in_specs=[
            pl.BlockSpec((tr, tc), lambda i, j: (i, base + j)),
            pl.BlockSpec((tr, tc), lambda i, j: (i, base + per + j)),
            pl.BlockSpec((tr, tc), lambda i, j: (i, base + 2 * per + j)),
            pl.BlockSpec((8, tc), lambda i, j: (prev8(i), base + per + j)),
            pl.BlockSpec((8, tc), lambda i, j: (prev8(i), base + 2 * per + j)),
            pl.BlockSpec((8, tc), lambda i, j: (next8(i), base + per + j)),
            pl.BlockSpec((8, tc), lambda i, j: (next8(i), base + 2 * per + j)),
            pl.BlockSpec((3, tc), lambda i, j: (0, j)),
            pl.BlockSpec((1, tc), lambda i, j: (0, j)),
        ],
        out_specs=pl.BlockSpec((tr, tc), lambda i, j: (i, j)),
        out_shape=jax.ShapeDtypeStruct((M, CCH), BF16),
        compiler_params=_cparams(("arbitrary", "arbitrary")),
        name="conv",
    )(p, p, p, p, p, p, p, conv_w, conv_b.reshape(1, CCH))


def _log_sigmoid(d):
    return jnp.minimum(d, 0.0) - jnp.log(1.0 + jnp.exp(-jnp.abs(d)))


def _ret_chunk(decay_ref, q_ref, k_ref, v_ref, tab_ref, st_ref, h, backward):
    lg = _log_sigmoid(jnp.full((1, 1), decay_ref[h], F32))
    tab = tab_ref[...]
    q = _rope(q_ref[:, h * RDK:(h + 1) * RDK], tab).astype(BF16)
    k = (_rope(k_ref[:, h * RDK:(h + 1) * RDK], tab) * (RDK ** -0.5)).astype(BF16)
    v = v_ref[:, h * RDV:(h + 1) * RDV]
    ii = lax.broadcasted_iota(jnp.int32, (RC, RC), 0)
    jj = lax.broadcasted_iota(jnp.int32, (RC, RC), 1)
    ic = lax.broadcasted_iota(jnp.int32, (RC, 1), 0).astype(F32)
    if backward:
        rel = jj - ii
        keep = rel > 0
        xi = jnp.exp(lg * (RC - ic))
        zeta = jnp.exp(lg * ic)
    else:
        rel = ii - jj
        keep = rel >= 0
        xi = jnp.exp(lg * (ic + 1.0))
        zeta = jnp.exp(lg * (RC - 1.0 - ic))
    dmask = jnp.where(keep, jnp.exp(lg * jnp.where(keep, rel, 0).astype(F32)), 0.0)
    inner = lax.dot_general(q, k, (((1,), (1,)), ((), ())), preferred_element_type=F32) * dmask
    y = jnp.dot(inner.astype(BF16), v.astype(BF16), preferred_element_type=F32)
    st = st_ref[h]
    y = y + jnp.dot(q, st.astype(BF16), preferred_element_type=F32) * xi
    kv = lax.dot_general(k, (v * zeta).astype(BF16), (((0,), (0,)), ((), ())),
                         preferred_element_type=F32)
    st_ref[h] = jnp.exp(lg * float(RC)) * st + kv
    return y


def _ret_fwd_kernel(decay_ref, q_ref, k_ref, v_ref, tab_ref, y_ref, st_ref):
    @pl.when(pl.program_id(0) == 0)
    def _():
        st_ref[...] = jnp.zeros_like(st_ref)

    for h in range(RH):
        y_ref[:, h * RDV:(h + 1) * RDV] = _ret_chunk(decay_ref, q_ref, k_ref, v_ref, tab_ref,
                                                     st_ref, h, False)


def _ret_bwd_kernel(decay_ref, q_ref, k_ref, v_ref, tab_ref, yf_ref, g_ref, gnw_ref, o_ref, st_ref):
    @pl.when(pl.program_id(0) == 0)
    def _():
        st_ref[...] = jnp.zeros_like(st_ref)

    for h in range(RH):
        sl = slice(h * RDV, (h + 1) * RDV)
        y = yf_ref[:, sl] + _ret_chunk(decay_ref, q_ref, k_ref, v_ref, tab_ref, st_ref, h, True)
        mu = jnp.mean(y, axis=-1, keepdims=True)
        yc = y - mu
        var = jnp.mean(yc * yc, axis=-1, keepdims=True)
        yn = yc * lax.rsqrt(var + EPS) * gnw_ref[:, sl]
        gt = g_ref[:, sl]
        o_ref[:, sl] = (gt * jax.nn.sigmoid(gt) * yn).astype(o_ref.dtype)


def _retention(p, decay_f, decay_b, gn_w, tabs):
    nch = M // RC
    nlat = S // RC
    qw, vw = RH * RDK, RH * RDV
    fwd = lambda s: ((s + nlat) % nch)
    bwd = lambda s: (nch - 1 - s)
    smem = pl.BlockSpec(memory_space=pltpu.SMEM)

    def specs(cm):
        return [smem,
                pl.BlockSpec((RC, qw), lambda s: (cm(s), 0)),
                pl.BlockSpec((RC, qw), lambda s: (cm(s), 1)),
                pl.BlockSpec((RC, vw), lambda s: (cm(s), 1)),
                pl.BlockSpec((RC, 2 * HD), lambda s: (cm(s), 0))]

    state = pltpu.VMEM((RH, RDK, RDV), F32)
    yf = pl.pallas_call(
        _ret_fwd_kernel,
        grid=(nch,),
        in_specs=specs(fwd),
        out_specs=pl.BlockSpec((RC, vw), lambda s: (fwd(s), 0)),
        out_shape=jax.ShapeDtypeStruct((M, vw), F32),
        scratch_shapes=[state],
        compiler_params=_cparams(("arbitrary",)),
        name="ret_fwd",
    )(decay_f, p, p, p, tabs)
    return pl.pallas_call(
        _ret_bwd_kernel,
        grid=(nch,),
        in_specs=specs(bwd) + [pl.BlockSpec((RC, vw), lambda s: (bwd(s), 0)),
                               pl.BlockSpec((RC, vw), lambda s: (bwd(s), 2)),
                               pl.BlockSpec((1, vw), lambda s: (0, 0))],
        out_specs=pl.BlockSpec((RC, vw), lambda s: (bwd(s), 0)),
        out_shape=jax.ShapeDtypeStruct((M, vw), BF16),
        scratch_shapes=[state],
        compiler_params=_cparams(("arbitrary",)),
        name="ret_bwd",
    )(decay_b, p, p, p, tabs, yf, p, gn_w.reshape(1, vw))


_NA_CASE_BLOCKS = (0, 1, 2, S // NQB - 2, S // NQB - 1)


def _na_window_row(b):
    return jnp.clip(2 * b - NROWS // 2, 0, GROWS - NWIN_ROWS)


def _na_tables(rpb):
    rows_per_blk = NQB // GW
    idx, ok = [], []
    for b in _NA_CASE_BLOCKS:
        w = int(np.clip(rows_per_blk * b - NROWS // 2, 0, GROWS - NWIN_ROWS))
        ql = np.arange(NQB)
        r = rows_per_blk * b + ql // GW
        c = ql % GW
        rs = np.clip(r - NROWS // 2, 0, GROWS - NROWS)
        cs = np.clip(c - NCOLS // 2, 0, GW - NCOLS)
        kl = np.arange(NWIN)
        kr = w + kl // GW
        kc = kl % GW
        valid = ((kr[None, :] >= rs[:, None]) & (kr[None, :] < rs[:, None] + NROWS)
                 & (kc[None, :] >= cs[:, None]) & (kc[None, :] < cs[:, None] + NCOLS))
        bi = (kr[None, :] - r[:, None] + NROWS - 1) * (2 * NCOLS - 1) + (kc[None, :] - c[:, None] + NCOLS - 1)
        idx.append(np.where(valid, bi, 0))
        ok.append(valid)
    idx = np.stack(idx).astype(np.int32)
    ok = np.stack(ok)
    flat = rpb.reshape(NH, -1).astype(F32)
    tab = jnp.transpose(flat[:, idx], (1, 0, 2, 3))
    return jnp.where(ok[:, None], tab, NEG)


def _na_kernel(q_ref, k_ref, v_ref, t_ref, o_ref):
    b = pl.program_id(1)
    scale = HD ** -0.5
    start = pl.multiple_of(_na_window_row(b) * GW, GW)
    q = q_ref[...].astype(BF16)
    kw = k_ref[pl.ds(start, NWIN), :].astype(BF16)
    vw = v_ref[pl.ds(start, NWIN), :].astype(BF16)
    kx = k_ref[S:M, :].astype(BF16)
    vx = v_ref[S:M, :].astype(BF16)
    nt = (((1,), (1,)), ((), ()))
    s_loc = lax.dot_general(q, kw, nt, preferred_element_type=F32) * scale + t_ref[0, 0]
    s_ctx = lax.dot_general(q, kx, nt, preferred_element_type=F32) * scale
    m = jnp.maximum(jnp.max(s_loc, axis=-1, keepdims=True), jnp.max(s_ctx, axis=-1, keepdims=True))
    e_loc = jnp.exp(s_loc - m)
    e_ctx = jnp.exp(s_ctx - m)
    den = jnp.sum(e_loc, axis=-1, keepdims=True) + jnp.sum(e_ctx, axis=-1, keepdims=True)
    o = (jnp.dot(e_loc.astype(BF16), vw, preferred_element_type=F32)
         + jnp.dot(e_ctx.astype(BF16), vx, preferred_element_type=F32))
    o_ref[...] = (o / den).astype(o_ref.dtype)


def _na(p, rpb):
    nblk = S // NQB
    qcol = (2 * RH * RDK + 2 * RH * RDV) // HD
    kcol = qcol + NH
    vcol = kcol + NH
    tables = _na_tables(rpb)

    def case(b):
        return jnp.minimum(b, 2) + jnp.maximum(b - (nblk - 3), 0)

    return pl.pallas_call(
        _na_kernel,
        grid=(NH, nblk),
        in_specs=[
            pl.BlockSpec((NQB, HD), lambda h, b: (b, qcol + h)),
            pl.BlockSpec((M, HD), lambda h, b: (0, kcol + h)),
            pl.BlockSpec((M, HD), lambda h, b: (0, vcol + h)),
            pl.BlockSpec((1, 1, NQB, NWIN), lambda h, b: (case(b), h, 0, 0)),
        ],
        out_specs=pl.BlockSpec((NQB, HD), lambda h, b: (b, h)),
        out_shape=jax.ShapeDtypeStruct((S, NH * HD), BF16),
        compiler_params=_cparams(("arbitrary", "arbitrary")),
        name="na",
    )(p, p, p, tables)


def _final_norm_kernel(x_ref, w_ref, o_ref):
    x = x_ref[...]
    ms = jnp.mean(x * x, axis=-1, keepdims=True)
    o_ref[...] = x * lax.rsqrt(ms + EPS) * w_ref[...]


def _final_norm(h, w):
    tr = 512
    return pl.pallas_call(
        _final_norm_kernel,
        grid=(S // tr,),
        in_specs=[pl.BlockSpec((tr, D), lambda i: (i, 0)), pl.BlockSpec((1, D), lambda i: (0, 0))],
        out_specs=pl.BlockSpec((tr, D), lambda i: (i, 0)),
        out_shape=jax.ShapeDtypeStruct((S, D), F32),
        compiler_params=_cparams(("arbitrary",)),
        name="final_norm",
    )(h, w.reshape(1, D))


RT_ALL = RowTile(1056, 352, 96)
RT_LAT = RowTile(1024, 256, 128)


def _ffn_half(h, rows, rt, mods, layer, third, nw, wi, wo):
    a = _g1(h, rows, nw, mods, layer, third, wi, swiglu=True, rt=rt, tn=512, out_dtype=BF16)
    return _g2(h, rows, mods, layer, 3 * third + 2, [a], wo, coef=0.5, rt=rt, tn=256)


def kernel(x, c, ctx, c_ctx, ada_w, ada_b, norm_w, ffn_a_wi, ffn_a_wo, ffn_b_wi, ffn_b_wo,
           ev_w_in, ev_w_out, ev_sink, ev_conv_w, ev_conv_b,
           od_w_in, od_w_out, od_decay_f, od_decay_b, od_gn_w, od_rpb, final_norm_w):
    assert x.shape == (1, S, D) and ctx.shape == (1, LC, D) and ada_w.shape[0] == 2
    h = jnp.concatenate([x[0], ctx[0]], axis=0)
    cvec = jnp.concatenate([c, c_ctx[None, :], jnp.zeros((6, D), F32)], axis=0)
    mods = _ada(cvec, ada_w, ada_b)
    tabs = _rope_tables()

    h = _ffn_half(h, M, RT_ALL, mods, 0, 0, norm_w[0, 0], ffn_a_wi[0], ffn_a_wo[0])
    p = _g1(h, M, norm_w[0, 1], mods, 0, 1, ev_w_in[0], swiglu=False, rt=RT_ALL, tn=512, out_dtype=F32)
    att = _attn(p, ev_sink[0], tabs)
    cnv = _conv(p, ev_conv_w[0], ev_conv_b[0])
    h = _g2(h, M, mods, 0, 5, [att, cnv], ev_w_out[0], coef=1.0, rt=RT_ALL, tn=512)
    h = _ffn_half(h, M, RT_ALL, mods, 0, 2, norm_w[0, 2], ffn_b_wi[0], ffn_b_wo[0])

    h = _ffn_half(h, M, RT_ALL, mods, 1, 0, norm_w[1, 0], ffn_a_wi[1], ffn_a_wo[1])
    p = _g1(h, M, norm_w[1, 1], mods, 1, 1, od_w_in[0], swiglu=False, rt=RT_ALL, tn=512, out_dtype=F32)
    ret = _retention(p, od_decay_f[0], od_decay_b[0], od_gn_w[0], tabs)
    nat = _na(p, od_rpb[0])
    h = _g2(h, S, mods, 1, 5, [ret, nat], od_w_out[0], coef=1.0, rt=RT_LAT, tn=512)
    h = _ffn_half(h, S, RT_LAT, mods, 1, 2, norm_w[1, 2], ffn_b_wi[1], ffn_b_wo[1])
    return _final_norm(h, final_norm_w)[None]
```

```python
import functools

import numpy as np
import jax
import jax.numpy as jnp
from jax import lax
from jax.experimental import pallas as pl
from jax.experimental.pallas import tpu as pltpu

F32 = jnp.float32
BF16 = jnp.bfloat16

D = 2048
S = 8192
LC = 256
M = S + LC
GW = 64
GROWS = S // GW
HD = 128
DFF = 5632
NMOD = 9
EPS = 1e-6
ROPE_BASE = 10000.0
NEG = -1e30

AH, AKV, AWIN = 8, 2, 128
AQB = 256
CCH = 1024
EV_IN = AH * HD + 2 * AKV * HD + 3 * CCH
RH, RDK, RDV, RC = 4, 128, 256, 128
NH, NROWS, NCOLS = 8, 8, 16
NQROWS = 4
NQB = NQROWS * GW
NWIN_ROWS = NQROWS + NROWS
NWIN = NWIN_ROWS * GW
NHPS = 2
OD_IN = 2 * RH * RDK + 2 * RH * RDV + 3 * NH * HD

VMEM_LIMIT = 56 * 1024 * 1024


def _cparams(sem):
    return pltpu.CompilerParams(dimension_semantics=sem, vmem_limit_bytes=VMEM_LIMIT)


def _sub(r, size):
    return pl.ds(pl.multiple_of(r * size, 16), size)


def _ada_kernel(c_ref, w_ref, b_ref, o_ref):
    cv = c_ref[...]
    a = cv * jax.nn.sigmoid(cv)
    acc = jnp.dot(a.astype(BF16), w_ref[0].astype(BF16), preferred_element_type=F32)
    o_ref[0, 0] = acc + b_ref[0]


def _ada(cvec, ada_w, ada_b):
    depth = ada_w.shape[0]
    tn = 1024
    per = D // tn
    return pl.pallas_call(
        _ada_kernel,
        grid=(depth, NMOD * per),
        in_specs=[
            pl.BlockSpec((8, D), lambda l, j: (0, 0)),
            pl.BlockSpec((1, D, tn), lambda l, j: (l, 0, j)),
            pl.BlockSpec((1, 1, tn), lambda l, j: (l, 0, j)),
        ],
        out_specs=pl.BlockSpec((1, 1, 8, tn), lambda l, j: (l, j // per, 0, j % per)),
        out_shape=jax.ShapeDtypeStruct((depth, NMOD, 8, D), F32),
        compiler_params=_cparams(("arbitrary", "arbitrary")),
        name="ada",
    )(cvec, ada_w, ada_b.reshape(depth, 1, NMOD * D))


def _normmod_kernel(x_ref, nw_ref, mod_ref, o_ref, *, tr, chunk):
    kind = (pl.program_id(0) * tr >= S).astype(jnp.int32)
    shift = mod_ref[0, 0, pl.ds(kind, 1), :]
    gain = nw_ref[...] * (1.0 + mod_ref[0, 1, pl.ds(kind, 1), :])

    def body(r, carry):
        x = x_ref[_sub(r, chunk), :]
        rinv = lax.rsqrt(jnp.mean(x * x, axis=-1, keepdims=True) + EPS)
        o_ref[_sub(r, chunk), :] = ((x * rinv) * gain + shift).astype(o_ref.dtype)
        return carry

    lax.fori_loop(0, tr // chunk, body, 0, unroll=4)


def _normmod(h, rows, nw, mods, layer, third):
    tr, chunk = 256, 16
    assert S % tr == 0 and rows % tr == 0
    return pl.pallas_call(
        functools.partial(_normmod_kernel, tr=tr, chunk=chunk),
        grid=(rows // tr,),
        in_specs=[pl.BlockSpec((tr, D), lambda i: (i, 0)),
                  pl.BlockSpec((1, D), lambda i: (0, 0)),
                  pl.BlockSpec((1, 3, 8, D), lambda i: (layer, third, 0, 0))],
        out_specs=pl.BlockSpec((tr, D), lambda i: (i, 0)),
        out_shape=jax.ShapeDtypeStruct((rows, D), BF16),
        compiler_params=_cparams(("arbitrary",)),
        name="normmod",
    )(h, nw.reshape(1, D), mods)


def _cast_weights(w_refs, wb_refs):
    @pl.when(pl.program_id(1) == 0)
    def _():
        for w_ref, wb_ref in zip(w_refs, wb_refs):
            wb_ref[...] = w_ref[...].astype(BF16)


def _mm_swiglu_kernel(a_ref, wg_ref, wu_ref, o_ref, wgb_ref, wub_ref, *, tm, mm):
    _cast_weights((wg_ref, wu_ref), (wgb_ref, wub_ref))

    def body(r, carry):
        a = a_ref[_sub(r, mm), :]
        g = jnp.dot(a, wgb_ref[...], preferred_element_type=F32)
        u = jnp.dot(a, wub_ref[...], preferred_element_type=F32)
        o_ref[_sub(r, mm), :] = (g * jax.nn.sigmoid(g) * u).astype(o_ref.dtype)
        return carry

    lax.fori_loop(0, tm // mm, body, 0)


def _mm_plain_kernel(a_ref, w_ref, o_ref, wb_ref, *, tm, mm):
    _cast_weights((w_ref,), (wb_ref,))

    def body(r, carry):
        acc = jnp.dot(a_ref[_sub(r, mm), :], wb_ref[...], preferred_element_type=F32)
        o_ref[_sub(r, mm), :] = acc.astype(o_ref.dtype)
        return carry

    lax.fori_loop(0, tm // mm, body, 0)


def _mm(a, w, layer, *, swiglu, tm, mm, tn, out_dtype):
    rows, k = a.shape
    assert rows % tm == 0 and tm % mm == 0 and mm % 16 == 0
    n_out = w.shape[2] // 2 if swiglu else w.shape[2]
    nj = n_out // tn
    if swiglu:
        body = functools.partial(_mm_swiglu_kernel, tm=tm, mm=mm)
        wspecs = [pl.BlockSpec((None, k, tn), lambda j, i: (layer, 0, j)),
                  pl.BlockSpec((None, k, tn), lambda j, i: (layer, 0, j + nj))]
        wargs = (w, w)
    else:
        body = functools.partial(_mm_plain_kernel, tm=tm, mm=mm)
        wspecs = [pl.BlockSpec((None, k, tn), lambda j, i: (layer, 0, j))]
        wargs = (w,)
    return pl.pallas_call(
        body,
        grid=(nj, rows // tm),
        in_specs=[pl.BlockSpec((tm, k), lambda j, i: (i, 0))] + wspecs,
        out_specs=pl.BlockSpec((tm, tn), lambda j, i: (i, j)),
        out_shape=jax.ShapeDtypeStruct((rows, n_out), out_dtype),
        scratch_shapes=[pltpu.VMEM((k, tn), BF16)] * len(wargs),
        compiler_params=_cparams(("arbitrary", "arbitrary")),
        name="mm_swiglu" if swiglu else "mm_plain",
    )(a, *wargs)


def _mm_res_kernel(*refs, n_a, coef, tm, mm):
    h_ref, gate_ref = refs[0], refs[1]
    a_refs = refs[2:2 + n_a]
    w_refs = refs[2 + n_a:2 + 2 * n_a]
    o_ref = refs[2 + 2 * n_a]
    wb_refs = refs[3 + 2 * n_a:3 + 3 * n_a]
    i = pl.program_id(1)
    _cast_weights(w_refs, wb_refs)

    def body(r, carry):
        acc = None
        for a_ref, wb_ref in zip(a_refs, wb_refs):
            part = jnp.dot(a_ref[_sub(r, mm), :], wb_ref[...], preferred_element_type=F32)
            acc = part if acc is None else acc + part
        rows = i * tm + r * mm + lax.broadcasted_iota(jnp.int32, (mm, 1), 0)
        gate = jnp.where(rows >= S, gate_ref[0, 0, 1:2, :], gate_ref[0, 0, 0:1, :])
        o_ref[_sub(r, mm), :] = h_ref[_sub(r, mm), :] + (coef * gate) * acc
        return carry

    lax.fori_loop(0, tm // mm, body, 0)


def _mm_res(h, rows, mods, layer, gate_idx, a_list, w, wl, *, coef, tm, mm, tn):
    assert rows % tm == 0 and tm % mm == 0 and mm % 16 == 0
    n_a = len(a_list)
    a_specs, w_specs = [], []
    off = 0
    for a in a_list:
        k = a.shape[1]
        assert off % k == 0
        a_specs.append(pl.BlockSpec((tm, k), lambda j, i: (i, 0)))
        w_specs.append(pl.BlockSpec((None, k, tn), lambda j, i, o=off // k: (wl, o, j)))
        off += k
    return pl.pallas_call(
        functools.partial(_mm_res_kernel, n_a=n_a, coef=coef, tm=tm, mm=mm),
        grid=(D // tn, rows // tm),
        in_specs=[pl.BlockSpec((tm, tn), lambda j, i: (i, j)),
                  pl.BlockSpec((1, 1, 8, tn), lambda j, i: (layer, gate_idx, 0, j))]
                 + a_specs + w_specs,
        out_specs=pl.BlockSpec((tm, tn), lambda j, i: (i, j)),
        out_shape=jax.ShapeDtypeStruct((rows, D), F32),
        scratch_shapes=[pltpu.VMEM((a.shape[1], tn), BF16) for a in a_list],
        compiler_params=_cparams(("arbitrary", "arbitrary")),
        name="mm_res_%d" % n_a,
    )(h, mods, *a_list, *([w] * n_a))


def _rope_tables():
    t = np.arange(S)
    rows = (t // GW).astype(np.float32)
    cols = (t % GW).astype(np.float32)
    inv = (ROPE_BASE ** (-jnp.arange(0, HD // 2, 2, dtype=F32) / (HD // 2)))
    ar = jnp.asarray(rows)[:, None] * inv
    ac = jnp.asarray(cols)[:, None] * inv
    cos = jnp.concatenate([jnp.cos(ar), jnp.cos(ar), jnp.cos(ac), jnp.cos(ac)], axis=-1)
    sin = jnp.concatenate([-jnp.sin(ar), jnp.sin(ar), -jnp.sin(ac), jnp.sin(ac)], axis=-1)
    lat = jnp.concatenate([cos, sin], axis=-1)
    ctx = jnp.concatenate([jnp.ones((LC, HD), F32), jnp.zeros((LC, HD), F32)], axis=-1)
    return jnp.concatenate([lat, ctx], axis=0)


def _rope(x, tab):
    c = tab[:, :HD]
    sg = tab[:, HD:]
    lane = lax.broadcasted_iota(jnp.int32, x.shape, 1)
    first = (lane % (HD // 2)) < (HD // 4)
    partner = jnp.where(first, pltpu.roll(x, HD - HD // 4, 1), pltpu.roll(x, HD // 4, 1))
    return x * c + partner * sg


def _attn_kernel(sink_ref, q_ref, kp_ref, kc_ref, kn_ref, vp_ref, vc_ref, vn_ref, kx_ref, vx_ref,
                 tp_ref, tc_ref, tn_ref, o_ref):
    g = pl.program_id(0)
    n = pl.program_id(1)
    nlat = S // AQB
    scale = HD ** -0.5
    k_all = jnp.concatenate([
        _rope(kp_ref[...], tp_ref[...]).astype(BF16),
        _rope(kc_ref[...], tc_ref[...]).astype(BF16),
        _rope(kn_ref[...], tn_ref[...]).astype(BF16),
        kx_ref[...].astype(BF16)], axis=0)
    v_all = jnp.concatenate([vp_ref[...].astype(BF16), vc_ref[...].astype(BF16),
                             vn_ref[...].astype(BF16), vx_ref[...].astype(BF16)], axis=0)
    nloc = AQB + 2 * AWIN
    nk = nloc + LC
    r = lax.broadcasted_iota(jnp.int32, (AQB, nk), 0)
    c = lax.broadcasted_iota(jnp.int32, (AQB, nk), 1)
    rel = c - AWIN - r
    pos = n * AQB - AWIN + c
    hi = jnp.where(n < nlat, S, 0)
    ok = ((jnp.abs(rel) <= AWIN) & (pos >= 0) & (pos < hi)) | (c >= nloc)
    tq = tc_ref[...]
    for hh in range(AH // AKV):
        q = _rope(q_ref[:, hh * HD:(hh + 1) * HD], tq).astype(BF16)
        s = lax.dot_general(q, k_all, (((1,), (1,)), ((), ())), preferred_element_type=F32) * scale
        s = jnp.where(ok, s, NEG)
        sk = sink_ref[g * (AH // AKV) + hh]
        m = jnp.maximum(jnp.max(s, axis=-1, keepdims=True), sk)
        e = jnp.exp(s - m)
        den = jnp.sum(e, axis=-1, keepdims=True) + jnp.exp(sk - m)
        o = jnp.dot(e.astype(BF16), v_all, preferred_element_type=F32)
        o_ref[:, hh * HD:(hh + 1) * HD] = (o / den).astype(o_ref.dtype)


def _attn(p, sink, tabs):
    assert AQB == LC and AQB == 2 * AWIN
    nlat = S // AQB
    qw = (AH // AKV) * HD
    kcol = AH
    vcol = kcol + AKV
    last_w = S // AWIN - 1

    def prev(n):
        return jnp.clip(2 * n - 1, 0, last_w)

    def cur(n):
        return jnp.minimum(n, nlat - 1)

    def nxt(n):
        return jnp.clip(2 * n + 2, 0, last_w)

    return pl.pallas_call(
        _attn_kernel,
        grid=(AKV, M // AQB),
        in_specs=[
            pl.BlockSpec(memory_space=pltpu.SMEM),
            pl.BlockSpec((AQB, qw), lambda g, n: (n, g)),
            pl.BlockSpec((AWIN, HD), lambda g, n: (prev(n), kcol + g)),
            pl.BlockSpec((AQB, HD), lambda g, n: (cur(n), kcol + g)),
            pl.BlockSpec((AWIN, HD), lambda g, n: (nxt(n), kcol + g)),
            pl.BlockSpec((AWIN, HD), lambda g, n: (prev(n), vcol + g)),
            pl.BlockSpec((AQB, HD), lambda g, n: (cur(n), vcol + g)),
            pl.BlockSpec((AWIN, HD), lambda g, n: (nxt(n), vcol + g)),
            pl.BlockSpec((LC, HD), lambda g, n: (S // LC, kcol + g)),
            pl.BlockSpec((LC, HD), lambda g, n: (S // LC, vcol + g)),
            pl.BlockSpec((AWIN, 2 * HD), lambda g, n: (prev(n), 0)),
            pl.BlockSpec((AQB, 2 * HD), lambda g, n: (n, 0)),
            pl.BlockSpec((AWIN, 2 * HD), lambda g, n: (nxt(n), 0)),
        ],
        out_specs=pl.BlockSpec((AQB, qw), lambda g, n: (n, g)),
        out_shape=jax.ShapeDtypeStruct((M, AH * HD), BF16),
        compiler_params=_cparams(("arbitrary", "arbitrary")),
        name="attn",
    )(sink, p, p, p, p, p, p, p, p, p, tabs, tabs, tabs)


def _conv_kernel(b_ref, c_ref, u_ref, cp_ref, up_ref, cn_ref, un_ref, w_ref, bias_ref, o_ref, *, tr):
    i = pl.program_id(0)
    lat_blocks = S // tr
    z = c_ref[...] * u_ref[...]
    zp = cp_ref[7:8, :] * up_ref[7:8, :]
    zn = cn_ref[0:1, :] * un_ref[0:1, :]
    has_prev = jnp.logical_and(i != 0, i != lat_blocks)
    has_next = jnp.logical_and(i != lat_blocks - 1, i != M // tr - 1)
    zp = jnp.where(has_prev, zp, 0.0)
    zn = jnp.where(has_next, zn, 0.0)
    row = lax.broadcasted_iota(jnp.int32, z.shape, 0)
    z_m1 = jnp.where(row == 0, zp, pltpu.roll(z, 1, 0))
    z_p1 = jnp.where(row == tr - 1, zn, pltpu.roll(z, tr - 1, 0))
    conv = z_m1 * w_ref[0:1, :] + z * w_ref[1:2, :] + z_p1 * w_ref[2:3, :] + bias_ref[...]
    o_ref[...] = (b_ref[...] * conv).astype(o_ref.dtype)


def _conv(p, conv_w, conv_b):
    tr, tc = 256, 512
    assert LC % tr == 0 and S % tr == 0
    base = (AH * HD + 2 * AKV * HD) // tc
    per = CCH // tc
    nrb = M // tr
    h8 = tr // 8

    def prev8(i):
        return jnp.maximum(i * h8 - 1, 0)

    def next8(i):
        return jnp.minimum((i + 1) * h8, M // 8 - 1)

    return pl.pallas_call(
        functools.partial(_conv_kernel, tr=tr),
        grid=(nrb, per),
        in_specs=[
            pl.BlockSpec((tr, tc), lambda i, j: (i, base + j)),
            pl.BlockSpec((tr, tc), lambda i, j: (i, base + per + j)),
            pl.BlockSpec((tr, tc), lambda i, j: (i, base + 2 * per + j)),
            pl.BlockSpec((8, tc), lambda i, j: (prev8(i), base + per + j)),
            pl.BlockSpec((8, tc), lambda i, j: (prev8(i), base + 2 * per + j)),
            pl.BlockSpec((8, tc), lambda i, j: (next8(i), base + per + j)),
            pl.BlockSpec((8, tc), lambda i, j: (next8(i), base + 2 * per + j)),
            pl.BlockSpec((3, tc), lambda i, j: (0, j)),
            pl.BlockSpec((1, tc), lambda i, j: (0, j)),
        ],
        out_specs=pl.BlockSpec((tr, tc), lambda i, j: (i, j)),
        out_shape=jax.ShapeDtypeStruct((M, CCH), BF16),
        compiler_params=_cparams(("arbitrary", "arbitrary")),
        name="conv",
    )(p, p, p, p, p, p, p, conv_w, conv_b.reshape(1, CCH))


def _log_sigmoid(d):
    return jnp.minimum(d, 0.0) - jnp.log(1.0 + jnp.exp(-jnp.abs(d)))


def _ret_chunk(decay_ref, q_ref, k_ref, v_ref, tab_ref, st_ref, h, backward):
    lg = _log_sigmoid(jnp.full((1, 1), decay_ref[h], F32))
    tab = tab_ref[...]
    q = _rope(q_ref[:, h * RDK:(h + 1) * RDK], tab).astype(BF16)
    k = (_rope(k_ref[:, h * RDK:(h + 1) * RDK], tab) * (RDK ** -0.5)).astype(BF16)
    v = v_ref[:, h * RDV:(h + 1) * RDV]
    ii = lax.broadcasted_iota(jnp.int32, (RC, RC), 0)
    jj = lax.broadcasted_iota(jnp.int32, (RC, RC), 1)
    ic = lax.broadcasted_iota(jnp.int32, (RC, 1), 0).astype(F32)
    if backward:
        rel = jj - ii
        keep = rel > 0
        xi = jnp.exp(lg * (RC - ic))
        zeta = jnp.exp(lg * ic)
    else:
        rel = ii - jj
        keep = rel >= 0
        xi = jnp.exp(lg * (ic + 1.0))
        zeta = jnp.exp(lg * (RC - 1.0 - ic))
    dmask = jnp.where(keep, jnp.exp(lg * jnp.where(keep, rel, 0).astype(F32)), 0.0)
    inner = lax.dot_general(q, k, (((1,), (1,)), ((), ())), preferred_element_type=F32) * dmask
    y = jnp.dot(inner.astype(BF16), v.astype(BF16), preferred_element_type=F32)
    st = st_ref[h]
    y = y + jnp.dot(q, st.astype(BF16), preferred_element_type=F32) * xi
    kv = lax.dot_general(k, (v * zeta).astype(BF16), (((0,), (0,)), ((), ())),
                         preferred_element_type=F32)
    st_ref[h] = jnp.exp(lg * float(RC)) * st + kv
    return y


def _ret_fwd_kernel(decay_ref, q_ref, k_ref, v_ref, tab_ref, y_ref, st_ref):
    @pl.when(pl.program_id(0) == 0)
    def _():
        st_ref[...] = jnp.zeros_like(st_ref)

    for h in range(RH):
        y_ref[:, h * RDV:(h + 1) * RDV] = _ret_chunk(decay_ref, q_ref, k_ref, v_ref, tab_ref,
                                                     st_ref, h, False)


def _ret_bwd_kernel(decay_ref, q_ref, k_ref, v_ref, tab_ref, yf_ref, g_ref, gnw_ref, o_ref, st_ref):
    @pl.when(pl.program_id(0) == 0)
    def _():
        st_ref[...] = jnp.zeros_like(st_ref)

    for h in range(RH):
        sl = slice(h * RDV, (h + 1) * RDV)
        y = yf_ref[:, sl] + _ret_chunk(decay_ref, q_ref, k_ref, v_ref, tab_ref, st_ref, h, True)
        mu = jnp.mean(y, axis=-1, keepdims=True)
        yc = y - mu
        var = jnp.mean(yc * yc, axis=-1, keepdims=True)
        yn = yc * lax.rsqrt(var + EPS) * gnw_ref[:, sl]
        gt = g_ref[:, sl]
        o_ref[:, sl] = (gt * jax.nn.sigmoid(gt) * yn).astype(o_ref.dtype)


def _retention(p, decay_f, decay_b, gn_w, tabs):
    nch = M // RC
    nlat = S // RC
    qw, vw = RH * RDK, RH * RDV
    fwd = lambda s: ((s + nlat) % nch)
    bwd = lambda s: (nch - 1 - s)
    smem = pl.BlockSpec(memory_space=pltpu.SMEM)

    def specs(cm):
        return [smem,
                pl.BlockSpec((RC, qw), lambda s: (cm(s), 0)),
                pl.BlockSpec((RC, qw), lambda s: (cm(s), 1)),
                pl.BlockSpec((RC, vw), lambda s: (cm(s), 1)),
                pl.BlockSpec((RC, 2 * HD), lambda s: (cm(s), 0))]

    state = pltpu.VMEM((RH, RDK, RDV), F32)
    yf = pl.pallas_call(
        _ret_fwd_kernel,
        grid=(nch,),
        in_specs=specs(fwd),
        out_specs=pl.BlockSpec((RC, vw), lambda s: (fwd(s), 0)),
        out_shape=jax.ShapeDtypeStruct((M, vw), F32),
        scratch_shapes=[state],
        compiler_params=_cparams(("arbitrary",)),
        name="ret_fwd",
    )(decay_f, p, p, p, tabs)
    return pl.pallas_call(
        _ret_bwd_kernel,
        grid=(nch,),
        in_specs=specs(bwd) + [pl.BlockSpec((RC, vw), lambda s: (bwd(s), 0)),
                               pl.BlockSpec((RC, vw), lambda s: (bwd(s), 2)),
                               pl.BlockSpec((1, vw), lambda s: (0, 0))],
        out_specs=pl.BlockSpec((RC, vw), lambda s: (bwd(s), 0)),
        out_shape=jax.ShapeDtypeStruct((M, vw), BF16),
        scratch_shapes=[state],
        compiler_params=_cparams(("arbitrary",)),
        name="ret_bwd",
    )(decay_b, p, p, p, tabs, yf, p, gn_w.reshape(1, vw))


_NA_NBLK = S // NQB
_NA_CASES = (0, 1, _NA_NBLK - 1)


def _na_window_row(b):
    return jnp.clip(NQROWS * b - NROWS // 2, 0, GROWS - NWIN_ROWS)


def _na_table_kernel(rpb_ref, o_ref):
    h = pl.program_id(0)
    n_dr, n_dc = 2 * NROWS - 1, 2 * NCOLS - 1
    cq = lax.broadcasted_iota(jnp.int32, (GW, 2 * GW), 0)
    lane = lax.broadcasted_iota(jnp.int32, (GW, 2 * GW), 1)
    right = lane >= GW
    kc = jnp.where(right, lane - GW, lane)
    dc = kc - cq + (NCOLS - 1)
    cs = jnp.clip(cq - NCOLS // 2, 0, GW - NCOLS)
    col_ok = (kc >= cs) & (kc < cs + NCOLS)
    memo = {}

    def row_scalar(a, b):
        return rpb_ref[(h * n_dr + a) * n_dc + b] if 0 <= a < n_dr else 0.0

    def pair(a0):
        if a0 not in memo:
            acc = jnp.zeros((GW, 2 * GW), F32)
            for b in range(n_dc):
                vec = jnp.where(right, row_scalar(a0 + 1, b), row_scalar(a0, b))
                acc = jnp.where(dc == b, vec, acc)
            memo[a0] = acc
        return memo[a0]

    for ci, blk in enumerate(_NA_CASES):
        w = int(np.clip(NQROWS * blk - NROWS // 2, 0, GROWS - NWIN_ROWS))
        for rl in range(NQROWS):
            r = NQROWS * blk + rl
            rs = int(np.clip(r - NROWS // 2, 0, GROWS - NROWS))
            for pr in range(NWIN_ROWS // 2):
                kr = w + 2 * pr
                ok_l = rs <= kr < rs + NROWS
                ok_r = rs <= kr + 1 < rs + NROWS
                if ok_l and ok_r:
                    ok = col_ok
                elif ok_l:
                    ok = col_ok & jnp.logical_not(right)
                elif ok_r:
                    ok = col_ok & right
                else:
                    ok = None
                if ok is None:
                    tile = jnp.full((GW, 2 * GW), NEG, F32)
                else:
                    tile = jnp.where(ok, pair(kr - r + NROWS - 1), NEG)
                o_ref[ci, 0, rl * GW:(rl + 1) * GW, pr * 2 * GW:(pr + 1) * 2 * GW] = tile


def _na_tables(rpb):
    return pl.pallas_call(
        _na_table_kernel,
        grid=(NH,),
        in_specs=[pl.BlockSpec(memory_space=pltpu.SMEM)],
        out_specs=pl.BlockSpec((len(_NA_CASES), 1, NQB, NWIN), lambda h: (0, h, 0, 0)),
        out_shape=jax.ShapeDtypeStruct((len(_NA_CASES), NH, NQB, NWIN), F32),
        compiler_params=_cparams(("arbitrary",)),
        name="na_tables",
    )(rpb.reshape(-1))


def _na_kernel(q_ref, k_ref, v_ref, t_ref, o_ref, kb_ref, vb_ref):
    b = pl.program_id(1)

    @pl.when(b == 0)
    def _():
        kb_ref[...] = k_ref[...].astype(BF16)
        vb_ref[...] = v_ref[...].astype(BF16)

    scale = HD ** -0.5
    start = pl.multiple_of(_na_window_row(b) * GW, NQB)
    nt = (((1,), (1,)), ((), ()))
    for hh in range(NHPS):
        cs = slice(hh * HD, (hh + 1) * HD)
        q = q_ref[:, cs].astype(BF16)
        s_loc = lax.dot_general(q, kb_ref[pl.ds(start, NWIN), cs], nt,
                                preferred_element_type=F32) * scale + t_ref[0, hh]
        s_ctx = lax.dot_general(q, kb_ref[S:M, cs], nt, preferred_element_type=F32) * scale
        m = jnp.maximum(jnp.max(s_loc, axis=-1, keepdims=True), jnp.max(s_ctx, axis=-1, keepdims=True))
        e_loc = jnp.exp(s_loc - m)
        e_ctx = jnp.exp(s_ctx - m)
        den = jnp.sum(e_loc, axis=-1, keepdims=True) + jnp.sum(e_ctx, axis=-1, keepdims=True)
        o = (jnp.dot(e_loc.astype(BF16), vb_ref[pl.ds(start, NWIN), cs], preferred_element_type=F32)
             + jnp.dot(e_ctx.astype(BF16), vb_ref[S:M, cs], preferred_element_type=F32))
        o_ref[:, cs] = (o / den).astype(o_ref.dtype)


def _na(p, rpb):
    gw = NHPS * HD
    qcol = (2 * RH * RDK + 2 * RH * RDV) // gw
    kcol = qcol + NH // NHPS
    vcol = kcol + NH // NHPS
    tables = _na_tables(rpb)

    def case(b):
        return jnp.minimum(b, 1) + jnp.maximum(b - (_NA_NBLK - 2), 0)

    return pl.pallas_call(
        _na_kernel,
        grid=(NH // NHPS, _NA_NBLK),
        in_specs=[
            pl.BlockSpec((NQB, gw), lambda h, b: (b, qcol + h)),
            pl.BlockSpec((M, gw), lambda h, b: (0, kcol + h)),
            pl.BlockSpec((M, gw), lambda h, b: (0, vcol + h)),
            pl.BlockSpec((1, NHPS, NQB, NWIN), lambda h, b: (case(b), h, 0, 0)),
        ],
        out_specs=pl.BlockSpec((NQB, gw), lambda h, b: (b, h)),
        out_shape=jax.ShapeDtypeStruct((S, NH * HD), BF16),
        scratch_shapes=[pltpu.VMEM((M, gw), BF16), pltpu.VMEM((M, gw), BF16)],
        compiler_params=_cparams(("arbitrary", "arbitrary")),
        name="na",
    )(p, p, p, tables)


def _final_norm_kernel(x_ref, w_ref, o_ref, *, tr, chunk):
    w = w_ref[...]

    def body(r, carry):
        x = x_ref[_sub(r, chunk), :]
        rinv = lax.rsqrt(jnp.mean(x * x, axis=-1, keepdims=True) + EPS)
        o_ref[_sub(r, chunk), :] = x * rinv * w
        return carry

    lax.fori_loop(0, tr // chunk, body, 0, unroll=4)


def _final_norm(h, w):
    tr, chunk = 512, 16
    return pl.pallas_call(
        functools.partial(_final_norm_kernel, tr=tr, chunk=chunk),
        grid=(S // tr,),
        in_specs=[pl.BlockSpec((tr, D), lambda i: (i, 0)), pl.BlockSpec((1, D), lambda i: (0, 0))],
        out_specs=pl.BlockSpec((tr, D), lambda i: (i, 0)),
        out_shape=jax.ShapeDtypeStruct((S, D), F32),
        compiler_params=_cparams(("arbitrary",)),
        name="final_norm",
    )(h, w.reshape(1, D))


def _tiles(rows):
    if rows == M:
        return 2112, 352, 528
    assert rows == S
    return 2048, 256, 512


def _ffn_half(h, rows, mods, layer, third, nw, wi, wo):
    tm, mm, tm_down = _tiles(rows)
    hn = _normmod(h, rows, nw, mods, layer, third)
    a = _mm(hn, wi, layer, swiglu=True, tm=tm, mm=mm, tn=512, out_dtype=BF16)
    return _mm_res(h, rows, mods, layer, 3 * third + 2, [a], wo, layer,
                   coef=0.5, tm=tm_down, mm=tm_down, tn=512)


def _in_proj(h, mods, layer, nw, w):
    tm, mm, _ = _tiles(M)
    hn = _normmod(h, M, nw, mods, layer, 1)
    return _mm(hn, w, 0, swiglu=False, tm=tm, mm=mm, tn=512, out_dtype=F32)


def _out_proj(h, rows, mods, layer, parts, w):
    tm, mm, _ = _tiles(rows)
    return _mm_res(h, rows, mods, layer, 5, parts, w, 0, coef=1.0, tm=tm // 2, mm=mm, tn=512)


def kernel(x, c, ctx, c_ctx, ada_w, ada_b, norm_w, ffn_a_wi, ffn_a_wo, ffn_b_wi, ffn_b_wo,
           ev_w_in, ev_w_out, ev_sink, ev_conv_w, ev_conv_b,
           od_w_in, od_w_out, od_decay_f, od_decay_b, od_gn_w, od_rpb, final_norm_w):
    assert x.shape == (1, S, D) and ctx.shape == (1, LC, D) and ada_w.shape[0] == 2
    h = jnp.concatenate([x[0], ctx[0]], axis=0)
    cvec = jnp.concatenate([c, c_ctx[None, :], jnp.zeros((6, D), F32)], axis=0)
    mods = _ada(cvec, ada_w, ada_b)
    tabs = _rope_tables()

    h = _ffn_half(h, M, mods, 0, 0, norm_w[0, 0], ffn_a_wi, ffn_a_wo)
    p = _in_proj(h, mods, 0, norm_w[0, 1], ev_w_in)
    att = _attn(p, ev_sink[0], tabs)
    cnv = _conv(p, ev_conv_w[0], ev_conv_b[0])
    h = _out_proj(h, M, mods, 0, [att, cnv], ev_w_out)
    h = _ffn_half(h, M, mods, 0, 2, norm_w[0, 2], ffn_b_wi, ffn_b_wo)

    h = _ffn_half(h, M, mods, 1, 0, norm_w[1, 0], ffn_a_wi, ffn_a_wo)
    p = _in_proj(h, mods, 1, norm_w[1, 1], od_w_in)
    ret = _retention(p, od_decay_f[0], od_decay_b[0], od_gn_w[0], tabs)
    nat = _na(p, od_rpb[0])
    h = _out_proj(h, S, mods, 1, [ret, nat], od_w_out)
    h = _ffn_half(h, S, mods, 1, 2, norm_w[1, 2], ffn_b_wi, ffn_b_wo)
    return _final_norm(h, final_norm_w)[None]
```

```python
import functools

import numpy as np
import jax
import jax.numpy as jnp
from jax import lax
from jax.experimental import pallas as pl
from jax.experimental.pallas import tpu as pltpu

F32 = jnp.float32
BF16 = jnp.bfloat16

D = 2048
S = 8192
LC = 256
M = S + LC
GW = 64
GROWS = S // GW
HD = 128
DFF = 5632
NMOD = 9
EPS = 1e-6
ROPE_BASE = 10000.0
NEG = -1e30

AH, AKV, AWIN = 8, 2, 128
AQB = 256
CCH = 1024
EV_IN = AH * HD + 2 * AKV * HD + 3 * CCH
RH, RDK, RDV = 4, 128, 256
RC = 256
NH, NROWS, NCOLS = 8, 8, 16
NQROWS = 4
NQB = NQROWS * GW
NWIN_ROWS = NQROWS + NROWS
NWIN = NWIN_ROWS * GW
NHPS = 2
OD_IN = 2 * RH * RDK + 2 * RH * RDV + 3 * NH * HD

VMEM_LIMIT = 56 * 1024 * 1024


def _cparams(sem):
    return pltpu.CompilerParams(dimension_semantics=sem, vmem_limit_bytes=VMEM_LIMIT)


def _sub(r, size):
    return pl.ds(pl.multiple_of(r * size, 16), size)


def _ada_kernel(c_ref, w_ref, b_ref, o_ref):
    cv = c_ref[...]
    a = cv * jax.nn.sigmoid(cv)
    acc = jnp.dot(a.astype(BF16), w_ref[0].astype(BF16), preferred_element_type=F32)
    o_ref[0, 0] = acc + b_ref[0]


def _ada(cvec, ada_w, ada_b):
    depth = ada_w.shape[0]
    tn = 1024
    per = D // tn
    return pl.pallas_call(
        _ada_kernel,
        grid=(depth, NMOD * per),
        in_specs=[
            pl.BlockSpec((8, D), lambda l, j: (0, 0)),
            pl.BlockSpec((1, D, tn), lambda l, j: (l, 0, j)),
            pl.BlockSpec((1, 1, tn), lambda l, j: (l, 0, j)),
        ],
        out_specs=pl.BlockSpec((1, 1, 8, tn), lambda l, j: (l, j // per, 0, j % per)),
        out_shape=jax.ShapeDtypeStruct((depth, NMOD, 8, D), F32),
        compiler_params=_cparams(("arbitrary", "arbitrary")),
        name="ada",
    )(cvec, ada_w, ada_b.reshape(depth, 1, NMOD * D))


NORM_TR = 1024
NORM_CHUNK = 16
NORM_UNROLL = 4


def _norm_rows(src_ref, n_rows, fn):
    group = NORM_CHUNK * NORM_UNROLL

    def outer(gi, carry):
        for u in range(NORM_UNROLL):
            rs = pl.ds(pl.multiple_of(gi * group + u * NORM_CHUNK, NORM_CHUNK), NORM_CHUNK)
            fn(rs, src_ref[rs, :])
        return carry

    lax.fori_loop(0, n_rows // group, outer, 0)


def _normmod_kernel(*refs, rows, first):
    if first:
        x_ref, c_ref, nw_ref, mod_ref, o_ref, h_ref = refs
    else:
        x_ref, nw_ref, mod_ref, o_ref = refs
    i = pl.program_id(0)
    kind = (i * NORM_TR >= S).astype(jnp.int32)
    n_rows = jnp.minimum(rows - i * NORM_TR, NORM_TR)
    shift = mod_ref[0, 0, pl.ds(kind, 1), :]
    gain = nw_ref[...] * (1.0 + mod_ref[0, 1, pl.ds(kind, 1), :])

    def emit(rs, x):
        rinv = lax.rsqrt(jnp.mean(x * x, axis=-1, keepdims=True) + EPS)
        o_ref[rs, :] = ((x * rinv) * gain + shift).astype(o_ref.dtype)
        if first:
            h_ref[rs, :] = x

    if first:
        @pl.when(kind == 0)
        def _():
            _norm_rows(x_ref, n_rows, emit)

        @pl.when(kind == 1)
        def _():
            _norm_rows(c_ref, n_rows, emit)
    else:
        _norm_rows(x_ref, n_rows, emit)


def _normmod(h, rows, nw, mods, layer, third, ctx=None):
    first = ctx is not None
    assert S % NORM_TR == 0 and LC % (NORM_CHUNK * NORM_UNROLL) == 0 and LC <= NORM_TR
    nblk = pl.cdiv(rows, NORM_TR)
    row_spec = pl.BlockSpec((NORM_TR, D), lambda i: (i, 0))
    small = [pl.BlockSpec((1, D), lambda i: (0, 0)),
             pl.BlockSpec((1, 3, 8, D), lambda i: (layer, third, 0, 0))]
    if first:
        in_specs = [pl.BlockSpec((NORM_TR, D), lambda i: (jnp.minimum(i, S // NORM_TR - 1), 0)),
                    pl.BlockSpec((LC, D), lambda i: (0, 0))] + small
        args = (h, ctx, nw.reshape(1, D), mods)
        out_specs = [row_spec, row_spec]
        out_shape = [jax.ShapeDtypeStruct((rows, D), BF16), jax.ShapeDtypeStruct((rows, D), F32)]
    else:
        in_specs = [row_spec] + small
        args = (h, nw.reshape(1, D), mods)
        out_specs = row_spec
        out_shape = jax.ShapeDtypeStruct((rows, D), BF16)
    return pl.pallas_call(
        functools.partial(_normmod_kernel, rows=rows, first=first),
        grid=(nblk,),
        in_specs=in_specs,
        out_specs=out_specs,
        out_shape=out_shape,
        compiler_params=_cparams(("arbitrary",)),
        name="normmod_first" if first else "normmod",
    )(*args)


def _cast_weights(w_refs, wb_refs):
    @pl.when(pl.program_id(1) == 0)
    def _():
        for w_ref, wb_ref in zip(w_refs, wb_refs):
            wb_ref[...] = w_ref[...].astype(BF16)


def _mm_swiglu_kernel(a_ref, wg_ref, wu_ref, o_ref, wgb_ref, wub_ref, *, tm, mm):
    _cast_weights((wg_ref, wu_ref), (wgb_ref, wub_ref))

    def body(r, carry):
        a = a_ref[_sub(r, mm), :]
        g = jnp.dot(a, wgb_ref[...], preferred_element_type=F32)
        u = jnp.dot(a, wub_ref[...], preferred_element_type=F32)
        o_ref[_sub(r, mm), :] = (g * jax.nn.sigmoid(g) * u).astype(o_ref.dtype)
        return carry

    lax.fori_loop(0, tm // mm, body, 0)


def _mm_plain_kernel(a_ref, w_ref, o_ref, wb_ref, *, tm, mm):
    _cast_weights((w_ref,), (wb_ref,))

    def body(r, carry):
        acc = jnp.dot(a_ref[_sub(r, mm), :], wb_ref[...], preferred_element_type=F32)
        o_ref[_sub(r, mm), :] = acc.astype(o_ref.dtype)
        return carry

    lax.fori_loop(0, tm // mm, body, 0)


def _mm(a, w, layer, *, swiglu, tm, mm, tn, out_dtype):
    rows, k = a.shape
    assert rows % tm == 0 and tm % mm == 0 and mm % 16 == 0
    n_out = w.shape[2] // 2 if swiglu else w.shape[2]
    nj = n_out // tn
    if swiglu:
        body = functools.partial(_mm_swiglu_kernel, tm=tm, mm=mm)
        wspecs = [pl.BlockSpec((None, k, tn), lambda j, i: (layer, 0, j)),
                  pl.BlockSpec((None, k, tn), lambda j, i: (layer, 0, j + nj))]
        wargs = (w, w)
    else:
        body = functools.partial(_mm_plain_kernel, tm=tm, mm=mm)
        wspecs = [pl.BlockSpec((None, k, tn), lambda j, i: (layer, 0, j))]
        wargs = (w,)
    return pl.pallas_call(
        body,
        grid=(nj, rows // tm),
        in_specs=[pl.BlockSpec((tm, k), lambda j, i: (i, 0))] + wspecs,
        out_specs=pl.BlockSpec((tm, tn), lambda j, i: (i, j)),
        out_shape=jax.ShapeDtypeStruct((rows, n_out), out_dtype),
        scratch_shapes=[pltpu.VMEM((k, tn), BF16)] * len(wargs),
        compiler_params=_cparams(("arbitrary", "arbitrary")),
        name="mm_swiglu" if swiglu else "mm_plain",
    )(a, *wargs)


def _mm_res_kernel(*refs, n_a, coef, tm, mm):
    h_ref, gate_ref = refs[0], refs[1]
    a_refs = refs[2:2 + n_a]
    w_refs = refs[2 + n_a:2 + 2 * n_a]
    o_ref = refs[2 + 2 * n_a]
    wb_refs = refs[3 + 2 * n_a:3 + 3 * n_a]
    i = pl.program_id(1)
    _cast_weights(w_refs, wb_refs)

    def body(r, carry):
        acc = None
        for a_ref, wb_ref in zip(a_refs, wb_refs):
            part = jnp.dot(a_ref[_sub(r, mm), :], wb_ref[...], preferred_element_type=F32)
            acc = part if acc is None else acc + part
        rows = i * tm + r * mm + lax.broadcasted_iota(jnp.int32, (mm, 1), 0)
        gate = jnp.where(rows >= S, gate_ref[0, 0, 1:2, :], gate_ref[0, 0, 0:1, :])
        o_ref[_sub(r, mm), :] = h_ref[_sub(r, mm), :] + (coef * gate) * acc
        return carry

    lax.fori_loop(0, tm // mm, body, 0)


def _mm_res(h, rows, mods, layer, gate_idx, a_list, w, wl, *, coef, tm, mm, tn):
    assert rows % tm == 0 and tm % mm == 0 and mm % 16 == 0
    n_a = len(a_list)
    a_specs, w_specs = [], []
    off = 0
    for a in a_list:
        k = a.shape[1]
        assert off % k == 0
        a_specs.append(pl.BlockSpec((tm, k), lambda j, i: (i, 0)))
        w_specs.append(pl.BlockSpec((None, k, tn), lambda j, i, o=off // k: (wl, o, j)))
        off += k
    return pl.pallas_call(
        functools.partial(_mm_res_kernel, n_a=n_a, coef=coef, tm=tm, mm=mm),
        grid=(D // tn, rows // tm),
        in_specs=[pl.BlockSpec((tm, tn), lambda j, i: (i, j)),
                  pl.BlockSpec((1, 1, 8, tn), lambda j, i: (layer, gate_idx, 0, j))]
                 + a_specs + w_specs,
        out_specs=pl.BlockSpec((tm, tn), lambda j, i: (i, j)),
        out_shape=jax.ShapeDtypeStruct((rows, D), F32),
        scratch_shapes=[pltpu.VMEM((a.shape[1], tn), BF16) for a in a_list],
        compiler_params=_cparams(("arbitrary", "arbitrary")),
        name="mm_res_%d" % n_a,
    )(h, mods, *a_list, *([w] * n_a))


def _rope_tables():
    t = np.arange(S)
    rows = (t // GW).astype(np.float32)
    cols = (t % GW).astype(np.float32)
    half = HD // 2
    inv = np.float32(ROPE_BASE) ** (-np.arange(0, half, 2, dtype=np.float32) / np.float32(half))
    ar = rows[:, None] * inv
    ac = cols[:, None] * inv
    cos = np.concatenate([np.cos(ar), np.cos(ar), np.cos(ac), np.cos(ac)], axis=-1)
    sin = np.concatenate([-np.sin(ar), np.sin(ar), -np.sin(ac), np.sin(ac)], axis=-1)
    lat = np.concatenate([cos, sin], axis=-1)
    ctx = np.concatenate([np.ones((LC, HD), np.float32), np.zeros((LC, HD), np.float32)], axis=-1)
    return jnp.asarray(np.concatenate([lat, ctx], axis=0).astype(np.float32))


def _rope(x, tab):
    c = tab[:, :HD]
    sg = tab[:, HD:]
    lane = lax.broadcasted_iota(jnp.int32, x.shape, 1)
    first = (lane % (HD // 2)) < (HD // 4)
    partner = jnp.where(first, pltpu.roll(x, HD - HD // 4, 1), pltpu.roll(x, HD // 4, 1))
    return x * c + partner * sg


def _attn_kernel(sink_ref, q_ref, kp_ref, kc_ref, kn_ref, vp_ref, vc_ref, vn_ref, kx_ref, vx_ref,
                 tp_ref, tc_ref, tn_ref, o_ref):
    g = pl.program_id(0)
    n = pl.program_id(1)
    nlat = S // AQB
    scale = HD ** -0.5
    k_all = jnp.concatenate([
        _rope(kp_ref[...], tp_ref[...]).astype(BF16),
        _rope(kc_ref[...], tc_ref[...]).astype(BF16),
        _rope(kn_ref[...], tn_ref[...]).astype(BF16),
        kx_ref[...].astype(BF16)], axis=0)
    v_all = jnp.concatenate([vp_ref[...].astype(BF16), vc_ref[...].astype(BF16),
                             vn_ref[...].astype(BF16), vx_ref[...].astype(BF16)], axis=0)
    nloc = AQB + 2 * AWIN
    nk = nloc + LC
    r = lax.broadcasted_iota(jnp.int32, (AQB, nk), 0)
    c = lax.broadcasted_iota(jnp.int32, (AQB, nk), 1)
    rel = c - AWIN - r
    pos = n * AQB - AWIN + c
    hi = jnp.where(n < nlat, S, 0)
    ok = ((jnp.abs(rel) <= AWIN) & (pos >= 0) & (pos < hi)) | (c >= nloc)
    tq = tc_ref[...]
    for hh in range(AH // AKV):
        q = _rope(q_ref[:, hh * HD:(hh + 1) * HD], tq).astype(BF16)
        s = lax.dot_general(q, k_all, (((1,), (1,)), ((), ())), preferred_element_type=F32) * scale
        s = jnp.where(ok, s, NEG)
        sk = sink_ref[g * (AH // AKV) + hh]
        m = jnp.maximum(jnp.max(s, axis=-1, keepdims=True), sk)
        e = jnp.exp(s - m)
        den = jnp.sum(e, axis=-1, keepdims=True) + jnp.exp(sk - m)
        o = jnp.dot(e.astype(BF16), v_all, preferred_element_type=F32)
        o_ref[:, hh * HD:(hh + 1) * HD] = (o / den).astype(o_ref.dtype)


def _attn(p, sink, tabs):
    assert AQB == LC and AQB == 2 * AWIN
    nlat = S // AQB
    qw = (AH // AKV) * HD
    kcol = AH
    vcol = kcol + AKV
    last_w = S // AWIN - 1

    def prev(n):
        return jnp.clip(2 * n - 1, 0, last_w)

    def cur(n):
        return jnp.minimum(n, nlat - 1)

    def nxt(n):
        return jnp.clip(2 * n + 2, 0, last_w)

    return pl.pallas_call(
        _attn_kernel,
        grid=(AKV, M // AQB),
        in_specs=[
            pl.BlockSpec(memory_space=pltpu.SMEM),
            pl.BlockSpec((AQB, qw), lambda g, n: (n, g)),
            pl.BlockSpec((AWIN, HD), lambda g, n: (prev(n), kcol + g)),
            pl.BlockSpec((AQB, HD), lambda g, n: (cur(n), kcol + g)),
            pl.BlockSpec((AWIN, HD), lambda g, n: (nxt(n), kcol + g)),
            pl.BlockSpec((AWIN, HD), lambda g, n: (prev(n), vcol + g)),
            pl.BlockSpec((AQB, HD), lambda g, n: (cur(n), vcol + g)),
            pl.BlockSpec((AWIN, HD), lambda g, n: (nxt(n), vcol + g)),
            pl.BlockSpec((LC, HD), lambda g, n: (S // LC, kcol + g)),
            pl.BlockSpec((LC, HD), lambda g, n: (S // LC, vcol + g)),
            pl.BlockSpec((AWIN, 2 * HD), lambda g, n: (prev(n), 0)),
            pl.BlockSpec((AQB, 2 * HD), lambda g, n: (n, 0)),
            pl.BlockSpec((AWIN, 2 * HD), lambda g, n: (nxt(n), 0)),
        ],
        out_specs=pl.BlockSpec((AQB, qw), lambda g, n: (n, g)),
        out_shape=jax.ShapeDtypeStruct((M, AH * HD), BF16),
        compiler_params=_cparams(("arbitrary", "arbitrary")),
        name="attn",
    )(sink, p, p, p, p, p, p, p, p, p, tabs, tabs, tabs)


def _conv_kernel(b_ref, c_ref, u_ref, cp_ref, up_ref, cn_ref, un_ref, w_ref, bias_ref, o_ref, *, tr):
    i = pl.program_id(0)
    lat_blocks = S // tr
    z = c_ref[...] * u_ref[...]
    zp = cp_ref[7:8, :] * up_ref[7:8, :]
    zn = cn_ref[0:1, :] * un_ref[0:1, :]
    has_prev = jnp.logical_and(i != 0, i != lat_blocks)
    has_next = jnp.logical_and(i != lat_blocks - 1, i != M // tr - 1)
    zp = jnp.where(has_prev, zp, 0.0)
    zn = jnp.where(has_next, zn, 0.0)
    row = lax.broadcasted_iota(jnp.int32, z.shape, 0)
    z_m1 = jnp.where(row == 0, zp, pltpu.roll(z, 1, 0))
    z_p1 = jnp.where(row == tr - 1, zn, pltpu.roll(z, tr - 1, 0))
    conv = z_m1 * w_ref[0:1, :] + z * w_ref[1:2, :] + z_p1 * w_ref[2:3, :] + bias_ref[...]
    o_ref[...] = (b_ref[...] * conv).astype(o_ref.dtype)


def _conv(p, conv_w, conv_b):
    tr, tc = 256, 512
    assert LC % tr == 0 and S % tr == 0
    base = (AH * HD + 2 * AKV * HD) // tc
    per = CCH // tc
    nrb = M // tr
    h8 = tr // 8

    def prev8(i):
        return jnp.maximum(i * h8 - 1, 0)

    def next8(i):
        return jnp.minimum((i + 1) * h8, M // 8 - 1)

    return pl.pallas_call(
        functools.partial(_conv_kernel, tr=tr),
        grid=(nrb, per),
        in_specs=[
            pl.BlockSpec((tr, tc), lambda i, j: (i, base + j)),
            pl.BlockSpec((tr, tc), lambda i, j: (i, base + per + j)),
            pl.BlockSpec((tr, tc), lambda i, j: (i, base + 2 * per + j)),
            pl.BlockSpec((8, tc), lambda i, j: (prev8(i), base + per + j)),
            pl.BlockSpec((8, tc), lambda i, j: (prev8(i), base + 2 * per + j)),
            pl.BlockSpec((8, tc), lambda i, j: (next8(i), base + per + j)),
            pl.BlockSpec((8, tc), lambda i, j: (next8(i), base + 2 * per + j)),
            pl.BlockSpec((3, tc), lambda i, j: (0, j)),
            pl.BlockSpec((1, tc), lambda i, j: (0, j)),
        ],
        out_specs=pl.BlockSpec((tr, tc), lambda i, j: (i, j)),
        out_shape=jax.ShapeDtypeStruct((M, CCH), BF16),
        compiler_params=_cparams(("arbitrary", "arbitrary")),
        name="conv",
    )(p, p, p, p, p, p, p, conv_w, conv_b.reshape(1, CCH))


def _log_sigmoid(d):
    return jnp.minimum(d, 0.0) - jnp.log(1.0 + jnp.exp(-jnp.abs(d)))


def _ret_chunk(decay_ref, q_ref, k_ref, v_ref, tab_ref, st_ref, h, backward):
    lg = _log_sigmoid(jnp.full((1, 1), decay_ref[h], F32))
    tab = tab_ref[...]
    q = _rope(q_ref[:, h * RDK:(h + 1) * RDK], tab).astype(BF16)
    k = (_rope(k_ref[:, h * RDK:(h + 1) * RDK], tab) * (RDK ** -0.5)).astype(BF16)
    v = v_ref[:, h * RDV:(h + 1) * RDV]
    ii = lax.broadcasted_iota(jnp.int32, (RC, RC), 0)
    jj = lax.broadcasted_iota(jnp.int32, (RC, RC), 1)
    ic = lax.broadcasted_iota(jnp.int32, (RC, 1), 0).astype(F32)
    if backward:
        rel = jj - ii
        keep = rel > 0
        xi = jnp.exp(lg * (RC - ic))
        zeta = jnp.exp(lg * ic)
    else:
        rel = ii - jj
        keep = rel >= 0
        xi = jnp.exp(lg * (ic + 1.0))
        zeta = jnp.exp(lg * (RC - 1.0 - ic))
    dmask = jnp.where(keep, jnp.exp(lg * jnp.where(keep, rel, 0).astype(F32)), 0.0)
    inner = lax.dot_general(q, k, (((1,), (1,)), ((), ())), preferred_element_type=F32) * dmask
    y = jnp.dot(inner.astype(BF16), v.astype(BF16), preferred_element_type=F32)
    st = st_ref[h]
    y = y + jnp.dot(q, st.astype(BF16), preferred_element_type=F32) * xi
    kv = lax.dot_general(k, (v * zeta).astype(BF16), (((0,), (0,)), ((), ())),
                         preferred_element_type=F32)
    st_ref[h] = jnp.exp(lg * float(RC)) * st + kv
    return y


def _ret_fwd_kernel(decay_ref, q_ref, k_ref, v_ref, tab_ref, y_ref, st_ref):
    @pl.when(pl.program_id(0) == 0)
    def _():
        st_ref[...] = jnp.zeros_like(st_ref)

    for h in range(RH):
        y_ref[:, h * RDV:(h + 1) * RDV] = _ret_chunk(decay_ref, q_ref, k_ref, v_ref, tab_ref,
                                                     st_ref, h, False)


def _ret_bwd_kernel(decay_ref, q_ref, k_ref, v_ref, tab_ref, yf_ref, g_ref, gnw_ref, o_ref, st_ref):
    @pl.when(pl.program_id(0) == 0)
    def _():
        st_ref[...] = jnp.zeros_like(st_ref)

    for h in range(RH):
        sl = slice(h * RDV, (h + 1) * RDV)
        y = yf_ref[:, sl] + _ret_chunk(decay_ref, q_ref, k_ref, v_ref, tab_ref, st_ref, h, True)
        mu = jnp.mean(y, axis=-1, keepdims=True)
        yc = y - mu
        var = jnp.mean(yc * yc, axis=-1, keepdims=True)
        yn = yc * lax.rsqrt(var + EPS) * gnw_ref[:, sl]
        gt = g_ref[:, sl]
        o_ref[:, sl] = (gt * jax.nn.sigmoid(gt) * yn).astype(o_ref.dtype)


def _retention(p, decay_f, decay_b, gn_w, tabs):
    nch = M // RC
    nlat = S // RC
    qw, vw = RH * RDK, RH * RDV
    fwd = lambda s: ((s + nlat) % nch)
    bwd = lambda s: (nch - 1 - s)
    smem = pl.BlockSpec(memory_space=pltpu.SMEM)

    def specs(cm):
        return [smem,
                pl.BlockSpec((RC, qw), lambda s: (cm(s), 0)),
                pl.BlockSpec((RC, qw), lambda s: (cm(s), 1)),
                pl.BlockSpec((RC, vw), lambda s: (cm(s), 1)),
                pl.BlockSpec((RC, 2 * HD), lambda s: (cm(s), 0))]

    state = pltpu.VMEM((RH, RDK, RDV), F32)
    yf = pl.pallas_call(
        _ret_fwd_kernel,
        grid=(nch,),
        in_specs=specs(fwd),
        out_specs=pl.BlockSpec((RC, vw), lambda s: (fwd(s), 0)),
        out_shape=jax.ShapeDtypeStruct((M, vw), F32),
        scratch_shapes=[state],
        compiler_params=_cparams(("arbitrary",)),
        name="ret_fwd",
    )(decay_f, p, p, p, tabs)
    return pl.pallas_call(
        _ret_bwd_kernel,
        grid=(nch,),
        in_specs=specs(bwd) + [pl.BlockSpec((RC, vw), lambda s: (bwd(s), 0)),
                               pl.BlockSpec((RC, vw), lambda s: (bwd(s), 2)),
                               pl.BlockSpec((1, vw), lambda s: (0, 0))],
        out_specs=pl.BlockSpec((RC, vw), lambda s: (bwd(s), 0)),
        out_shape=jax.ShapeDtypeStruct((M, vw), BF16),
        scratch_shapes=[state],
        compiler_params=_cparams(("arbitrary",)),
        name="ret_bwd",
    )(decay_b, p, p, p, tabs, yf, p, gn_w.reshape(1, vw))


_NA_NBLK = S // NQB
_NA_CASES = (0, 1, _NA_NBLK - 1)


def _na_window_row(b):
    return jnp.clip(NQROWS * b - NROWS // 2, 0, GROWS - NWIN_ROWS)


def _na_table_kernel(rpb_ref, o_ref):
    h = pl.program_id(0)
    n_dr, n_dc = 2 * NROWS - 1, 2 * NCOLS - 1
    cq = lax.broadcasted_iota(jnp.int32, (GW, 2 * GW), 0)
    lane = lax.broadcasted_iota(jnp.int32, (GW, 2 * GW), 1)
    right = lane >= GW
    kc = jnp.where(right, lane - GW, lane)
    dc = kc - cq + (NCOLS - 1)
    cs = jnp.clip(cq - NCOLS // 2, 0, GW - NCOLS)
    col_ok = (kc >= cs) & (kc < cs + NCOLS)
    memo = {}

    def row_scalar(a, b):
        return rpb_ref[(h * n_dr + a) * n_dc + b] if 0 <= a < n_dr else 0.0

    def pair(a0):
        if a0 not in memo:
            acc = jnp.zeros((GW, 2 * GW), F32)
            for b in range(n_dc):
                vec = jnp.where(right, row_scalar(a0 + 1, b), row_scalar(a0, b))
                acc = jnp.where(dc == b, vec, acc)
            memo[a0] = acc
        return memo[a0]

    for ci, blk in enumerate(_NA_CASES):
        w = int(np.clip(NQROWS * blk - NROWS // 2, 0, GROWS - NWIN_ROWS))
        for rl in range(NQROWS):
            r = NQROWS * blk + rl
            rs = int(np.clip(r - NROWS // 2, 0, GROWS - NROWS))
            for pr in range(NWIN_ROWS // 2):
                kr = w + 2 * pr
                ok_l = rs <= kr < rs + NROWS
                ok_r = rs <= kr + 1 < rs + NROWS
                if ok_l and ok_r:
                    ok = col_ok
                elif ok_l:
                    ok = col_ok & jnp.logical_not(right)
                elif ok_r:
                    ok = col_ok & right
                else:
                    ok = None
                if ok is None:
                    tile = jnp.full((GW, 2 * GW), NEG, F32)
                else:
                    tile = jnp.where(ok, pair(kr - r + NROWS - 1), NEG)
                o_ref[ci, 0, rl * GW:(rl + 1) * GW, pr * 2 * GW:(pr + 1) * 2 * GW] = tile


def _na_tables(rpb):
    return pl.pallas_call(
        _na_table_kernel,
        grid=(NH,),
        in_specs=[pl.BlockSpec(memory_space=pltpu.SMEM)],
        out_specs=pl.BlockSpec((len(_NA_CASES), 1, NQB, NWIN), lambda h: (0, h, 0, 0)),
        out_shape=jax.ShapeDtypeStruct((len(_NA_CASES), NH, NQB, NWIN), F32),
        compiler_params=_cparams(("arbitrary",)),
        name="na_tables",
    )(rpb.reshape(-1))


def _na_kernel(q_ref, k_ref, v_ref, t_ref, o_ref, kb_ref, vb_ref):
    b = pl.program_id(1)

    @pl.when(b == 0)
    def _():
        kb_ref[...] = k_ref[...].astype(BF16)
        vb_ref[...] = v_ref[...].astype(BF16)

    scale = HD ** -0.5
    start = pl.multiple_of(_na_window_row(b) * GW, NQB)
    nt = (((1,), (1,)), ((), ()))
    for hh in range(NHPS):
        cs = slice(hh * HD, (hh + 1) * HD)
        q = q_ref[:, cs].astype(BF16)
        s_loc = lax.dot_general(q, kb_ref[pl.ds(start, NWIN), cs], nt,
                                preferred_element_type=F32) * scale + t_ref[0, hh]
        s_ctx = lax.dot_general(q, kb_ref[S:M, cs], nt, preferred_element_type=F32) * scale
        m = jnp.maximum(jnp.max(s_loc, axis=-1, keepdims=True), jnp.max(s_ctx, axis=-1, keepdims=True))
        e_loc = jnp.exp(s_loc - m)
        e_ctx = jnp.exp(s_ctx - m)
        den = jnp.sum(e_loc, axis=-1, keepdims=True) + jnp.sum(e_ctx, axis=-1, keepdims=True)
        o = (jnp.dot(e_loc.astype(BF16), vb_ref[pl.ds(start, NWIN), cs], preferred_element_type=F32)
             + jnp.dot(e_ctx.astype(BF16), vb_ref[S:M, cs], preferred_element_type=F32))
        o_ref[:, cs] = (o / den).astype(o_ref.dtype)


def _na(p, rpb):
    gw = NHPS * HD
    qcol = (2 * RH * RDK + 2 * RH * RDV) // gw
    kcol = qcol + NH // NHPS
    vcol = kcol + NH // NHPS
    tables = _na_tables(rpb)

    def case(b):
        return jnp.minimum(b, 1) + jnp.maximum(b - (_NA_NBLK - 2), 0)

    return pl.pallas_call(
        _na_kernel,
        grid=(NH // NHPS, _NA_NBLK),
        in_specs=[
            pl.BlockSpec((NQB, gw), lambda h, b: (b, qcol + h)),
            pl.BlockSpec((M, gw), lambda h, b: (0, kcol + h)),
            pl.BlockSpec((M, gw), lambda h, b: (0, vcol + h)),
            pl.BlockSpec((1, NHPS, NQB, NWIN), lambda h, b: (case(b), h, 0, 0)),
        ],
        out_specs=pl.BlockSpec((NQB, gw), lambda h, b: (b, h)),
        out_shape=jax.ShapeDtypeStruct((S, NH * HD), BF16),
        scratch_shapes=[pltpu.VMEM((M, gw), BF16), pltpu.VMEM((M, gw), BF16)],
        compiler_params=_cparams(("arbitrary", "arbitrary")),
        name="na",
    )(p, p, p, tables)


def _final_norm_kernel(x_ref, w_ref, o_ref):
    w = w_ref[...]

    def emit(rs, x):
        rinv = lax.rsqrt(jnp.mean(x * x, axis=-1, keepdims=True) + EPS)
        o_ref[rs, :] = x * rinv * w

    _norm_rows(x_ref, NORM_TR, emit)


def _final_norm(h, w):
    tr = NORM_TR
    return pl.pallas_call(
        _final_norm_kernel,
        grid=(S // tr,),
        in_specs=[pl.BlockSpec((tr, D), lambda i: (i, 0)), pl.BlockSpec((1, D), lambda i: (0, 0))],
        out_specs=pl.BlockSpec((tr, D), lambda i: (i, 0)),
        out_shape=jax.ShapeDtypeStruct((S, D), F32),
        compiler_params=_cparams(("arbitrary",)),
        name="final_norm",
    )(h, w.reshape(1, D))


def _tiles(rows):
    if rows == M:
        return dict(up=(2112, 1056), inp=(528, 528), down=(528, 528), outp=(1056, 1056))
    assert rows == S
    return dict(up=(2048, 1024), inp=(512, 512), down=(512, 512), outp=(1024, 1024))


def _ffn_half(h, rows, mods, layer, third, nw, wi, wo, ctx=None):
    t = _tiles(rows)
    if ctx is None:
        hn = _normmod(h, rows, nw, mods, layer, third)
    else:
        hn, h = _normmod(h, rows, nw, mods, layer, third, ctx=ctx)
    a = _mm(hn, wi, layer, swiglu=True, tm=t["up"][0], mm=t["up"][1], tn=512, out_dtype=BF16)
    return _mm_res(h, rows, mods, layer, 3 * third + 2, [a], wo, layer,
                   coef=0.5, tm=t["down"][0], mm=t["down"][1], tn=512)


def _in_proj(h, mods, layer, nw, w):
    tm, mm = _tiles(M)["inp"]
    hn = _normmod(h, M, nw, mods, layer, 1)
    return _mm(hn, w, 0, swiglu=False, tm=tm, mm=mm, tn=1536, out_dtype=F32)


def _out_proj(h, rows, mods, layer, parts, w):
    tm, mm = _tiles(rows)["outp"]
    return _mm_res(h, rows, mods, layer, 5, parts, w, 0, coef=1.0, tm=tm, mm=mm, tn=512)


def kernel(x, c, ctx, c_ctx, ada_w, ada_b, norm_w, ffn_a_wi, ffn_a_wo, ffn_b_wi, ffn_b_wo,
           ev_w_in, ev_w_out, ev_sink, ev_conv_w, ev_conv_b,
           od_w_in, od_w_out, od_decay_f, od_decay_b, od_gn_w, od_rpb, final_norm_w):
    assert x.shape == (1, S, D) and ctx.shape == (1, LC, D) and ada_w.shape[0] == 2
    cvec = jnp.concatenate([c, c_ctx[None, :], jnp.zeros((6, D), F32)], axis=0)
    mods = _ada(cvec, ada_w, ada_b)
    tabs = _rope_tables()

    h = _ffn_half(x[0], M, mods, 0, 0, norm_w[0, 0], ffn_a_wi, ffn_a_wo, ctx=ctx[0])
    p = _in_proj(h, mods, 0, norm_w[0, 1], ev_w_in)
    att = _attn(p, ev_sink[0], tabs)
    cnv = _conv(p, ev_conv_w[0], ev_conv_b[0])
    h = _out_proj(h, M, mods, 0, [att, cnv], ev_w_out)
    h = _ffn_half(h, M, mods, 0, 2, norm_w[0, 2], ffn_b_wi, ffn_b_wo)

    h = _ffn_half(h, M, mods, 1, 0, norm_w[1, 0], ffn_a_wi, ffn_a_wo)
    p = _in_proj(h, mods, 1, norm_w[1, 1], od_w_in)
    ret = _retention(p, od_decay_f[0], od_decay_b[0], od_gn_w[0], tabs)
    nat = _na(p, od_rpb[0])
    h = _out_proj(h, S, mods, 1, [ret, nat], od_w_out)
    h = _ffn_half(h, S, mods, 1, 2, norm_w[1, 2], ffn_b_wi, ffn_b_wo)
    return _final_norm(h, final_norm_w)[None]
```

```python
import functools

import numpy as np
import jax
import jax.numpy as jnp
from jax import lax
from jax.experimental import pallas as pl
from jax.experimental.pallas import tpu as pltpu

F32 = jnp.float32
BF16 = jnp.bfloat16

D = 2048
S = 8192
LC = 256
M = S + LC
GW = 64
GROWS = S // GW
HD = 128
DFF = 5632
NMOD = 9
EPS = 1e-6
ROPE_BASE = 10000.0
NEG = -1e30

AH, AKV, AWIN = 8, 2, 128
AQB = 256
CCH = 1024
EV_IN = AH * HD + 2 * AKV * HD + 3 * CCH
RH, RDK, RDV = 4, 128, 256
RC = 256
NH, NROWS, NCOLS = 8, 8, 16
NQROWS = 4
NQB = NQROWS * GW
NWIN_ROWS = NQROWS + NROWS
NWIN = NWIN_ROWS * GW
NHPS = 2
OD_IN = 2 * RH * RDK + 2 * RH * RDV + 3 * NH * HD

VMEM_LIMIT = 56 * 1024 * 1024


def _cparams(sem):
    return pltpu.CompilerParams(dimension_semantics=sem, vmem_limit_bytes=VMEM_LIMIT)


def _sub(r, size):
    return pl.ds(pl.multiple_of(r * size, 16), size)


def _ada_kernel(c_ref, w_ref, b_ref, o_ref):
    cv = c_ref[...]
    a = cv * jax.nn.sigmoid(cv)
    acc = jnp.dot(a.astype(BF16), w_ref[0].astype(BF16), preferred_element_type=F32)
    o_ref[0, 0] = acc + b_ref[0]


def _ada(cvec, ada_w, ada_b):
    depth = ada_w.shape[0]
    tn = 1024
    per = D // tn
    return pl.pallas_call(
        _ada_kernel,
        grid=(depth, NMOD * per),
        in_specs=[
            pl.BlockSpec((8, D), lambda l, j: (0, 0)),
            pl.BlockSpec((1, D, tn), lambda l, j: (l, 0, j)),
            pl.BlockSpec((1, 1, tn), lambda l, j: (l, 0, j)),
        ],
        out_specs=pl.BlockSpec((1, 1, 8, tn), lambda l, j: (l, j // per, 0, j % per)),
        out_shape=jax.ShapeDtypeStruct((depth, NMOD, 8, D), F32),
        compiler_params=_cparams(("arbitrary", "arbitrary")),
        name="ada",
    )(cvec, ada_w, ada_b.reshape(depth, 1, NMOD * D))


NORM_TR = 1024
NORM_CHUNK = 16
NORM_UNROLL = 4


def _norm_rows(src_ref, n_rows, fn, unroll=NORM_UNROLL):
    group = NORM_CHUNK * unroll

    def outer(gi, carry):
        for u in range(unroll):
            start = pl.multiple_of(gi * group + u * NORM_CHUNK, NORM_CHUNK)
            rs = pl.ds(start, NORM_CHUNK)
            fn(rs, src_ref[rs, :], start)
        return carry

    lax.fori_loop(0, n_rows // group, outer, 0)


def _normmod_kernel(*refs, rows, first):
    if first:
        x_ref, c_ref, nw_ref, mod_ref, o_ref, h_ref = refs
    else:
        x_ref, nw_ref, mod_ref, o_ref = refs
    i = pl.program_id(0)
    kind = (i * NORM_TR >= S).astype(jnp.int32)
    n_rows = jnp.minimum(rows - i * NORM_TR, NORM_TR)
    shift = mod_ref[0, 0, pl.ds(kind, 1), :]
    gain = nw_ref[...] * (1.0 + mod_ref[0, 1, pl.ds(kind, 1), :])

    def emit(rs, x, start):
        rinv = lax.rsqrt(jnp.mean(x * x, axis=-1, keepdims=True) + EPS)
        o_ref[rs, :] = ((x * rinv) * gain + shift).astype(o_ref.dtype)
        if first:
            h_ref[rs, :] = x

    if first:
        @pl.when(kind == 0)
        def _():
            _norm_rows(x_ref, n_rows, emit)

        @pl.when(kind == 1)
        def _():
            _norm_rows(c_ref, n_rows, emit)
    else:
        _norm_rows(x_ref, n_rows, emit)


def _normmod(h, rows, nw, mods, layer, third, ctx=None):
    first = ctx is not None
    assert S % NORM_TR == 0 and LC % (NORM_CHUNK * NORM_UNROLL) == 0 and LC <= NORM_TR
    nblk = pl.cdiv(rows, NORM_TR)
    row_spec = pl.BlockSpec((NORM_TR, D), lambda i: (i, 0))
    small = [pl.BlockSpec((1, D), lambda i: (0, 0)),
             pl.BlockSpec((1, 3, 8, D), lambda i: (layer, third, 0, 0))]
    if first:
        in_specs = [pl.BlockSpec((NORM_TR, D), lambda i: (jnp.minimum(i, S // NORM_TR - 1), 0)),
                    pl.BlockSpec((LC, D), lambda i: (0, 0))] + small
        args = (h, ctx, nw.reshape(1, D), mods)
        out_specs = [row_spec, row_spec]
        out_shape = [jax.ShapeDtypeStruct((rows, D), BF16), jax.ShapeDtypeStruct((rows, D), F32)]
    else:
        in_specs = [row_spec] + small
        args = (h, nw.reshape(1, D), mods)
        out_specs = row_spec
        out_shape = jax.ShapeDtypeStruct((rows, D), BF16)
    return pl.pallas_call(
        functools.partial(_normmod_kernel, rows=rows, first=first),
        grid=(nblk,),
        in_specs=in_specs,
        out_specs=out_specs,
        out_shape=out_shape,
        compiler_params=_cparams(("arbitrary",)),
        name="normmod_first" if first else "normmod",
    )(*args)


def _cast_weights(w_refs, wb_refs):
    @pl.when(pl.program_id(1) == 0)
    def _():
        for w_ref, wb_ref in zip(w_refs, wb_refs):
            wb_ref[...] = w_ref[...].astype(BF16)


def _mm_swiglu_kernel(a_ref, wg_ref, wu_ref, wo_ref, o_ref, wob_ref, wgb_ref, wub_ref, *, tm, mm):
    _cast_weights((wg_ref, wu_ref), (wgb_ref, wub_ref))
    wob_ref[...] = wo_ref[...].astype(BF16)

    def body(r, carry):
        a = a_ref[_sub(r, mm), :]
        g = jnp.dot(a, wgb_ref[...], preferred_element_type=F32)
        u = jnp.dot(a, wub_ref[...], preferred_element_type=F32)
        o_ref[_sub(r, mm), :] = (g * jax.nn.sigmoid(g) * u).astype(o_ref.dtype)
        return carry

    lax.fori_loop(0, tm // mm, body, 0)


def _mm_plain_kernel(a_ref, w_ref, o_ref, wb_ref, *, tm, mm):
    _cast_weights((w_ref,), (wb_ref,))

    def body(r, carry):
        acc = jnp.dot(a_ref[_sub(r, mm), :], wb_ref[...], preferred_element_type=F32)
        o_ref[_sub(r, mm), :] = acc.astype(o_ref.dtype)
        return carry

    lax.fori_loop(0, tm // mm, body, 0)


def _mm_swiglu(a, wi, wo, layer, *, tm, mm, tn):
    rows, k = a.shape
    assert rows % tm == 0 and tm % mm == 0 and mm % 16 == 0
    n_out = wi.shape[2] // 2
    nj, ni = n_out // tn, rows // tm
    assert wo.shape[1] % (nj * ni) == 0
    slab = wo.shape[1] // (nj * ni)
    assert slab % 16 == 0
    return pl.pallas_call(
        functools.partial(_mm_swiglu_kernel, tm=tm, mm=mm),
        grid=(nj, ni),
        in_specs=[pl.BlockSpec((tm, k), lambda j, i: (i, 0)),
                  pl.BlockSpec((None, k, tn), lambda j, i: (layer, 0, j)),
                  pl.BlockSpec((None, k, tn), lambda j, i: (layer, 0, j + nj)),
                  pl.BlockSpec((None, slab, D), lambda j, i: (layer, j * ni + i, 0))],
        out_specs=[pl.BlockSpec((tm, tn), lambda j, i: (i, j)),
                   pl.BlockSpec((slab, D), lambda j, i: (j * ni + i, 0))],
        out_shape=[jax.ShapeDtypeStruct((rows, n_out), BF16),
                   jax.ShapeDtypeStruct((wo.shape[1], D), BF16)],
        scratch_shapes=[pltpu.VMEM((k, tn), BF16)] * 2,
        compiler_params=_cparams(("arbitrary", "arbitrary")),
        name="mm_swiglu",
    )(a, wi, wi, wo)


def _mm_plain(a, w, layer, *, tm, mm, tn, out_dtype):
    rows, k = a.shape
    assert rows % tm == 0 and tm % mm == 0 and mm % 16 == 0
    return pl.pallas_call(
        functools.partial(_mm_plain_kernel, tm=tm, mm=mm),
        grid=(w.shape[2] // tn, rows // tm),
        in_specs=[pl.BlockSpec((tm, k), lambda j, i: (i, 0)),
                  pl.BlockSpec((None, k, tn), lambda j, i: (layer, 0, j))],
        out_specs=pl.BlockSpec((tm, tn), lambda j, i: (i, j)),
        out_shape=jax.ShapeDtypeStruct((rows, w.shape[2]), out_dtype),
        scratch_shapes=[pltpu.VMEM((k, tn), BF16)],
        compiler_params=_cparams(("arbitrary", "arbitrary")),
        name="mm_plain",
    )(a, w)


def _gate_rows(gate_ref, i, tm):
    rows = i * tm + lax.broadcasted_iota(jnp.int32, (tm, 1), 0)
    return jnp.where(rows >= S, gate_ref[0, 0, 1:2, :], gate_ref[0, 0, 0:1, :])


def _ffn_down_kernel(h_ref, gate_ref, a_ref, w_ref, o_ref, *, tm):
    acc = jnp.dot(a_ref[...], w_ref[...], preferred_element_type=F32)
    o_ref[...] = h_ref[...] + (0.5 * _gate_rows(gate_ref, pl.program_id(1), tm)) * acc


def _ffn_down(h, rows, mods, layer, gate_idx, a, wb, *, tm, tn):
    k = a.shape[1]
    assert rows % tm == 0
    return pl.pallas_call(
        functools.partial(_ffn_down_kernel, tm=tm),
        grid=(D // tn, rows // tm),
        in_specs=[pl.BlockSpec((tm, tn), lambda j, i: (i, j)),
                  pl.BlockSpec((1, 1, 8, tn), lambda j, i: (layer, gate_idx, 0, j)),
                  pl.BlockSpec((tm, k), lambda j, i: (i, 0)),
                  pl.BlockSpec((k, tn), lambda j, i: (0, j))],
        out_specs=pl.BlockSpec((tm, tn), lambda j, i: (i, j)),
        out_shape=jax.ShapeDtypeStruct((rows, D), F32),
        compiler_params=_cparams(("arbitrary", "arbitrary")),
        name="ffn_down",
    )(h, mods, a, wb)


def _cast_kernel(w_ref, o_ref):
    o_ref[...] = w_ref[...].astype(o_ref.dtype)


def _cast_bf16(w, layer):
    k, n = w.shape[1:]
    tk = 512
    return pl.pallas_call(
        _cast_kernel,
        grid=(k // tk,),
        in_specs=[pl.BlockSpec((None, tk, n), lambda i: (layer, i, 0))],
        out_specs=pl.BlockSpec((tk, n), lambda i: (i, 0)),
        out_shape=jax.ShapeDtypeStruct((k, n), BF16),
        compiler_params=_cparams(("arbitrary",)),
        name="cast_bf16",
    )(w)


def _out_proj_kernel(h_ref, gate_ref, a1_ref, a2_ref, w1_ref, w2_ref, nw_ref, mod_ref,
                     o_ref, hn_ref, *, tm, unroll):
    i = pl.program_id(0)
    acc = (jnp.dot(a1_ref[...], w1_ref[...], preferred_element_type=F32)
           + jnp.dot(a2_ref[...], w2_ref[...], preferred_element_type=F32))
    o_ref[...] = h_ref[...] + _gate_rows(gate_ref, i, tm) * acc
    nw = nw_ref[...]
    gain_lat = nw * (1.0 + mod_ref[0, 1, 0:1, :])
    gain_ctx = nw * (1.0 + mod_ref[0, 1, 1:2, :])
    shift_lat = mod_ref[0, 0, 0:1, :]
    shift_ctx = mod_ref[0, 0, 1:2, :]

    def emit(rs, x, start):
        isc = (i * tm + start + lax.broadcasted_iota(jnp.int32, (NORM_CHUNK, 1), 0)) >= S
        rinv = lax.rsqrt(jnp.mean(x * x, axis=-1, keepdims=True) + EPS)
        gain = jnp.where(isc, gain_ctx, gain_lat)
        shift = jnp.where(isc, shift_ctx, shift_lat)
        hn_ref[rs, :] = ((x * rinv) * gain + shift).astype(hn_ref.dtype)

    _norm_rows(o_ref, tm, emit, unroll=unroll)


def _out_proj(h, rows, mods, layer, a1, a2, wb, nw_next, *, tm):
    k = a1.shape[1]
    assert a2.shape[1] == k and wb.shape == (2 * k, D) and rows % tm == 0
    unroll = 4 if tm % (4 * NORM_CHUNK) == 0 else 3
    row = lambda i: (i, 0)
    return pl.pallas_call(
        functools.partial(_out_proj_kernel, tm=tm, unroll=unroll),
        grid=(rows // tm,),
        in_specs=[pl.BlockSpec((tm, D), row),
                  pl.BlockSpec((1, 1, 8, D), lambda i: (layer, 5, 0, 0)),
                  pl.BlockSpec((tm, k), row),
                  pl.BlockSpec((tm, k), row),
                  pl.BlockSpec((k, D), lambda i: (0, 0)),
                  pl.BlockSpec((k, D), lambda i: (1, 0)),
                  pl.BlockSpec((1, D), lambda i: (0, 0)),
                  pl.BlockSpec((1, 3, 8, D), lambda i: (layer, 2, 0, 0))],
        out_specs=[pl.BlockSpec((tm, D), row), pl.BlockSpec((tm, D), row)],
        out_shape=[jax.ShapeDtypeStruct((rows, D), F32), jax.ShapeDtypeStruct((rows, D), BF16)],
        compiler_params=_cparams(("arbitrary",)),
        name="out_proj",
    )(h, mods, a1, a2, wb, wb, nw_next.reshape(1, D), mods)


def _rope_tables():
    t = np.arange(S)
    rows = (t // GW).astype(np.float32)
    cols = (t % GW).astype(np.float32)
    half = HD // 2
    inv = np.float32(ROPE_BASE) ** (-np.arange(0, half, 2, dtype=np.float32) / np.float32(half))
    ar = rows[:, None] * inv
    ac = cols[:, None] * inv
    cos = np.concatenate([np.cos(ar), np.cos(ar), np.cos(ac), np.cos(ac)], axis=-1)
    sin = np.concatenate([-np.sin(ar), np.sin(ar), -np.sin(ac), np.sin(ac)], axis=-1)
    lat = np.concatenate([cos, sin], axis=-1)
    ctx = np.concatenate([np.ones((LC, HD), np.float32), np.zeros((LC, HD), np.float32)], axis=-1)
    return jnp.asarray(np.concatenate([lat, ctx], axis=0).astype(np.float32))


def _rope(x, tab):
    c = tab[:, :HD]
    sg = tab[:, HD:]
    lane = lax.broadcasted_iota(jnp.int32, x.shape, 1)
    first = (lane % (HD // 2)) < (HD // 4)
    partner = jnp.where(first, pltpu.roll(x, HD - HD // 4, 1), pltpu.roll(x, HD // 4, 1))
    return x * c + partner * sg


def _attn_kernel(sink_ref, q_ref, kp_ref, kc_ref, kn_ref, vp_ref, vc_ref, vn_ref, kx_ref, vx_ref,
                 tp_ref, tc_ref, tn_ref, o_ref):
    g = pl.program_id(0)
    n = pl.program_id(1)
    nlat = S // AQB
    scale = HD ** -0.5
    k_all = jnp.concatenate([
        _rope(kp_ref[...], tp_ref[...]).astype(BF16),
        _rope(kc_ref[...], tc_ref[...]).astype(BF16),
        _rope(kn_ref[...], tn_ref[...]).astype(BF16),
        kx_ref[...].astype(BF16)], axis=0)
    v_all = jnp.concatenate([vp_ref[...].astype(BF16), vc_ref[...].astype(BF16),
                             vn_ref[...].astype(BF16), vx_ref[...].astype(BF16)], axis=0)
    nloc = AQB + 2 * AWIN
    nk = nloc + LC
    r = lax.broadcasted_iota(jnp.int32, (AQB, nk), 0)
    c = lax.broadcasted_iota(jnp.int32, (AQB, nk), 1)
    rel = c - AWIN - r
    pos = n * AQB - AWIN + c
    hi = jnp.where(n < nlat, S, 0)
    ok = ((jnp.abs(rel) <= AWIN) & (pos >= 0) & (pos < hi)) | (c >= nloc)
    tq = tc_ref[...]
    for hh in range(AH // AKV):
        q = _rope(q_ref[:, hh * HD:(hh + 1) * HD], tq).astype(BF16)
        s = lax.dot_general(q, k_all, (((1,), (1,)), ((), ())), preferred_element_type=F32) * scale
        s = jnp.where(ok, s, NEG)
        sk = sink_ref[g * (AH // AKV) + hh]
        m = jnp.maximum(jnp.max(s, axis=-1, keepdims=True), sk)
        e = jnp.exp(s - m)
        den = jnp.sum(e, axis=-1, keepdims=True) + jnp.exp(sk - m)
        o = jnp.dot(e.astype(BF16), v_all, preferred_element_type=F32)
        o_ref[:, hh * HD:(hh + 1) * HD] = (o / den).astype(o_ref.dtype)


def _attn(p, sink, tabs):
    assert AQB == LC and AQB == 2 * AWIN
    nlat = S // AQB
    qw = (AH // AKV) * HD
    kcol = AH
    vcol = kcol + AKV
    last_w = S // AWIN - 1

    def prev(n):
        return jnp.clip(2 * n - 1, 0, last_w)

    def cur(n):
        return jnp.minimum(n, nlat - 1)

    def nxt(n):
        return jnp.clip(2 * n + 2, 0, last_w)

    return pl.pallas_call(
        _attn_kernel,
        grid=(AKV, M // AQB),
        in_specs=[
            pl.BlockSpec(memory_space=pltpu.SMEM),
            pl.BlockSpec((AQB, qw), lambda g, n: (n, g)),
            pl.BlockSpec((AWIN, HD), lambda g, n: (prev(n), kcol + g)),
            pl.BlockSpec((AQB, HD), lambda g, n: (cur(n), kcol + g)),
            pl.BlockSpec((AWIN, HD), lambda g, n: (nxt(n), kcol + g)),
            pl.BlockSpec((AWIN, HD), lambda g, n: (prev(n), vcol + g)),
            pl.BlockSpec((AQB, HD), lambda g, n: (cur(n), vcol + g)),
            pl.BlockSpec((AWIN, HD), lambda g, n: (nxt(n), vcol + g)),
            pl.BlockSpec((LC, HD), lambda g, n: (S // LC, kcol + g)),
            pl.BlockSpec((LC, HD), lambda g, n: (S // LC, vcol + g)),
            pl.BlockSpec((AWIN, 2 * HD), lambda g, n: (prev(n), 0)),
            pl.BlockSpec((AQB, 2 * HD), lambda g, n: (n, 0)),
            pl.BlockSpec((AWIN, 2 * HD), lambda g, n: (nxt(n), 0)),
        ],
        out_specs=pl.BlockSpec((AQB, qw), lambda g, n: (n, g)),
        out_shape=jax.ShapeDtypeStruct((M, AH * HD), BF16),
        compiler_params=_cparams(("arbitrary", "arbitrary")),
        name="attn",
    )(sink, p, p, p, p, p, p, p, p, p, tabs, tabs, tabs)


def _conv_kernel(b_ref, c_ref, u_ref, cp_ref, up_ref, cn_ref, un_ref, w_ref, bias_ref, o_ref, *, tr):
    i = pl.program_id(0)
    lat_blocks = S // tr
    z = c_ref[...] * u_ref[...]
    zp = cp_ref[7:8, :] * up_ref[7:8, :]
    zn = cn_ref[0:1, :] * un_ref[0:1, :]
    has_prev = jnp.logical_and(i != 0, i != lat_blocks)
    has_next = jnp.logical_and(i != lat_blocks - 1, i != M // tr - 1)
    zp = jnp.where(has_prev, zp, 0.0)
    zn = jnp.where(has_next, zn, 0.0)
    row = lax.broadcasted_iota(jnp.int32, z.shape, 0)
    z_m1 = jnp.where(row == 0, zp, pltpu.roll(z, 1, 0))
    z_p1 = jnp.where(row == tr - 1, zn, pltpu.roll(z, tr - 1, 0))
    conv = z_m1 * w_ref[0:1, :] + z * w_ref[1:2, :] + z_p1 * w_ref[2:3, :] + bias_ref[...]
    o_ref[...] = (b_ref[...] * conv).astype(o_ref.dtype)


def _conv(p, conv_w, conv_b):
    tr, tc = 256, 512
    assert LC % tr == 0 and S % tr == 0
    base = (AH * HD + 2 * AKV * HD) // tc
    per = CCH // tc
    nrb = M // tr
    h8 = tr // 8

    def prev8(i):
        return jnp.maximum(i * h8 - 1, 0)

    def next8(i):
        return jnp.minimum((i + 1) * h8, M // 8 - 1)

    return pl.pallas_call(
        functools.partial(_conv_kernel, tr=tr),
        grid=(nrb, per),
        in_specs=[
            pl.BlockSpec((tr, tc), lambda i, j: (i, base + j)),
            pl.BlockSpec((tr, tc), lambda i, j: (i, base + per + j)),
            pl.BlockSpec((tr, tc), lambda i, j: (i, base + 2 * per + j)),
            pl.BlockSpec((8, tc), lambda i, j: (prev8(i), base + per + j)),
            pl.BlockSpec((8, tc), lambda i, j: (prev8(i), base + 2 * per + j)),
            pl.BlockSpec((8, tc), lambda i, j: (next8(i), base + per + j)),
            pl.BlockSpec((8, tc), lambda i, j: (next8(i), base + 2 * per + j)),
            pl.BlockSpec((3, tc), lambda i, j: (0, j)),
            pl.BlockSpec((1, tc), lambda i, j: (0, j)),
        ],
        out_specs=pl.BlockSpec((tr, tc), lambda i, j: (i, j)),
        out_shape=jax.ShapeDtypeStruct((M, CCH), BF16),
        compiler_params=_cparams(("arbitrary", "arbitrary")),
        name="conv",
    )(p, p, p, p, p, p, p, conv_w, conv_b.reshape(1, CCH))


def _log_sigmoid(d):
    return jnp.minimum(d, 0.0) - jnp.log(1.0 + jnp.exp(-jnp.abs(d)))


def _ret_chunk(decay_ref, q_ref, k_ref, v_ref, tab_ref, st_ref, h, backward):
    lg = _log_sigmoid(jnp.full((1, 1), decay_ref[h], F32))
    tab = tab_ref[...]
    q = _rope(q_ref[:, h * RDK:(h + 1) * RDK], tab).astype(BF16)
    k = (_rope(k_ref[:, h * RDK:(h + 1) * RDK], tab) * (RDK ** -0.5)).astype(BF16)
    v = v_ref[:, h * RDV:(h + 1) * RDV]
    ii = lax.broadcasted_iota(jnp.int32, (RC, RC), 0)
    jj = lax.broadcasted_iota(jnp.int32, (RC, RC), 1)
    ic = lax.broadcasted_iota(jnp.int32, (RC, 1), 0).astype(F32)
    if backward:
        rel = jj - ii
        keep = rel > 0
        xi = jnp.exp(lg * (RC - ic))
        zeta = jnp.exp(lg * ic)
    else:
        rel = ii - jj
        keep = rel >= 0
        xi = jnp.exp(lg * (ic + 1.0))
        zeta = jnp.exp(lg * (RC - 1.0 - ic))
    dmask = jnp.where(keep, jnp.exp(lg * jnp.where(keep, rel, 0).astype(F32)), 0.0)
    inner = lax.dot_general(q, k, (((1,), (1,)), ((), ())), preferred_element_type=F32) * dmask
    y = jnp.dot(inner.astype(BF16), v.astype(BF16), preferred_element_type=F32)
    st = st_ref[h]
    y = y + jnp.dot(q, st.astype(BF16), preferred_element_type=F32) * xi
    kv = lax.dot_general(k, (v * zeta).astype(BF16), (((0,), (0,)), ((), ())),
                         preferred_element_type=F32)
    st_ref[h] = jnp.exp(lg * float(RC)) * st + kv
    return y


def _ret_fwd_kernel(decay_ref, q_ref, k_ref, v_ref, tab_ref, y_ref, st_ref):
    @pl.when(pl.program_id(0) == 0)
    def _():
        st_ref[...] = jnp.zeros_like(st_ref)

    for h in range(RH):
        y_ref[:, h * RDV:(h + 1) * RDV] = _ret_chunk(decay_ref, q_ref, k_ref, v_ref, tab_ref,
                                                     st_ref, h, False)


def _ret_bwd_kernel(decay_ref, q_ref, k_ref, v_ref, tab_ref, yf_ref, g_ref, gnw_ref, o_ref, st_ref):
    @pl.when(pl.program_id(0) == 0)
    def _():
        st_ref[...] = jnp.zeros_like(st_ref)

    for h in range(RH):
        sl = slice(h * RDV, (h + 1) * RDV)
        y = yf_ref[:, sl] + _ret_chunk(decay_ref, q_ref, k_ref, v_ref, tab_ref, st_ref, h, True)
        mu = jnp.mean(y, axis=-1, keepdims=True)
        yc = y - mu
        var = jnp.mean(yc * yc, axis=-1, keepdims=True)
        yn = yc * lax.rsqrt(var + EPS) * gnw_ref[:, sl]
        gt = g_ref[:, sl]
        o_ref[:, sl] = (gt * jax.nn.sigmoid(gt) * yn).astype(o_ref.dtype)


def _retention(p, decay_f, decay_b, gn_w, tabs):
    nch = M // RC
    nlat = S // RC
    qw, vw = RH * RDK, RH * RDV
    fwd = lambda s: ((s + nlat) % nch)
    bwd = lambda s: (nch - 1 - s)
    smem = pl.BlockSpec(memory_space=pltpu.SMEM)

    def specs(cm):
        return [smem,
                pl.BlockSpec((RC, qw), lambda s: (cm(s), 0)),
                pl.BlockSpec((RC, qw), lambda s: (cm(s), 1)),
                pl.BlockSpec((RC, vw), lambda s: (cm(s), 1)),
                pl.BlockSpec((RC, 2 * HD), lambda s: (cm(s), 0))]

    state = pltpu.VMEM((RH, RDK, RDV), F32)
    yf = pl.pallas_call(
        _ret_fwd_kernel,
        grid=(nch,),
        in_specs=specs(fwd),
        out_specs=pl.BlockSpec((RC, vw), lambda s: (fwd(s), 0)),
        out_shape=jax.ShapeDtypeStruct((M, vw), F32),
        scratch_shapes=[state],
        compiler_params=_cparams(("arbitrary",)),
        name="ret_fwd",
    )(decay_f, p, p, p, tabs)
    return pl.pallas_call(
        _ret_bwd_kernel,
        grid=(nch,),
        in_specs=specs(bwd) + [pl.BlockSpec((RC, vw), lambda s: (bwd(s), 0)),
                               pl.BlockSpec((RC, vw), lambda s: (bwd(s), 2)),
                               pl.BlockSpec((1, vw), lambda s: (0, 0))],
        out_specs=pl.BlockSpec((RC, vw), lambda s: (bwd(s), 0)),
        out_shape=jax.ShapeDtypeStruct((M, vw), BF16),
        scratch_shapes=[state],
        compiler_params=_cparams(("arbitrary",)),
        name="ret_bwd",
    )(decay_b, p, p, p, tabs, yf, p, gn_w.reshape(1, vw))


_NA_NBLK = S // NQB
_NA_CASES = (0, 1, _NA_NBLK - 1)


def _na_window_row(b):
    return jnp.clip(NQROWS * b - NROWS // 2, 0, GROWS - NWIN_ROWS)


def _na_table_kernel(rpb_ref, o_ref):
    h = pl.program_id(0)
    n_dr, n_dc = 2 * NROWS - 1, 2 * NCOLS - 1
    cq = lax.broadcasted_iota(jnp.int32, (GW, 2 * GW), 0)
    lane = lax.broadcasted_iota(jnp.int32, (GW, 2 * GW), 1)
    right = lane >= GW
    kc = jnp.where(right, lane - GW, lane)
    dc = kc - cq + (NCOLS - 1)
    cs = jnp.clip(cq - NCOLS // 2, 0, GW - NCOLS)
    col_ok = (kc >= cs) & (kc < cs + NCOLS)
    memo = {}

    def row_scalar(a, b):
        return rpb_ref[(h * n_dr + a) * n_dc + b] if 0 <= a < n_dr else 0.0

    def pair(a0):
        if a0 not in memo:
            acc = jnp.zeros((GW, 2 * GW), F32)
            for b in range(n_dc):
                vec = jnp.where(right, row_scalar(a0 + 1, b), row_scalar(a0, b))
                acc = jnp.where(dc == b, vec, acc)
            memo[a0] = acc
        return memo[a0]

    for ci, blk in enumerate(_NA_CASES):
        w = int(np.clip(NQROWS * blk - NROWS // 2, 0, GROWS - NWIN_ROWS))
        for rl in range(NQROWS):
            r = NQROWS * blk + rl
            rs = int(np.clip(r - NROWS // 2, 0, GROWS - NROWS))
            for pr in range(NWIN_ROWS // 2):
                kr = w + 2 * pr
                ok_l = rs <= kr < rs + NROWS
                ok_r = rs <= kr + 1 < rs + NROWS
                if ok_l and ok_r:
                    ok = col_ok
                elif ok_l:
                    ok = col_ok & jnp.logical_not(right)
                elif ok_r:
                    ok = col_ok & right
                else:
                    ok = None
                if ok is None:
                    tile = jnp.full((GW, 2 * GW), NEG, F32)
                else:
                    tile = jnp.where(ok, pair(kr - r + NROWS - 1), NEG)
                o_ref[ci, 0, rl * GW:(rl + 1) * GW, pr * 2 * GW:(pr + 1) * 2 * GW] = tile


def _na_tables(rpb):
    return pl.pallas_call(
        _na_table_kernel,
        grid=(NH,),
        in_specs=[pl.BlockSpec(memory_space=pltpu.SMEM)],
        out_specs=pl.BlockSpec((len(_NA_CASES), 1, NQB, NWIN), lambda h: (0, h, 0, 0)),
        out_shape=jax.ShapeDtypeStruct((len(_NA_CASES), NH, NQB, NWIN), F32),
        compiler_params=_cparams(("arbitrary",)),
        name="na_tables",
    )(rpb.reshape(-1))


def _na_kernel(q_ref, k_ref, v_ref, t_ref, o_ref, kb_ref, vb_ref):
    b = pl.program_id(1)

    @pl.when(b == 0)
    def _():
        kb_ref[...] = k_ref[...].astype(BF16)
        vb_ref[...] = v_ref[...].astype(BF16)

    scale = HD ** -0.5
    start = pl.multiple_of(_na_window_row(b) * GW, NQB)
    nt = (((1,), (1,)), ((), ()))
    for hh in range(NHPS):
        cs = slice(hh * HD, (hh + 1) * HD)
        q = q_ref[:, cs].astype(BF16)
        s_loc = lax.dot_general(q, kb_ref[pl.ds(start, NWIN), cs], nt,
                                preferred_element_type=F32) * scale + t_ref[0, hh]
        s_ctx = lax.dot_general(q, kb_ref[S:M, cs], nt, preferred_element_type=F32) * scale
        m = jnp.maximum(jnp.max(s_loc, axis=-1, keepdims=True), jnp.max(s_ctx, axis=-1, keepdims=True))
        e_loc = jnp.exp(s_loc - m)
        e_ctx = jnp.exp(s_ctx - m)
        den = jnp.sum(e_loc, axis=-1, keepdims=True) + jnp.sum(e_ctx, axis=-1, keepdims=True)
        o = (jnp.dot(e_loc.astype(BF16), vb_ref[pl.ds(start, NWIN), cs], preferred_element_type=F32)
             + jnp.dot(e_ctx.astype(BF16), vb_ref[S:M, cs], preferred_element_type=F32))
        o_ref[:, cs] = (o / den).astype(o_ref.dtype)


def _na(p, rpb):
    gw = NHPS * HD
    qcol = (2 * RH * RDK + 2 * RH * RDV) // gw
    kcol = qcol + NH // NHPS
    vcol = kcol + NH // NHPS
    tables = _na_tables(rpb)

    def case(b):
        return jnp.minimum(b, 1) + jnp.maximum(b - (_NA_NBLK - 2), 0)

    return pl.pallas_call(
        _na_kernel,
        grid=(NH // NHPS, _NA_NBLK),
        in_specs=[
            pl.BlockSpec((NQB, gw), lambda h, b: (b, qcol + h)),
            pl.BlockSpec((M, gw), lambda h, b: (0, kcol + h)),
            pl.BlockSpec((M, gw), lambda h, b: (0, vcol + h)),
            pl.BlockSpec((1, NHPS, NQB, NWIN), lambda h, b: (case(b), h, 0, 0)),
        ],
        out_specs=pl.BlockSpec((NQB, gw), lambda h, b: (b, h)),
        out_shape=jax.ShapeDtypeStruct((S, NH * HD), BF16),
        scratch_shapes=[pltpu.VMEM((M, gw), BF16), pltpu.VMEM((M, gw), BF16)],
        compiler_params=_cparams(("arbitrary", "arbitrary")),
        name="na",
    )(p, p, p, tables)


def _final_norm_kernel(x_ref, w_ref, o_ref):
    w = w_ref[...]

    def emit(rs, x, start):
        rinv = lax.rsqrt(jnp.mean(x * x, axis=-1, keepdims=True) + EPS)
        o_ref[rs, :] = x * rinv * w

    _norm_rows(x_ref, NORM_TR, emit)


def _final_norm(h, w):
    tr = NORM_TR
    return pl.pallas_call(
        _final_norm_kernel,
        grid=(S // tr,),
        in_specs=[pl.BlockSpec((tr, D), lambda i: (i, 0)), pl.BlockSpec((1, D), lambda i: (0, 0))],
        out_specs=pl.BlockSpec((tr, D), lambda i: (i, 0)),
        out_shape=jax.ShapeDtypeStruct((S, D), F32),
        compiler_params=_cparams(("arbitrary",)),
        name="final_norm",
    )(h, w.reshape(1, D))


def _tiles(rows):
    if rows == M:
        return dict(up=(2112, 1056), small=528)
    assert rows == S
    return dict(up=(2048, 1024), small=512)


def _ffn_half(h, rows, mods, layer, third, nw, wi, wo, hn=None, ctx=None):
    t = _tiles(rows)
    if ctx is not None:
        hn, h = _normmod(h, rows, nw, mods, layer, third, ctx=ctx)
    elif hn is None:
        hn = _normmod(h, rows, nw, mods, layer, third)
    a, wob = _mm_swiglu(hn, wi, wo, layer, tm=t["up"][0], mm=t["up"][1], tn=512)
    return _ffn_down(h, rows, mods, layer, 3 * third + 2, a, wob, tm=t["small"], tn=1024)


def _in_proj(h, mods, layer, nw, w):
    tm = _tiles(M)["small"]
    hn = _normmod(h, M, nw, mods, layer, 1)
    return _mm_plain(hn, w, 0, tm=tm, mm=tm, tn=1536, out_dtype=F32)


def kernel(x, c, ctx, c_ctx, ada_w, ada_b, norm_w, ffn_a_wi, ffn_a_wo, ffn_b_wi, ffn_b_wo,
           ev_w_in, ev_w_out, ev_sink, ev_conv_w, ev_conv_b,
           od_w_in, od_w_out, od_decay_f, od_decay_b, od_gn_w, od_rpb, final_norm_w):
    assert x.shape == (1, S, D) and ctx.shape == (1, LC, D) and ada_w.shape[0] == 2
    cvec = jnp.concatenate([c, c_ctx[None, :], jnp.zeros((6, D), F32)], axis=0)
    mods = _ada(cvec, ada_w, ada_b)
    tabs = _rope_tables()

    h = _ffn_half(x[0], M, mods, 0, 0, norm_w[0, 0], ffn_a_wi, ffn_a_wo, ctx=ctx[0])
    p = _in_proj(h, mods, 0, norm_w[0, 1], ev_w_in)
    att = _attn(p, ev_sink[0], tabs)
    cnv = _conv(p, ev_conv_w[0], ev_conv_b[0])
    h, hn = _out_proj(h, M, mods, 0, att, cnv, _cast_bf16(ev_w_out, 0), norm_w[0, 2],
                      tm=_tiles(M)["small"])
    h = _ffn_half(h, M, mods, 0, 2, norm_w[0, 2], ffn_b_wi, ffn_b_wo, hn=hn)

    h = _ffn_half(h, M, mods, 1, 0, norm_w[1, 0], ffn_a_wi, ffn_a_wo)
    p = _in_proj(h, mods, 1, norm_w[1, 1], od_w_in)
    ret = _retention(p, od_decay_f[0], od_decay_b[0], od_gn_w[0], tabs)
    nat = _na(p, od_rpb[0])
    h, hn = _out_proj(h, S, mods, 1, ret, nat, _cast_bf16(od_w_out, 0), norm_w[1, 2],
                      tm=_tiles(S)["small"])
    h = _ffn_half(h, S, mods, 1, 2, norm_w[1, 2], ffn_b_wi, ffn_b_wo, hn=hn)
    return _final_norm(h, final_norm_w)[None]
```

```python
import functools

import numpy as np
import jax
import jax.numpy as jnp
from jax import lax
from jax.experimental import pallas as pl
from jax.experimental.pallas import tpu as pltpu

F32 = jnp.float32
BF16 = jnp.bfloat16

D = 2048
S = 8192
LC = 256
M = S + LC
GW = 64
GROWS = S // GW
HD = 128
DFF = 5632
NMOD = 9
EPS = 1e-6
ROPE_BASE = 10000.0
NEG = -1e30

AH, AKV, AWIN = 8, 2, 128
AQB = 256
CCH = 1024
EV_IN = AH * HD + 2 * AKV * HD + 3 * CCH
RH, RDK, RDV = 4, 128, 256
RC = 256
NH, NROWS, NCOLS = 8, 8, 16
NQROWS = 4
NQB = NQROWS * GW
NWIN_ROWS = NQROWS + NROWS
NWIN = NWIN_ROWS * GW
NHPS = 4
LOG2E = float(np.log2(np.e))
OD_IN = 2 * RH * RDK + 2 * RH * RDV + 3 * NH * HD

VMEM_LIMIT = 56 * 1024 * 1024


def _cparams(sem):
    return pltpu.CompilerParams(dimension_semantics=sem, vmem_limit_bytes=VMEM_LIMIT)


def _sub(r, size):
    return pl.ds(pl.multiple_of(r * size, 16), size)


def _ada_kernel(c_ref, w_ref, b_ref, o_ref):
    cv = c_ref[...]
    a = cv * jax.nn.sigmoid(cv)
    acc = jnp.dot(a.astype(BF16), w_ref[0].astype(BF16), preferred_element_type=F32)
    o_ref[0, 0] = acc + b_ref[0]


def _ada(cvec, ada_w, ada_b):
    depth = ada_w.shape[0]
    tn = 1024
    per = D // tn
    return pl.pallas_call(
        _ada_kernel,
        grid=(depth, NMOD * per),
        in_specs=[
            pl.BlockSpec((8, D), lambda l, j: (0, 0)),
            pl.BlockSpec((1, D, tn), lambda l, j: (l, 0, j)),
            pl.BlockSpec((1, 1, tn), lambda l, j: (l, 0, j)),
        ],
        out_specs=pl.BlockSpec((1, 1, 8, tn), lambda l, j: (l, j // per, 0, j % per)),
        out_shape=jax.ShapeDtypeStruct((depth, NMOD, 8, D), F32),
        compiler_params=_cparams(("arbitrary", "arbitrary")),
        name="ada",
    )(cvec, ada_w, ada_b.reshape(depth, 1, NMOD * D))


NORM_TR = 1024
NORM_CHUNK = 16
NORM_UNROLL = 4


def _norm_rows(src_ref, n_rows, fn, unroll=NORM_UNROLL):
    group = NORM_CHUNK * unroll

    def outer(gi, carry):
        for u in range(unroll):
            start = pl.multiple_of(gi * group + u * NORM_CHUNK, NORM_CHUNK)
            rs = pl.ds(start, NORM_CHUNK)
            fn(rs, src_ref[rs, :], start)
        return carry

    lax.fori_loop(0, n_rows // group, outer, 0)


def _normmod_kernel(*refs, rows, first):
    if first:
        x_ref, c_ref, nw_ref, mod_ref, o_ref, h_ref = refs
    else:
        x_ref, nw_ref, mod_ref, o_ref = refs
    i = pl.program_id(0)
    kind = (i * NORM_TR >= S).astype(jnp.int32)
    n_rows = jnp.minimum(rows - i * NORM_TR, NORM_TR)
    shift = mod_ref[0, 0, pl.ds(kind, 1), :]
    gain = nw_ref[...] * (1.0 + mod_ref[0, 1, pl.ds(kind, 1), :])

    def emit(rs, x, start):
        rinv = lax.rsqrt(jnp.mean(x * x, axis=-1, keepdims=True) + EPS)
        o_ref[rs, :] = ((x * rinv) * gain + shift).astype(o_ref.dtype)
        if first:
            h_ref[rs, :] = x

    if first:
        @pl.when(kind == 0)
        def _():
            _norm_rows(x_ref, n_rows, emit)

        @pl.when(kind == 1)
        def _():
            _norm_rows(c_ref, n_rows, emit)
    else:
        _norm_rows(x_ref, n_rows, emit)


def _normmod(h, rows, nw, mods, layer, third, ctx=None):
    first = ctx is not None
    assert S % NORM_TR == 0 and LC % (NORM_CHUNK * NORM_UNROLL) == 0 and LC <= NORM_TR
    nblk = pl.cdiv(rows, NORM_TR)
    row_spec = pl.BlockSpec((NORM_TR, D), lambda i: (i, 0))
    small = [pl.BlockSpec((1, D), lambda i: (0, 0)),
             pl.BlockSpec((1, 3, 8, D), lambda i: (layer, third, 0, 0))]
    if first:
        in_specs = [pl.BlockSpec((NORM_TR, D), lambda i: (jnp.minimum(i, S // NORM_TR - 1), 0)),
                    pl.BlockSpec((LC, D), lambda i: (0, 0))] + small
        args = (h, ctx, nw.reshape(1, D), mods)
        out_specs = [row_spec, row_spec]
        out_shape = [jax.ShapeDtypeStruct((rows, D), BF16), jax.ShapeDtypeStruct((rows, D), F32)]
    else:
        in_specs = [row_spec] + small
        args = (h, nw.reshape(1, D), mods)
        out_specs = row_spec
        out_shape = jax.ShapeDtypeStruct((rows, D), BF16)
    return pl.pallas_call(
        functools.partial(_normmod_kernel, rows=rows, first=first),
        grid=(nblk,),
        in_specs=in_specs,
        out_specs=out_specs,
        out_shape=out_shape,
        compiler_params=_cparams(("arbitrary",)),
        name="normmod_first" if first else "normmod",
    )(*args)


def _cast_weights(w_refs, wb_refs):
    @pl.when(pl.program_id(1) == 0)
    def _():
        for w_ref, wb_ref in zip(w_refs, wb_refs):
            wb_ref[...] = w_ref[...].astype(BF16)


def _mm_swiglu_kernel(a_ref, wg_ref, wu_ref, wo_ref, o_ref, wob_ref, wgb_ref, wub_ref, *, tm, mm):
    _cast_weights((wg_ref, wu_ref), (wgb_ref, wub_ref))
    wob_ref[...] = wo_ref[...].astype(BF16)

    def body(r, carry):
        a = a_ref[_sub(r, mm), :]
        g = jnp.dot(a, wgb_ref[...], preferred_element_type=F32)
        u = jnp.dot(a, wub_ref[...], preferred_element_type=F32)
        o_ref[_sub(r, mm), :] = (g * jax.nn.sigmoid(g) * u).astype(o_ref.dtype)
        return carry

    lax.fori_loop(0, tm // mm, body, 0)


def _mm_plain_kernel(a_ref, w_ref, o_ref):
    o_ref[...] = jnp.dot(a_ref[...], w_ref[...], preferred_element_type=F32).astype(o_ref.dtype)


def _mm_swiglu(a, wi, wo, layer, *, tm, mm, tn):
    rows, k = a.shape
    assert rows % tm == 0 and tm % mm == 0 and mm % 16 == 0
    n_out = wi.shape[2] // 2
    nj, ni = n_out // tn, rows // tm
    assert wo.shape[1] % (nj * ni) == 0
    slab = wo.shape[1] // (nj * ni)
    assert slab % 16 == 0
    return pl.pallas_call(
        functools.partial(_mm_swiglu_kernel, tm=tm, mm=mm),
        grid=(nj, ni),
        in_specs=[pl.BlockSpec((tm, k), lambda j, i: (i, 0)),
                  pl.BlockSpec((None, k, tn), lambda j, i: (layer, 0, j)),
                  pl.BlockSpec((None, k, tn), lambda j, i: (layer, 0, j + nj)),
                  pl.BlockSpec((None, slab, D), lambda j, i: (layer, j * ni + i, 0))],
        out_specs=[pl.BlockSpec((tm, tn), lambda j, i: (i, j)),
                   pl.BlockSpec((slab, D), lambda j, i: (j * ni + i, 0))],
        out_shape=[jax.ShapeDtypeStruct((rows, n_out), BF16),
                   jax.ShapeDtypeStruct((wo.shape[1], D), BF16)],
        scratch_shapes=[pltpu.VMEM((k, tn), BF16)] * 2,
        compiler_params=_cparams(("arbitrary", "arbitrary")),
        name="mm_swiglu",
    )(a, wi, wi, wo)


def _mm_plain(a, wb, col0, ncols, *, tm, tn, out_dtype):
    rows, k = a.shape
    assert rows % tm == 0 and ncols % tn == 0 and col0 % tn == 0
    return pl.pallas_call(
        _mm_plain_kernel,
        grid=(ncols // tn, rows // tm),
        in_specs=[pl.BlockSpec((tm, k), lambda j, i: (i, 0)),
                  pl.BlockSpec((k, tn), lambda j, i: (0, col0 // tn + j))],
        out_specs=pl.BlockSpec((tm, tn), lambda j, i: (i, j)),
        out_shape=jax.ShapeDtypeStruct((rows, ncols), out_dtype),
        compiler_params=_cparams(("arbitrary", "arbitrary")),
        name="mm_plain",
    )(a, wb)


def _gate_rows(gate_ref, i, tm):
    rows = i * tm + lax.broadcasted_iota(jnp.int32, (tm, 1), 0)
    return jnp.where(rows >= S, gate_ref[0, 0, 1:2, :], gate_ref[0, 0, 0:1, :])


def _ffn_down_kernel(*refs, tm, n_jobs):
    h_ref, gate_ref, a_ref, w_ref = refs[:4]
    o_ref = refs[4 + n_jobs]
    for src_ref, dst_ref in zip(refs[4:4 + n_jobs], refs[5 + n_jobs:]):
        dst_ref[...] = src_ref[...].astype(BF16)
    acc = jnp.dot(a_ref[...], w_ref[...], preferred_element_type=F32)
    o_ref[...] = h_ref[...] + (0.5 * _gate_rows(gate_ref, pl.program_id(1), tm)) * acc


def _ffn_down(h, rows, mods, layer, gate_idx, a, wb, cast_jobs=(), *, tm, tn):
    k = a.shape[1]
    assert rows % tm == 0
    nj, ni = D // tn, rows // tm
    job_in, job_out, job_shape = [], [], []
    for w, idx in cast_jobs:
        kk, nn = w.shape[1:]
        assert kk % (nj * ni) == 0 and (kk // (nj * ni)) % 16 == 0
        slab = kk // (nj * ni)
        job_in.append(pl.BlockSpec((None, slab, nn), lambda j, i, idx=idx: (idx, j * ni + i, 0)))
        job_out.append(pl.BlockSpec((slab, nn), lambda j, i: (j * ni + i, 0)))
        job_shape.append(jax.ShapeDtypeStruct((kk, nn), BF16))
    res = pl.pallas_call(
        functools.partial(_ffn_down_kernel, tm=tm, n_jobs=len(cast_jobs)),
        grid=(nj, ni),
        in_specs=[pl.BlockSpec((tm, tn), lambda j, i: (i, j)),
                  pl.BlockSpec((1, 1, 8, tn), lambda j, i: (layer, gate_idx, 0, j)),
                  pl.BlockSpec((tm, k), lambda j, i: (i, 0)),
                  pl.BlockSpec((k, tn), lambda j, i: (0, j))] + job_in,
        out_specs=[pl.BlockSpec((tm, tn), lambda j, i: (i, j))] + job_out,
        out_shape=[jax.ShapeDtypeStruct((rows, D), F32)] + job_shape,
        compiler_params=_cparams(("arbitrary", "arbitrary")),
        name="ffn_down",
    )(h, mods, a, wb, *[w for w, _ in cast_jobs])
    return res[0], list(res[1:])


def _out_proj_kernel(h_ref, gate_ref, a1_ref, a2_ref, w1_ref, w2_ref, nw_ref, mod_ref,
                     o_ref, hn_ref, *, tm, unroll):
    i = pl.program_id(0)
    acc = (jnp.dot(a1_ref[...], w1_ref[...], preferred_element_type=F32)
           + jnp.dot(a2_ref[...], w2_ref[...], preferred_element_type=F32))
    o_ref[...] = h_ref[...] + _gate_rows(gate_ref, i, tm) * acc
    nw = nw_ref[...]
    gain_lat = nw * (1.0 + mod_ref[0, 1, 0:1, :])
    gain_ctx = nw * (1.0 + mod_ref[0, 1, 1:2, :])
    shift_lat = mod_ref[0, 0, 0:1, :]
    shift_ctx = mod_ref[0, 0, 1:2, :]

    def emit(rs, x, start):
        isc = (i * tm + start + lax.broadcasted_iota(jnp.int32, (NORM_CHUNK, 1), 0)) >= S
        rinv = lax.rsqrt(jnp.mean(x * x, axis=-1, keepdims=True) + EPS)
        gain = jnp.where(isc, gain_ctx, gain_lat)
        shift = jnp.where(isc, shift_ctx, shift_lat)
        hn_ref[rs, :] = ((x * rinv) * gain + shift).astype(hn_ref.dtype)

    _norm_rows(o_ref, tm, emit, unroll=unroll)


def _out_proj(h, rows, mods, layer, a1, a2, wb, nw_next, *, tm):
    k = a1.shape[1]
    assert a2.shape[1] == k and wb.shape == (2 * k, D) and rows % tm == 0
    unroll = 4 if tm % (4 * NORM_CHUNK) == 0 else 3
    row = lambda i: (i, 0)
    return pl.pallas_call(
        functools.partial(_out_proj_kernel, tm=tm, unroll=unroll),
        grid=(rows // tm,),
        in_specs=[pl.BlockSpec((tm, D), row),
                  pl.BlockSpec((1, 1, 8, D), lambda i: (layer, 5, 0, 0)),
                  pl.BlockSpec((tm, k), row),
                  pl.BlockSpec((tm, k), row),
                  pl.BlockSpec((k, D), lambda i: (0, 0)),
                  pl.BlockSpec((k, D), lambda i: (1, 0)),
                  pl.BlockSpec((1, D), lambda i: (0, 0)),
                  pl.BlockSpec((1, 3, 8, D), lambda i: (layer, 2, 0, 0))],
        out_specs=[pl.BlockSpec((tm, D), row), pl.BlockSpec((tm, D), row)],
        out_shape=[jax.ShapeDtypeStruct((rows, D), F32), jax.ShapeDtypeStruct((rows, D), BF16)],
        compiler_params=_cparams(("arbitrary",)),
        name="out_proj",
    )(h, mods, a1, a2, wb, wb, nw_next.reshape(1, D), mods)


def _rope_tables():
    t = np.arange(S)
    rows = (t // GW).astype(np.float32)
    cols = (t % GW).astype(np.float32)
    half = HD // 2
    inv = np.float32(ROPE_BASE) ** (-np.arange(0, half, 2, dtype=np.float32) / np.float32(half))
    ar = rows[:, None] * inv
    ac = cols[:, None] * inv
    cos = np.concatenate([np.cos(ar), np.cos(ar), np.cos(ac), np.cos(ac)], axis=-1)
    sin = np.concatenate([-np.sin(ar), np.sin(ar), -np.sin(ac), np.sin(ac)], axis=-1)
    lat = np.concatenate([cos, sin], axis=-1)
    ctx = np.concatenate([np.ones((LC, HD), np.float32), np.zeros((LC, HD), np.float32)], axis=-1)
    return jnp.asarray(np.concatenate([lat, ctx], axis=0).astype(np.float32))


def _rope(x, tab):
    c = tab[:, :HD]
    sg = tab[:, HD:]
    lane = lax.broadcasted_iota(jnp.int32, x.shape, 1)
    first = (lane % (HD // 2)) < (HD // 4)
    partner = jnp.where(first, pltpu.roll(x, HD - HD // 4, 1), pltpu.roll(x, HD // 4, 1))
    return x * c + partner * sg


def _attn_kernel(sink_ref, q_ref, kp_ref, kc_ref, kn_ref, vp_ref, vc_ref, vn_ref, kx_ref, vx_ref,
                 tp_ref, tc_ref, tn_ref, o_ref):
    g = pl.program_id(0)
    n = pl.program_id(1)
    nlat = S // AQB
    scale2 = (HD ** -0.5) * LOG2E
    k_all = jnp.concatenate([
        _rope(kp_ref[...], tp_ref[...]).astype(BF16),
        _rope(kc_ref[...], tc_ref[...]).astype(BF16),
        _rope(kn_ref[...], tn_ref[...]).astype(BF16),
        kx_ref[...].astype(BF16)], axis=0)
    v_all = jnp.concatenate([vp_ref[...].astype(BF16), vc_ref[...].astype(BF16),
                             vn_ref[...].astype(BF16), vx_ref[...].astype(BF16)], axis=0)
    nloc = AQB + 2 * AWIN
    nk = nloc + LC
    r = lax.broadcasted_iota(jnp.int32, (AQB, nk), 0)
    c = lax.broadcasted_iota(jnp.int32, (AQB, nk), 1)
    rel = c - AWIN - r
    pos = n * AQB - AWIN + c
    hi = jnp.where(n < nlat, S, 0)
    ok = ((jnp.abs(rel) <= AWIN) & (pos >= 0) & (pos < hi)) | (c >= nloc)
    tq = tc_ref[...]
    for hh in range(AH // AKV):
        q = _rope(q_ref[:, hh * HD:(hh + 1) * HD], tq).astype(BF16)
        s = lax.dot_general(q, k_all, (((1,), (1,)), ((), ())), preferred_element_type=F32) * scale2
        s = jnp.where(ok, s, NEG)
        sk = sink_ref[g * (AH // AKV) + hh] * LOG2E
        m = jnp.maximum(jnp.max(s, axis=-1, keepdims=True), sk)
        e = jnp.exp2(s - m)
        den = jnp.sum(e, axis=-1, keepdims=True) + jnp.exp2(sk - m)
        o = jnp.dot(e.astype(BF16), v_all, preferred_element_type=F32)
        o_ref[:, hh * HD:(hh + 1) * HD] = (o / den).astype(o_ref.dtype)


def _attn(p, sink, tabs):
    assert AQB == LC and AQB == 2 * AWIN
    nlat = S // AQB
    qw = (AH // AKV) * HD
    kcol = AH
    vcol = kcol + AKV
    last_w = S // AWIN - 1

    def prev(n):
        return jnp.clip(2 * n - 1, 0, last_w)

    def cur(n):
        return jnp.minimum(n, nlat - 1)

    def nxt(n):
        return jnp.clip(2 * n + 2, 0, last_w)

    return pl.pallas_call(
        _attn_kernel,
        grid=(AKV, M // AQB),
        in_specs=[
            pl.BlockSpec(memory_space=pltpu.SMEM),
            pl.BlockSpec((AQB, qw), lambda g, n: (n, g)),
            pl.BlockSpec((AWIN, HD), lambda g, n: (prev(n), kcol + g)),
            pl.BlockSpec((AQB, HD), lambda g, n: (cur(n), kcol + g)),
            pl.BlockSpec((AWIN, HD), lambda g, n: (nxt(n), kcol + g)),
            pl.BlockSpec((AWIN, HD), lambda g, n: (prev(n), vcol + g)),
            pl.BlockSpec((AQB, HD), lambda g, n: (cur(n), vcol + g)),
            pl.BlockSpec((AWIN, HD), lambda g, n: (nxt(n), vcol + g)),
            pl.BlockSpec((LC, HD), lambda g, n: (S // LC, kcol + g)),
            pl.BlockSpec((LC, HD), lambda g, n: (S // LC, vcol + g)),
            pl.BlockSpec((AWIN, 2 * HD), lambda g, n: (prev(n), 0)),
            pl.BlockSpec((AQB, 2 * HD), lambda g, n: (n, 0)),
            pl.BlockSpec((AWIN, 2 * HD), lambda g, n: (nxt(n), 0)),
        ],
        out_specs=pl.BlockSpec((AQB, qw), lambda g, n: (n, g)),
        out_shape=jax.ShapeDtypeStruct((M, AH * HD), BF16),
        compiler_params=_cparams(("arbitrary", "arbitrary")),
        name="attn",
    )(sink, p, p, p, p, p, p, p, p, p, tabs, tabs, tabs)


def _conv_kernel(b_ref, c_ref, u_ref, cp_ref, up_ref, cn_ref, un_ref, w_ref, bias_ref, o_ref, *, tr):
    i = pl.program_id(0)
    lat_blocks = S // tr
    z = c_ref[...] * u_ref[...]
    zp = cp_ref[7:8, :] * up_ref[7:8, :]
    zn = cn_ref[0:1, :] * un_ref[0:1, :]
    has_prev = jnp.logical_and(i != 0, i != lat_blocks)
    has_next = jnp.logical_and(i != lat_blocks - 1, i != M // tr - 1)
    zp = jnp.where(has_prev, zp, 0.0)
    zn = jnp.where(has_next, zn, 0.0)
    row = lax.broadcasted_iota(jnp.int32, z.shape, 0)
    z_m1 = jnp.where(row == 0, zp, pltpu.roll(z, 1, 0))
    z_p1 = jnp.where(row == tr - 1, zn, pltpu.roll(z, tr - 1, 0))
    conv = z_m1 * w_ref[0:1, :] + z * w_ref[1:2, :] + z_p1 * w_ref[2:3, :] + bias_ref[...]
    o_ref[...] = (b_ref[...] * conv).astype(o_ref.dtype)


def _conv(p, conv_w, conv_b):
    tr, tc = 256, 512
    assert LC % tr == 0 and S % tr == 0
    base = (AH * HD + 2 * AKV * HD) // tc
    per = CCH // tc
    nrb = M // tr
    h8 = tr // 8

    def prev8(i):
        return jnp.maximum(i * h8 - 1, 0)

    def next8(i):
        return jnp.minimum((i + 1) * h8, M // 8 - 1)

    return pl.pallas_call(
        functools.partial(_conv_kernel, tr=tr),
        grid=(nrb, per),
        in_specs=[
            pl.BlockSpec((tr, tc), lambda i, j: (i, base + j)),
            pl.BlockSpec((tr, tc), lambda i, j: (i, base + per + j)),
            pl.BlockSpec((tr, tc), lambda i, j: (i, base + 2 * per + j)),
            pl.BlockSpec((8, tc), lambda i, j: (prev8(i), base + per + j)),
            pl.BlockSpec((8, tc), lambda i, j: (prev8(i), base + 2 * per + j)),
            pl.BlockSpec((8, tc), lambda i, j: (next8(i), base + per + j)),
            pl.BlockSpec((8, tc), lambda i, j: (next8(i), base + 2 * per + j)),
            pl.BlockSpec((3, tc), lambda i, j: (0, j)),
            pl.BlockSpec((1, tc), lambda i, j: (0, j)),
        ],
        out_specs=pl.BlockSpec((tr, tc), lambda i, j: (i, j)),
        out_shape=jax.ShapeDtypeStruct((M, CCH), BF16),
        compiler_params=_cparams(("arbitrary", "arbitrary")),
        name="conv",
    )(p, p, p, p, p, p, p, conv_w, conv_b.reshape(1, CCH))


def _log_sigmoid(d):
    return jnp.minimum(d, 0.0) - jnp.log(1.0 + jnp.exp(-jnp.abs(d)))


def _ret_chunk(decay_ref, q_ref, k_ref, v_ref, tab_ref, st_ref, h, backward):
    lg = _log_sigmoid(jnp.full((1, 1), decay_ref[h], F32))
    tab = tab_ref[...]
    q = _rope(q_ref[:, h * RDK:(h + 1) * RDK], tab).astype(BF16)
    k = (_rope(k_ref[:, h * RDK:(h + 1) * RDK], tab) * (RDK ** -0.5)).astype(BF16)
    v = v_ref[:, h * RDV:(h + 1) * RDV]
    ii = lax.broadcasted_iota(jnp.int32, (RC, RC), 0)
    jj = lax.broadcasted_iota(jnp.int32, (RC, RC), 1)
    ic = lax.broadcasted_iota(jnp.int32, (RC, 1), 0).astype(F32)
    if backward:
        rel = jj - ii
        keep = rel > 0
        xi = jnp.exp(lg * (RC - ic))
        zeta = jnp.exp(lg * ic)
    else:
        rel = ii - jj
        keep = rel >= 0
        xi = jnp.exp(lg * (ic + 1.0))
        zeta = jnp.exp(lg * (RC - 1.0 - ic))
    dmask = jnp.where(keep, jnp.exp(lg * jnp.where(keep, rel, 0).astype(F32)), 0.0)
    inner = lax.dot_general(q, k, (((1,), (1,)), ((), ())), preferred_element_type=F32) * dmask
    y = jnp.dot(inner.astype(BF16), v.astype(BF16), preferred_element_type=F32)
    st = st_ref[h]
    y = y + jnp.dot(q, st.astype(BF16), preferred_element_type=F32) * xi
    kv = lax.dot_general(k, (v * zeta).astype(BF16), (((0,), (0,)), ((), ())),
                         preferred_element_type=F32)
    st_ref[h] = jnp.exp(lg * float(RC)) * st + kv
    return y


def _ret_fwd_kernel(decay_ref, q_ref, k_ref, v_ref, tab_ref, y_ref, st_ref):
    @pl.when(pl.program_id(0) == 0)
    def _():
        st_ref[...] = jnp.zeros_like(st_ref)

    for h in range(RH):
        y_ref[:, h * RDV:(h + 1) * RDV] = _ret_chunk(decay_ref, q_ref, k_ref, v_ref, tab_ref,
                                                     st_ref, h, False)


def _ret_bwd_kernel(decay_ref, q_ref, k_ref, v_ref, tab_ref, yf_ref, g_ref, gnw_ref, o_ref, st_ref):
    @pl.when(pl.program_id(0) == 0)
    def _():
        st_ref[...] = jnp.zeros_like(st_ref)

    for h in range(RH):
        sl = slice(h * RDV, (h + 1) * RDV)
        y = yf_ref[:, sl] + _ret_chunk(decay_ref, q_ref, k_ref, v_ref, tab_ref, st_ref, h, True)
        mu = jnp.mean(y, axis=-1, keepdims=True)
        yc = y - mu
        var = jnp.mean(yc * yc, axis=-1, keepdims=True)
        yn = yc * lax.rsqrt(var + EPS) * gnw_ref[:, sl]
        gt = g_ref[:, sl]
        o_ref[:, sl] = (gt * jax.nn.sigmoid(gt) * yn).astype(o_ref.dtype)


def _retention(p, decay_f, decay_b, gn_w, tabs):
    nch = M // RC
    nlat = S // RC
    qw, vw = RH * RDK, RH * RDV
    fwd = lambda s: ((s + nlat) % nch)
    bwd = lambda s: (nch - 1 - s)
    smem = pl.BlockSpec(memory_space=pltpu.SMEM)

    def specs(cm):
        return [smem,
                pl.BlockSpec((RC, qw), lambda s: (cm(s), 0)),
                pl.BlockSpec((RC, qw), lambda s: (cm(s), 1)),
                pl.BlockSpec((RC, vw), lambda s: (cm(s), 1)),
                pl.BlockSpec((RC, 2 * HD), lambda s: (cm(s), 0))]

    state = pltpu.VMEM((RH, RDK, RDV), F32)
    yf = pl.pallas_call(
        _ret_fwd_kernel,
        grid=(nch,),
        in_specs=specs(fwd),
        out_specs=pl.BlockSpec((RC, vw), lambda s: (fwd(s), 0)),
        out_shape=jax.ShapeDtypeStruct((M, vw), F32),
        scratch_shapes=[state],
        compiler_params=_cparams(("arbitrary",)),
        name="ret_fwd",
    )(decay_f, p, p, p, tabs)
    return pl.pallas_call(
        _ret_bwd_kernel,
        grid=(nch,),
        in_specs=specs(bwd) + [pl.BlockSpec((RC, vw), lambda s: (bwd(s), 0)),
                               pl.BlockSpec((RC, vw), lambda s: (bwd(s), 2)),
                               pl.BlockSpec((1, vw), lambda s: (0, 0))],
        out_specs=pl.BlockSpec((RC, vw), lambda s: (bwd(s), 0)),
        out_shape=jax.ShapeDtypeStruct((M, vw), BF16),
        scratch_shapes=[state],
        compiler_params=_cparams(("arbitrary",)),
        name="ret_bwd",
    )(decay_b, p, p, p, tabs, yf, p, gn_w.reshape(1, vw))


_NA_NBLK = S // NQB
_NA_CASES = (0, 1, _NA_NBLK - 1)


def _na_window_row(b):
    return jnp.clip(NQROWS * b - NROWS // 2, 0, GROWS - NWIN_ROWS)


def _na_table_kernel(rpb_ref, o_ref):
    h = pl.program_id(0)
    n_dr, n_dc = 2 * NROWS - 1, 2 * NCOLS - 1
    cq = lax.broadcasted_iota(jnp.int32, (GW, 2 * GW), 0)
    lane = lax.broadcasted_iota(jnp.int32, (GW, 2 * GW), 1)
    right = lane >= GW
    kc = jnp.where(right, lane - GW, lane)
    dc = kc - cq + (NCOLS - 1)
    cs = jnp.clip(cq - NCOLS // 2, 0, GW - NCOLS)
    col_ok = (kc >= cs) & (kc < cs + NCOLS)
    memo = {}

    def row_scalar(a, b):
        return rpb_ref[(h * n_dr + a) * n_dc + b] if 0 <= a < n_dr else 0.0

    def pair(a0):
        if a0 not in memo:
            acc = jnp.zeros((GW, 2 * GW), F32)
            for b in range(n_dc):
                vec = jnp.where(right, row_scalar(a0 + 1, b), row_scalar(a0, b))
                acc = jnp.where(dc == b, vec, acc)
            memo[a0] = acc
        return memo[a0]

    for ci, blk in enumerate(_NA_CASES):
        w = int(np.clip(NQROWS * blk - NROWS // 2, 0, GROWS - NWIN_ROWS))
        for rl in range(NQROWS):
            r = NQROWS * blk + rl
            rs = int(np.clip(r - NROWS // 2, 0, GROWS - NROWS))
            for pr in range(NWIN_ROWS // 2):
                kr = w + 2 * pr
                ok_l = rs <= kr < rs + NROWS
                ok_r = rs <= kr + 1 < rs + NROWS
                if ok_l and ok_r:
                    ok = col_ok
                elif ok_l:
                    ok = col_ok & jnp.logical_not(right)
                elif ok_r:
                    ok = col_ok & right
                else:
                    ok = None
                if ok is None:
                    tile = jnp.full((GW, 2 * GW), NEG, F32)
                else:
                    tile = jnp.where(ok, pair(kr - r + NROWS - 1) * LOG2E, NEG)
                o_ref[ci, 0, rl * GW:(rl + 1) * GW, pr * 2 * GW:(pr + 1) * 2 * GW] = tile


def _na_tables(rpb):
    return pl.pallas_call(
        _na_table_kernel,
        grid=(NH,),
        in_specs=[pl.BlockSpec(memory_space=pltpu.SMEM)],
        out_specs=pl.BlockSpec((len(_NA_CASES), 1, NQB, NWIN), lambda h: (0, h, 0, 0)),
        out_shape=jax.ShapeDtypeStruct((len(_NA_CASES), NH, NQB, NWIN), F32),
        compiler_params=_cparams(("arbitrary",)),
        name="na_tables",
    )(rpb.reshape(-1))


def _na_kernel(q_ref, k_ref, v_ref, t_ref, o_ref):
    b = pl.program_id(1)
    scale2 = (HD ** -0.5) * LOG2E
    start = pl.multiple_of(_na_window_row(b) * GW, NQB)
    nt = (((1,), (1,)), ((), ()))
    for hh in range(NHPS):
        cs = slice(hh * HD, (hh + 1) * HD)
        q = q_ref[:, cs]
        s_loc = lax.dot_general(q, k_ref[pl.ds(start, NWIN), cs], nt,
                                preferred_element_type=F32) * scale2 + t_ref[0, hh]
        s_ctx = lax.dot_general(q, k_ref[S:M, cs], nt, preferred_element_type=F32) * scale2
        m = jnp.maximum(jnp.max(s_loc, axis=-1, keepdims=True), jnp.max(s_ctx, axis=-1, keepdims=True))
        e_loc = jnp.exp2(s_loc - m)
        e_ctx = jnp.exp2(s_ctx - m)
        den = jnp.sum(e_loc, axis=-1, keepdims=True) + jnp.sum(e_ctx, axis=-1, keepdims=True)
        o = (jnp.dot(e_loc.astype(BF16), v_ref[pl.ds(start, NWIN), cs], preferred_element_type=F32)
             + jnp.dot(e_ctx.astype(BF16), v_ref[S:M, cs], preferred_element_type=F32))
        o_ref[:, cs] = (o / den).astype(o_ref.dtype)


def _na(p, rpb):
    gw = NHPS * HD
    qcol = 0
    kcol = qcol + NH // NHPS
    vcol = kcol + NH // NHPS
    tables = _na_tables(rpb)

    def case(b):
        return jnp.minimum(b, 1) + jnp.maximum(b - (_NA_NBLK - 2), 0)

    return pl.pallas_call(
        _na_kernel,
        grid=(NH // NHPS, _NA_NBLK),
        in_specs=[
            pl.BlockSpec((NQB, gw), lambda h, b: (b, qcol + h)),
            pl.BlockSpec((M, gw), lambda h, b: (0, kcol + h)),
            pl.BlockSpec((M, gw), lambda h, b: (0, vcol + h)),
            pl.BlockSpec((1, NHPS, NQB, NWIN), lambda h, b: (case(b), h, 0, 0)),
        ],
        out_specs=pl.BlockSpec((NQB, gw), lambda h, b: (b, h)),
        out_shape=jax.ShapeDtypeStruct((S, NH * HD), BF16),
        compiler_params=_cparams(("arbitrary", "arbitrary")),
        name="na",
    )(p, p, p, tables)


def _final_norm_kernel(x_ref, w_ref, o_ref):
    w = w_ref[...]

    def emit(rs, x, start):
        rinv = lax.rsqrt(jnp.mean(x * x, axis=-1, keepdims=True) + EPS)
        o_ref[rs, :] = x * rinv * w

    _norm_rows(x_ref, NORM_TR, emit)


def _final_norm(h, w):
    tr = NORM_TR
    return pl.pallas_call(
        _final_norm_kernel,
        grid=(S // tr,),
        in_specs=[pl.BlockSpec((tr, D), lambda i: (i, 0)), pl.BlockSpec((1, D), lambda i: (0, 0))],
        out_specs=pl.BlockSpec((tr, D), lambda i: (i, 0)),
        out_shape=jax.ShapeDtypeStruct((S, D), F32),
        compiler_params=_cparams(("arbitrary",)),
        name="final_norm",
    )(h, w.reshape(1, D))


def _tiles(rows):
    if rows == M:
        return dict(up=(2112, 1056), small=528, inp=1056)
    assert rows == S
    return dict(up=(2048, 1024), small=512, inp=1024)


def _ffn_half(h, rows, mods, layer, third, nw, wi, wo, hn=None, ctx=None, cast_jobs=()):
    t = _tiles(rows)
    if ctx is not None:
        hn, h = _normmod(h, rows, nw, mods, layer, third, ctx=ctx)
    elif hn is None:
        hn = _normmod(h, rows, nw, mods, layer, third)
    a, wob = _mm_swiglu(hn, wi, wo, layer, tm=t["up"][0], mm=t["up"][1], tn=512)
    return _ffn_down(h, rows, mods, layer, 3 * third + 2, a, wob, cast_jobs, tm=t["small"], tn=1024)


IN_TN = 1536


def kernel(x, c, ctx, c_ctx, ada_w, ada_b, norm_w, ffn_a_wi, ffn_a_wo, ffn_b_wi, ffn_b_wo,
           ev_w_in, ev_w_out, ev_sink, ev_conv_w, ev_conv_b,
           od_w_in, od_w_out, od_decay_f, od_decay_b, od_gn_w, od_rpb, final_norm_w):
    assert x.shape == (1, S, D) and ctx.shape == (1, LC, D) and ada_w.shape[0] == 2
    cvec = jnp.concatenate([c, c_ctx[None, :], jnp.zeros((6, D), F32)], axis=0)
    mods = _ada(cvec, ada_w, ada_b)
    tabs = _rope_tables()

    tm_in = _tiles(M)["inp"]
    h, (w_in, w_out) = _ffn_half(x[0], M, mods, 0, 0, norm_w[0, 0], ffn_a_wi, ffn_a_wo, ctx=ctx[0],
                                 cast_jobs=((ev_w_in, 0), (ev_w_out, 0)))
    hn = _normmod(h, M, norm_w[0, 1], mods, 0, 1)
    p = _mm_plain(hn, w_in, 0, EV_IN, tm=tm_in, tn=IN_TN, out_dtype=F32)
    att = _attn(p, ev_sink[0], tabs)
    cnv = _conv(p, ev_conv_w[0], ev_conv_b[0])
    h, hn = _out_proj(h, M, mods, 0, att, cnv, w_out, norm_w[0, 2], tm=_tiles(M)["small"])
    h, _ = _ffn_half(h, M, mods, 0, 2, norm_w[0, 2], ffn_b_wi, ffn_b_wo, hn=hn)

    h, (w_in, w_out) = _ffn_half(h, M, mods, 1, 0, norm_w[1, 0], ffn_a_wi, ffn_a_wo,
                                 cast_jobs=((od_w_in, 0), (od_w_out, 0)))
    hn = _normmod(h, M, norm_w[1, 1], mods, 1, 1)
    n_ret = 2 * RH * RDK + 2 * RH * RDV
    p_ret = _mm_plain(hn, w_in, 0, n_ret, tm=tm_in, tn=IN_TN, out_dtype=F32)
    p_na = _mm_plain(hn, w_in, n_ret, OD_IN - n_ret, tm=tm_in, tn=IN_TN, out_dtype=BF16)
    ret = _retention(p_ret, od_decay_f[0], od_decay_b[0], od_gn_w[0], tabs)
    nat = _na(p_na, od_rpb[0])
    h, hn = _out_proj(h, S, mods, 1, ret, nat, w_out, norm_w[1, 2], tm=_tiles(S)["small"])
    h, _ = _ffn_half(h, S, mods, 1, 2, norm_w[1, 2], ffn_b_wi, ffn_b_wo, hn=hn)
    return _final_norm(h, final_norm_w)[None]
```

```python
import functools

import numpy as np
import jax
import jax.numpy as jnp
from jax import lax
from jax.experimental import pallas as pl
from jax.experimental.pallas import tpu as pltpu

F32 = jnp.float32
BF16 = jnp.bfloat16

D = 2048
S = 8192
LC = 256
M = S + LC
GW = 64
GROWS = S // GW
HD = 128
DFF = 5632
NMOD = 9
EPS = 1e-6
ROPE_BASE = 10000.0
NEG = -1e30

AH, AKV, AWIN = 8, 2, 128
AQB = 256
CCH = 1024
EV_IN = AH * HD + 2 * AKV * HD + 3 * CCH
RH, RDK, RDV = 4, 128, 256
RC = 256
NH, NROWS, NCOLS = 8, 8, 16
NQROWS = 4
NQB = NQROWS * GW
NWIN_ROWS = NQROWS + NROWS
NWIN = NWIN_ROWS * GW
NHPS = 4
LOG2E = float(np.log2(np.e))
OD_IN = 2 * RH * RDK + 2 * RH * RDV + 3 * NH * HD

VMEM_LIMIT = 56 * 1024 * 1024


def _cparams(sem):
    return pltpu.CompilerParams(dimension_semantics=sem, vmem_limit_bytes=VMEM_LIMIT)


def _sub(r, size):
    return pl.ds(pl.multiple_of(r * size, 16), size)


def _ada_kernel(c_ref, w_ref, b_ref, o_ref):
    cv = c_ref[...]
    a = cv * jax.nn.sigmoid(cv)
    acc = jnp.dot(a.astype(BF16), w_ref[0].astype(BF16), preferred_element_type=F32)
    o_ref[0, 0] = acc + b_ref[0]


def _ada(cvec, ada_w, ada_b):
    depth = ada_w.shape[0]
    tn = 1024
    per = D // tn
    return pl.pallas_call(
        _ada_kernel,
        grid=(depth, NMOD * per),
        in_specs=[
            pl.BlockSpec((8, D), lambda l, j: (0, 0)),
            pl.BlockSpec((1, D, tn), lambda l, j: (l, 0, j)),
            pl.BlockSpec((1, 1, tn), lambda l, j: (l, 0, j)),
        ],
        out_specs=pl.BlockSpec((1, 1, 8, tn), lambda l, j: (l, j // per, 0, j % per)),
        out_shape=jax.ShapeDtypeStruct((depth, NMOD, 8, D), F32),
        compiler_params=_cparams(("arbitrary", "arbitrary")),
        name="ada",
    )(cvec, ada_w, ada_b.reshape(depth, 1, NMOD * D))


NORM_TR = 1024
NORM_CHUNK = 16
NORM_UNROLL = 4


def _norm_rows(src_ref, n_rows, fn, unroll=NORM_UNROLL):
    group = NORM_CHUNK * unroll

    def outer(gi, carry):
        for u in range(unroll):
            start = pl.multiple_of(gi * group + u * NORM_CHUNK, NORM_CHUNK)
            rs = pl.ds(start, NORM_CHUNK)
            fn(rs, src_ref[rs, :], start)
        return carry

    lax.fori_loop(0, n_rows // group, outer, 0)


def _normmod_first_kernel(x_ref, c_ref, nw_ref, mod_ref, o_ref, h_ref):
    i = pl.program_id(0)
    kind = (i * NORM_TR >= S).astype(jnp.int32)
    n_rows = jnp.minimum(M - i * NORM_TR, NORM_TR)
    shift = mod_ref[0, 0, pl.ds(kind, 1), :]
    gain = nw_ref[...] * (1.0 + mod_ref[0, 1, pl.ds(kind, 1), :])

    def emit(rs, x, start):
        rinv = lax.rsqrt(jnp.mean(x * x, axis=-1, keepdims=True) + EPS)
        o_ref[rs, :] = ((x * rinv) * gain + shift).astype(o_ref.dtype)
        h_ref[rs, :] = x

    @pl.when(kind == 0)
    def _():
        _norm_rows(x_ref, n_rows, emit)

    @pl.when(kind == 1)
    def _():
        _norm_rows(c_ref, n_rows, emit)


def _normmod_first(x, ctx, nw, mods):
    assert S % NORM_TR == 0 and LC % (NORM_CHUNK * NORM_UNROLL) == 0 and LC <= NORM_TR
    row_spec = pl.BlockSpec((NORM_TR, D), lambda i: (i, 0))
    return pl.pallas_call(
        _normmod_first_kernel,
        grid=(pl.cdiv(M, NORM_TR),),
        in_specs=[pl.BlockSpec((NORM_TR, D), lambda i: (jnp.minimum(i, S // NORM_TR - 1), 0)),
                  pl.BlockSpec((LC, D), lambda i: (0, 0)),
                  pl.BlockSpec((1, D), lambda i: (0, 0)),
                  pl.BlockSpec((1, 3, 8, D), lambda i: (0, 0, 0, 0))],
        out_specs=[row_spec, row_spec],
        out_shape=[jax.ShapeDtypeStruct((M, D), BF16), jax.ShapeDtypeStruct((M, D), F32)],
        compiler_params=_cparams(("arbitrary",)),
        name="normmod_first",
    )(x, ctx, nw.reshape(1, D), mods)


def _cast_weights(w_refs, wb_refs):
    @pl.when(pl.program_id(1) == 0)
    def _():
        for w_ref, wb_ref in zip(w_refs, wb_refs):
            wb_ref[...] = w_ref[...].astype(BF16)


def _mm_swiglu_kernel(a_ref, wg_ref, wu_ref, wo_ref, o_ref, wob_ref, wgb_ref, wub_ref, *, tm, mm):
    _cast_weights((wg_ref, wu_ref), (wgb_ref, wub_ref))
    wob_ref[...] = wo_ref[...].astype(BF16)

    def body(r, carry):
        a = a_ref[_sub(r, mm), :]
        g = jnp.dot(a, wgb_ref[...], preferred_element_type=F32)
        u = jnp.dot(a, wub_ref[...], preferred_element_type=F32)
        o_ref[_sub(r, mm), :] = (g * jax.nn.sigmoid(g) * u).astype(o_ref.dtype)
        return carry

    lax.fori_loop(0, tm // mm, body, 0, unroll=True)


def _mm_plain_kernel(a_ref, w_ref, o_ref):
    o_ref[...] = jnp.dot(a_ref[...], w_ref[...], preferred_element_type=F32).astype(o_ref.dtype)


def _mm_swiglu(a, wi, wo, layer, *, tm, mm, tn):
    rows, k = a.shape
    assert rows % tm == 0 and tm % mm == 0 and mm % 16 == 0
    n_out = wi.shape[2] // 2
    nj, ni = n_out // tn, rows // tm
    assert wo.shape[1] % (nj * ni) == 0
    slab = wo.shape[1] // (nj * ni)
    assert slab % 16 == 0
    return pl.pallas_call(
        functools.partial(_mm_swiglu_kernel, tm=tm, mm=mm),
        grid=(nj, ni),
        in_specs=[pl.BlockSpec((tm, k), lambda j, i: (i, 0)),
                  pl.BlockSpec((None, k, tn), lambda j, i: (layer, 0, j)),
                  pl.BlockSpec((None, k, tn), lambda j, i: (layer, 0, j + nj)),
                  pl.BlockSpec((None, slab, D), lambda j, i: (layer, j * ni + i, 0))],
        out_specs=[pl.BlockSpec((tm, tn), lambda j, i: (i, j)),
                   pl.BlockSpec((slab, D), lambda j, i: (j * ni + i, 0))],
        out_shape=[jax.ShapeDtypeStruct((rows, n_out), BF16),
                   jax.ShapeDtypeStruct((wo.shape[1], D), BF16)],
        scratch_shapes=[pltpu.VMEM((k, tn), BF16)] * 2,
        compiler_params=_cparams(("arbitrary", "arbitrary")),
        name="mm_swiglu",
    )(a, wi, wi, wo)


def _mm_plain(a, wb, col0, ncols, *, tm, tn, out_dtype):
    rows, k = a.shape
    assert rows % tm == 0 and ncols % tn == 0 and col0 % tn == 0
    return pl.pallas_call(
        _mm_plain_kernel,
        grid=(ncols // tn, rows // tm),
        in_specs=[pl.BlockSpec((tm, k), lambda j, i: (i, 0)),
                  pl.BlockSpec((k, tn), lambda j, i: (0, col0 // tn + j))],
        out_specs=pl.BlockSpec((tm, tn), lambda j, i: (i, j)),
        out_shape=jax.ShapeDtypeStruct((rows, ncols), out_dtype),
        compiler_params=_cparams(("arbitrary", "arbitrary")),
        name="mm_plain",
    )(a, wb)


def _gate_rows(gate_ref, i, tm):
    rows = i * tm + lax.broadcasted_iota(jnp.int32, (tm, 1), 0)
    return jnp.where(rows >= S, gate_ref[0, 0, 1:2, :], gate_ref[0, 0, 0:1, :])


DOWN_TM = 256


def _ffn_down_kernel(*refs, n_jobs, final):
    h_ref, gate_ref, a_ref, w_ref, nw_ref = refs[:5]
    n_in = 5 if final else 6
    o_ref = refs[n_in + n_jobs]
    n_out = 1 if final else 2
    first_job_out = n_in + n_jobs + n_out
    for src_ref, dst_ref in zip(refs[n_in:n_in + n_jobs], refs[first_job_out:first_job_out + n_jobs]):
        dst_ref[...] = src_ref[...].astype(BF16)
    kind = (pl.program_id(0) * DOWN_TM >= S).astype(jnp.int32)
    acc = jnp.dot(a_ref[...], w_ref[...], preferred_element_type=F32)
    new_ref = refs[-1] if final else o_ref
    new_ref[...] = h_ref[...] + (0.5 * gate_ref[0, 0, pl.ds(kind, 1), :]) * acc
    nw = nw_ref[...]
    if final:
        def emit(rs, x, start):
            rinv = lax.rsqrt(jnp.mean(x * x, axis=-1, keepdims=True) + EPS)
            o_ref[rs, :] = x * rinv * nw
    else:
        mod_ref, hn_ref = refs[5], refs[n_in + n_jobs + 1]
        shift = mod_ref[0, 0, pl.ds(kind, 1), :]
        gain = nw * (1.0 + mod_ref[0, 1, pl.ds(kind, 1), :])

        def emit(rs, x, start):
            rinv = lax.rsqrt(jnp.mean(x * x, axis=-1, keepdims=True) + EPS)
            hn_ref[rs, :] = ((x * rinv) * gain + shift).astype(hn_ref.dtype)

    _norm_rows(new_ref, DOWN_TM, emit)


def _ffn_down(h, rows, mods, layer, gate_idx, a, wb, nw_next, next_mod=None, cast_jobs=()):
    k = a.shape[1]
    final = next_mod is None
    tm = DOWN_TM
    assert rows % tm == 0 and S % tm == 0 and tm % (NORM_CHUNK * NORM_UNROLL) == 0
    n_slabs = S // tm
    row = lambda i: (i, 0)
    job_in, job_out, job_shape = [], [], []
    for w, idx in cast_jobs:
        kk, nn = w.shape[1:]
        assert kk % n_slabs == 0 and (kk // n_slabs) % 16 == 0
        slab = kk // n_slabs
        job_in.append(pl.BlockSpec((None, slab, nn),
                                   lambda i, idx=idx: (idx, jnp.minimum(i, n_slabs - 1), 0)))
        job_out.append(pl.BlockSpec((slab, nn), lambda i: (jnp.minimum(i, n_slabs - 1), 0)))
        job_shape.append(jax.ShapeDtypeStruct((kk, nn), BF16))
    in_specs = [pl.BlockSpec((tm, D), row),
                pl.BlockSpec((1, 1, 8, D), lambda i: (layer, gate_idx, 0, 0)),
                pl.BlockSpec((tm, k), row),
                pl.BlockSpec((k, D), lambda i: (0, 0), pipeline_mode=pl.Buffered(1)),
                pl.BlockSpec((1, D), lambda i: (0, 0))]
    args = [h, mods, a, wb, nw_next.reshape(1, D)]
    out_specs = [pl.BlockSpec((tm, D), row)]
    out_shape = [jax.ShapeDtypeStruct((rows, D), F32)]
    if not final:
        in_specs.append(pl.BlockSpec((1, 3, 8, D), lambda i: (next_mod[0], next_mod[1], 0, 0)))
        args.append(mods)
        out_specs.append(pl.BlockSpec((tm, D), row))
        out_shape.append(jax.ShapeDtypeStruct((rows, D), BF16))
    n_main = len(out_shape)
    res = pl.pallas_call(
        functools.partial(_ffn_down_kernel, n_jobs=len(cast_jobs), final=final),
        grid=(rows // tm,),
        in_specs=in_specs + job_in,
        out_specs=out_specs + job_out,
        out_shape=out_shape + job_shape,
        scratch_shapes=[pltpu.VMEM((tm, D), F32)] if final else [],
        compiler_params=_cparams(("arbitrary",)),
        name="ffn_down_final" if final else "ffn_down",
    )(*args, *[w for w, _ in cast_jobs])
    return tuple(res[:n_main]), list(res[n_main:])


def _out_proj_kernel(h_ref, gate_ref, a1_ref, a2_ref, w1_ref, w2_ref, nw_ref, mod_ref,
                     o_ref, hn_ref, *, tm, unroll):
    i = pl.program_id(0)
    acc = (jnp.dot(a1_ref[...], w1_ref[...], preferred_element_type=F32)
           + jnp.dot(a2_ref[...], w2_ref[...], preferred_element_type=F32))
    o_ref[...] = h_ref[...] + _gate_rows(gate_ref, i, tm) * acc
    nw = nw_ref[...]
    gain_lat = nw * (1.0 + mod_ref[0, 1, 0:1, :])
    gain_ctx = nw * (1.0 + mod_ref[0, 1, 1:2, :])
    shift_lat = mod_ref[0, 0, 0:1, :]
    shift_ctx = mod_ref[0, 0, 1:2, :]

    def emit(rs, x, start):
        isc = (i * tm + start + lax.broadcasted_iota(jnp.int32, (NORM_CHUNK, 1), 0)) >= S
        rinv = lax.rsqrt(jnp.mean(x * x, axis=-1, keepdims=True) + EPS)
        gain = jnp.where(isc, gain_ctx, gain_lat)
        shift = jnp.where(isc, shift_ctx, shift_lat)
        hn_ref[rs, :] = ((x * rinv) * gain + shift).astype(hn_ref.dtype)

    _norm_rows(o_ref, tm, emit, unroll=unroll)


def _out_proj(h, rows, mods, layer, a1, a2, wb, nw_next, *, tm):
    k = a1.shape[1]
    assert a2.shape[1] == k and wb.shape == (2 * k, D) and rows % tm == 0
    unroll = 4 if tm % (4 * NORM_CHUNK) == 0 else 3
    row = lambda i: (i, 0)
    return pl.pallas_call(
        functools.partial(_out_proj_kernel, tm=tm, unroll=unroll),
        grid=(rows // tm,),
        in_specs=[pl.BlockSpec((tm, D), row),
                  pl.BlockSpec((1, 1, 8, D), lambda i: (layer, 5, 0, 0)),
                  pl.BlockSpec((tm, k), row),
                  pl.BlockSpec((tm, k), row),
                  pl.BlockSpec((k, D), lambda i: (0, 0)),
                  pl.BlockSpec((k, D), lambda i: (1, 0)),
                  pl.BlockSpec((1, D), lambda i: (0, 0)),
                  pl.BlockSpec((1, 3, 8, D), lambda i: (layer, 2, 0, 0))],
        out_specs=[pl.BlockSpec((tm, D), row), pl.BlockSpec((tm, D), row)],
        out_shape=[jax.ShapeDtypeStruct((rows, D), F32), jax.ShapeDtypeStruct((rows, D), BF16)],
        compiler_params=_cparams(("arbitrary",)),
        name="out_proj",
    )(h, mods, a1, a2, wb, wb, nw_next.reshape(1, D), mods)


def _rope_tables():
    t = np.arange(S)
    rows = (t // GW).astype(np.float32)
    cols = (t % GW).astype(np.float32)
    half = HD // 2
    inv = np.float32(ROPE_BASE) ** (-np.arange(0, half, 2, dtype=np.float32) / np.float32(half))
    ar = rows[:, None] * inv
    ac = cols[:, None] * inv
    cos = np.concatenate([np.cos(ar), np.cos(ar), np.cos(ac), np.cos(ac)], axis=-1)
    sin = np.concatenate([-np.sin(ar), np.sin(ar), -np.sin(ac), np.sin(ac)], axis=-1)
    lat = np.concatenate([cos, sin], axis=-1)
    ctx = np.concatenate([np.ones((LC, HD), np.float32), np.zeros((LC, HD), np.float32)], axis=-1)
    return jnp.asarray(np.concatenate([lat, ctx], axis=0).astype(np.float32))


def _rope(x, tab):
    c = tab[:, :HD]
    sg = tab[:, HD:]
    lane = lax.broadcasted_iota(jnp.int32, x.shape, 1)
    first = (lane % (HD // 2)) < (HD // 4)
    partner = jnp.where(first, pltpu.roll(x, HD - HD // 4, 1), pltpu.roll(x, HD // 4, 1))
    return x * c + partner * sg


def _attn_kernel(sink_ref, q_ref, kp_ref, kc_ref, kn_ref, vp_ref, vc_ref, vn_ref, kx_ref, vx_ref,
                 tp_ref, tc_ref, tn_ref, o_ref):
    g = pl.program_id(0)
    n = pl.program_id(1)
    nlat = S // AQB
    scale2 = (HD ** -0.5) * LOG2E
    k_all = jnp.concatenate([
        _rope(kp_ref[...], tp_ref[...]).astype(BF16),
        _rope(kc_ref[...], tc_ref[...]).astype(BF16),
        _rope(kn_ref[...], tn_ref[...]).astype(BF16),
        kx_ref[...].astype(BF16)], axis=0)
    v_all = jnp.concatenate([vp_ref[...].astype(BF16), vc_ref[...].astype(BF16),
                             vn_ref[...].astype(BF16), vx_ref[...].astype(BF16)], axis=0)
    nloc = AQB + 2 * AWIN
    nk = nloc + LC
    r = lax.broadcasted_iota(jnp.int32, (AQB, nk), 0)
    c = lax.broadcasted_iota(jnp.int32, (AQB, nk), 1)
    rel = c - AWIN - r
    pos = n * AQB - AWIN + c
    hi = jnp.where(n < nlat, S, 0)
    ok = ((jnp.abs(rel) <= AWIN) & (pos >= 0) & (pos < hi)) | (c >= nloc)
    tq = tc_ref[...]
    for hh in range(AH // AKV):
        q = _rope(q_ref[:, hh * HD:(hh + 1) * HD], tq).astype(BF16)
        s = lax.dot_general(q, k_all, (((1,), (1,)), ((), ())), preferred_element_type=F32) * scale2
        s = jnp.where(ok, s, NEG)
        sk = sink_ref[g * (AH // AKV) + hh] * LOG2E
        m = jnp.maximum(jnp.max(s, axis=-1, keepdims=True), sk)
        e = jnp.exp2(s - m)
        den = jnp.sum(e, axis=-1, keepdims=True) + jnp.exp2(sk - m)
        o = jnp.dot(e.astype(BF16), v_all, preferred_element_type=F32)
        o_ref[:, hh * HD:(hh + 1) * HD] = (o / den).astype(o_ref.dtype)


def _attn(p, sink, tabs):
    assert AQB == LC and AQB == 2 * AWIN
    nlat = S // AQB
    qw = (AH // AKV) * HD
    kcol = AH
    vcol = kcol + AKV
    last_w = S // AWIN - 1

    def prev(n):
        return jnp.clip(2 * n - 1, 0, last_w)

    def cur(n):
        return jnp.minimum(n, nlat - 1)

    def nxt(n):
        return jnp.clip(2 * n + 2, 0, last_w)

    return pl.pallas_call(
        _attn_kernel,
        grid=(AKV, M // AQB),
        in_specs=[
            pl.BlockSpec(memory_space=pltpu.SMEM),
            pl.BlockSpec((AQB, qw), lambda g, n: (n, g)),
            pl.BlockSpec((AWIN, HD), lambda g, n: (prev(n), kcol + g)),
            pl.BlockSpec((AQB, HD), lambda g, n: (cur(n), kcol + g)),
            pl.BlockSpec((AWIN, HD), lambda g, n: (nxt(n), kcol + g)),
            pl.BlockSpec((AWIN, HD), lambda g, n: (prev(n), vcol + g)),
            pl.BlockSpec((AQB, HD), lambda g, n: (cur(n), vcol + g)),
            pl.BlockSpec((AWIN, HD), lambda g, n: (nxt(n), vcol + g)),
            pl.BlockSpec((LC, HD), lambda g, n: (S // LC, kcol + g)),
            pl.BlockSpec((LC, HD), lambda g, n: (S // LC, vcol + g)),
            pl.BlockSpec((AWIN, 2 * HD), lambda g, n: (prev(n), 0)),
            pl.BlockSpec((AQB, 2 * HD), lambda g, n: (n, 0)),
            pl.BlockSpec((AWIN, 2 * HD), lambda g, n: (nxt(n), 0)),
        ],
        out_specs=pl.BlockSpec((AQB, qw), lambda g, n: (n, g)),
        out_shape=jax.ShapeDtypeStruct((M, AH * HD), BF16),
        compiler_params=_cparams(("arbitrary", "arbitrary")),
        name="attn",
    )(sink, p, p, p, p, p, p, p, p, p, tabs, tabs, tabs)


def _conv_kernel(b_ref, c_ref, u_ref, cp_ref, up_ref, cn_ref, un_ref, w_ref, bias_ref, o_ref, *, tr):
    i = pl.program_id(0)
    lat_blocks = S // tr
    z = c_ref[...] * u_ref[...]
    zp = cp_ref[7:8, :] * up_ref[7:8, :]
    zn = cn_ref[0:1, :] * un_ref[0:1, :]
    has_prev = jnp.logical_and(i != 0, i != lat_blocks)
    has_next = jnp.logical_and(i != lat_blocks - 1, i != M // tr - 1)
    zp = jnp.where(has_prev, zp, 0.0)
    zn = jnp.where(has_next, zn, 0.0)
    row = lax.broadcasted_iota(jnp.int32, z.shape, 0)
    z_m1 = jnp.where(row == 0, zp, pltpu.roll(z, 1, 0))
    z_p1 = jnp.where(row == tr - 1, zn, pltpu.roll(z, tr - 1, 0))
    conv = z_m1 * w_ref[0:1, :] + z * w_ref[1:2, :] + z_p1 * w_ref[2:3, :] + bias_ref[...]
    o_ref[...] = (b_ref[...] * conv).astype(o_ref.dtype)


def _conv(p, conv_w, conv_b):
    tr, tc = 256, 512
    assert LC % tr == 0 and S % tr == 0
    base = (AH * HD + 2 * AKV * HD) // tc
    per = CCH // tc
    nrb = M // tr
    h8 = tr // 8

    def prev8(i):
        return jnp.maximum(i * h8 - 1, 0)

    def next8(i):
        return jnp.minimum((i + 1) * h8, M // 8 - 1)

    return pl.pallas_call(
        functools.partial(_conv_kernel, tr=tr),
        grid=(nrb, per),
        in_specs=[
            pl.BlockSpec((tr, tc), lambda i, j: (i, base + j)),
            pl.BlockSpec((tr, tc), lambda i, j: (i, base + per + j)),
            pl.BlockSpec((tr, tc), lambda i, j: (i, base + 2 * per + j)),
            pl.BlockSpec((8, tc), lambda i, j: (prev8(i), base + per + j)),
            pl.BlockSpec((8, tc), lambda i, j: (prev8(i), base + 2 * per + j)),
            pl.BlockSpec((8, tc), lambda i, j: (next8(i), base + per + j)),
            pl.BlockSpec((8, tc), lambda i, j: (next8(i), base + 2 * per + j)),
            pl.BlockSpec((3, tc), lambda i, j: (0, j)),
            pl.BlockSpec((1, tc), lambda i, j: (0, j)),
        ],
        out_specs=pl.BlockSpec((tr, tc), lambda i, j: (i, j)),
        out_shape=jax.ShapeDtypeStruct((M, CCH), BF16),
        compiler_params=_cparams(("arbitrary", "arbitrary")),
        name="conv",
    )(p, p, p, p, p, p, p, conv_w, conv_b.reshape(1, CCH))


def _log_sigmoid(d):
    return jnp.minimum(d, 0.0) - jnp.log(1.0 + jnp.exp(-jnp.abs(d)))


def _ret_chunk(decay_ref, q_ref, k_ref, v_ref, tab_ref, st_ref, h, backward):
    lg = _log_sigmoid(jnp.full((1, 1), decay_ref[h], F32))
    tab = tab_ref[...]
    q = _rope(q_ref[:, h * RDK:(h + 1) * RDK], tab).astype(BF16)
    k = (_rope(k_ref[:, h * RDK:(h + 1) * RDK], tab) * (RDK ** -0.5)).astype(BF16)
    v = v_ref[:, h * RDV:(h + 1) * RDV]
    ii = lax.broadcasted_iota(jnp.int32, (RC, RC), 0)
    jj = lax.broadcasted_iota(jnp.int32, (RC, RC), 1)
    ic = lax.broadcasted_iota(jnp.int32, (RC, 1), 0).astype(F32)
    if backward:
        rel = jj - ii
        keep = rel > 0
        xi = jnp.exp(lg * (RC - ic))
        zeta = jnp.exp(lg * ic)
    else:
        rel = ii - jj
        keep = rel >= 0
        xi = jnp.exp(lg * (ic + 1.0))
        zeta = jnp.exp(lg * (RC - 1.0 - ic))
    dmask = jnp.where(keep, jnp.exp(lg * jnp.where(keep, rel, 0).astype(F32)), 0.0)
    inner = lax.dot_general(q, k, (((1,), (1,)), ((), ())), preferred_element_type=F32) * dmask
    y = jnp.dot(inner.astype(BF16), v.astype(BF16), preferred_element_type=F32)
    st = st_ref[h]
    y = y + jnp.dot(q, st.astype(BF16), preferred_element_type=F32) * xi
    kv = lax.dot_general(k, (v * zeta).astype(BF16), (((0,), (0,)), ((), ())),
                         preferred_element_type=F32)
    st_ref[h] = jnp.exp(lg * float(RC)) * st + kv
    return y


def _ret_fwd_kernel(decay_ref, q_ref, k_ref, v_ref, tab_ref, y_ref, st_ref):
    @pl.when(pl.program_id(0) == 0)
    def _():
        st_ref[...] = jnp.zeros_like(st_ref)

    for h in range(RH):
        y_ref[:, h * RDV:(h + 1) * RDV] = _ret_chunk(decay_ref, q_ref, k_ref, v_ref, tab_ref,
                                                     st_ref, h, False)


def _ret_bwd_kernel(decay_ref, q_ref, k_ref, v_ref, tab_ref, yf_ref, g_ref, gnw_ref, o_ref, st_ref):
    @pl.when(pl.program_id(0) == 0)
    def _():
        st_ref[...] = jnp.zeros_like(st_ref)

    for h in range(RH):
        sl = slice(h * RDV, (h + 1) * RDV)
        y = yf_ref[:, sl] + _ret_chunk(decay_ref, q_ref, k_ref, v_ref, tab_ref, st_ref, h, True)
        mu = jnp.mean(y, axis=-1, keepdims=True)
        yc = y - mu
        var = jnp.mean(yc * yc, axis=-1, keepdims=True)
        yn = yc * lax.rsqrt(var + EPS) * gnw_ref[:, sl]
        gt = g_ref[:, sl]
        o_ref[:, sl] = (gt * jax.nn.sigmoid(gt) * yn).astype(o_ref.dtype)


def _retention(p, decay_f, decay_b, gn_w, tabs):
    nch = M // RC
    nlat = S // RC
    qw, vw = RH * RDK, RH * RDV
    fwd = lambda s: ((s + nlat) % nch)
    bwd = lambda s: (nch - 1 - s)
    smem = pl.BlockSpec(memory_space=pltpu.SMEM)

    def specs(cm):
        return [smem,
                pl.BlockSpec((RC, qw), lambda s: (cm(s), 0)),
                pl.BlockSpec((RC, qw), lambda s: (cm(s), 1)),
                pl.BlockSpec((RC, vw), lambda s: (cm(s), 1)),
                pl.BlockSpec((RC, 2 * HD), lambda s: (cm(s), 0))]

    state = pltpu.VMEM((RH, RDK, RDV), F32)
    yf = pl.pallas_call(
        _ret_fwd_kernel,
        grid=(nch,),
        in_specs=specs(fwd),
        out_specs=pl.BlockSpec((RC, vw), lambda s: (fwd(s), 0)),
        out_shape=jax.ShapeDtypeStruct((M, vw), F32),
        scratch_shapes=[state],
        compiler_params=_cparams(("arbitrary",)),
        name="ret_fwd",
    )(decay_f, p, p, p, tabs)
    return pl.pallas_call(
        _ret_bwd_kernel,
        grid=(nch,),
        in_specs=specs(bwd) + [pl.BlockSpec((RC, vw), lambda s: (bwd(s), 0)),
                               pl.BlockSpec((RC, vw), lambda s: (bwd(s), 2)),
                               pl.BlockSpec((1, vw), lambda s: (0, 0))],
        out_specs=pl.BlockSpec((RC, vw), lambda s: (bwd(s), 0)),
        out_shape=jax.ShapeDtypeStruct((M, vw), BF16),
        scratch_shapes=[state],
        compiler_params=_cparams(("arbitrary",)),
        name="ret_bwd",
    )(decay_b, p, p, p, tabs, yf, p, gn_w.reshape(1, vw))


_NA_NBLK = S // NQB
_NA_CASES = (0, 1, _NA_NBLK - 1)


def _na_window_row(b):
    return jnp.clip(NQROWS * b - NROWS // 2, 0, GROWS - NWIN_ROWS)


def _na_table_kernel(rpb_ref, o_ref):
    h = pl.program_id(0)
    n_dr, n_dc = 2 * NROWS - 1, 2 * NCOLS - 1
    cq = lax.broadcasted_iota(jnp.int32, (GW, 2 * GW), 0)
    lane = lax.broadcasted_iota(jnp.int32, (GW, 2 * GW), 1)
    right = lane >= GW
    kc = jnp.where(right, lane - GW, lane)
    dc = kc - cq + (NCOLS - 1)
    cs = jnp.clip(cq - NCOLS // 2, 0, GW - NCOLS)
    col_ok = (kc >= cs) & (kc < cs + NCOLS)
    memo = {}

    def row_scalar(a, b):
        return rpb_ref[(h * n_dr + a) * n_dc + b] if 0 <= a < n_dr else 0.0

    def pair(a0):
        if a0 not in memo:
            acc = jnp.zeros((GW, 2 * GW), F32)
            for b in range(n_dc):
                vec = jnp.where(right, row_scalar(a0 + 1, b), row_scalar(a0, b))
                acc = jnp.where(dc == b, vec, acc)
            memo[a0] = acc
        return memo[a0]

    for ci, blk in enumerate(_NA_CASES):
        w = int(np.clip(NQROWS * blk - NROWS // 2, 0, GROWS - NWIN_ROWS))
        for rl in range(NQROWS):
            r = NQROWS * blk + rl
            rs = int(np.clip(r - NROWS // 2, 0, GROWS - NROWS))
            for pr in range(NWIN_ROWS // 2):
                kr = w + 2 * pr
                ok_l = rs <= kr < rs + NROWS
                ok_r = rs <= kr + 1 < rs + NROWS
                if ok_l and ok_r:
                    ok = col_ok
                elif ok_l:
                    ok = col_ok & jnp.logical_not(right)
                elif ok_r:
                    ok = col_ok & right
                else:
                    ok = None
                if ok is None:
                    tile = jnp.full((GW, 2 * GW), NEG, F32)
                else:
                    tile = jnp.where(ok, pair(kr - r + NROWS - 1) * LOG2E, NEG)
                o_ref[ci, 0, rl * GW:(rl + 1) * GW, pr * 2 * GW:(pr + 1) * 2 * GW] = tile


def _na_tables(rpb):
    return pl.pallas_call(
        _na_table_kernel,
        grid=(NH,),
        in_specs=[pl.BlockSpec(memory_space=pltpu.SMEM)],
        out_specs=pl.BlockSpec((len(_NA_CASES), 1, NQB, NWIN), lambda h: (0, h, 0, 0)),
        out_shape=jax.ShapeDtypeStruct((len(_NA_CASES), NH, NQB, NWIN), F32),
        compiler_params=_cparams(("arbitrary",)),
        name="na_tables",
    )(rpb.reshape(-1))


def _na_kernel(q_ref, k_ref, v_ref, t_ref, o_ref):
    b = pl.program_id(1)
    scale2 = (HD ** -0.5) * LOG2E
    start = pl.multiple_of(_na_window_row(b) * GW, NQB)
    nt = (((1,), (1,)), ((), ()))
    for hh in range(NHPS):
        cs = slice(hh * HD, (hh + 1) * HD)
        q = q_ref[:, cs]
        s_loc = lax.dot_general(q, k_ref[pl.ds(start, NWIN), cs], nt,
                                preferred_element_type=F32) * scale2 + t_ref[0, hh]
        s_ctx = lax.dot_general(q, k_ref[S:M, cs], nt, preferred_element_type=F32) * scale2
        m = jnp.maximum(jnp.max(s_loc, axis=-1, keepdims=True), jnp.max(s_ctx, axis=-1, keepdims=True))
        e_loc = jnp.exp2(s_loc - m)
        e_ctx = jnp.exp2(s_ctx - m)
        den = jnp.sum(e_loc, axis=-1, keepdims=True) + jnp.sum(e_ctx, axis=-1, keepdims=True)
        o = (jnp.dot(e_loc.astype(BF16), v_ref[pl.ds(start, NWIN), cs], preferred_element_type=F32)
             + jnp.dot(e_ctx.astype(BF16), v_ref[S:M, cs], preferred_element_type=F32))
        o_ref[:, cs] = (o / den).astype(o_ref.dtype)


def _na(p, rpb):
    gw = NHPS * HD
    qcol = 0
    kcol = qcol + NH // NHPS
    vcol = kcol + NH // NHPS
    tables = _na_tables(rpb)

    def case(b):
        return jnp.minimum(b, 1) + jnp.maximum(b - (_NA_NBLK - 2), 0)

    return pl.pallas_call(
        _na_kernel,
        grid=(NH // NHPS, _NA_NBLK),
        in_specs=[
            pl.BlockSpec((NQB, gw), lambda h, b: (b, qcol + h)),
            pl.BlockSpec((M, gw), lambda h, b: (0, kcol + h)),
            pl.BlockSpec((M, gw), lambda h, b: (0, vcol + h)),
            pl.BlockSpec((1, NHPS, NQB, NWIN), lambda h, b: (case(b), h, 0, 0)),
        ],
        out_specs=pl.BlockSpec((NQB, gw), lambda h, b: (b, h)),
        out_shape=jax.ShapeDtypeStruct((S, NH * HD), BF16),
        compiler_params=_cparams(("arbitrary", "arbitrary")),
        name="na",
    )(p, p, p, tables)


def _tiles(rows):
    if rows == M:
        return dict(up=(2112, 1056), small=528, inp=1056)
    assert rows == S
    return dict(up=(2048, 1024), small=512, inp=1024)


def _ffn_half(h, hn, rows, mods, layer, third, wi, wo, nw_next, next_mod, cast_jobs=()):
    t = _tiles(rows)
    a, wob = _mm_swiglu(hn, wi, wo, layer, tm=t["up"][0], mm=t["up"][1], tn=512)
    return _ffn_down(h, rows, mods, layer, 3 * third + 2, a, wob, nw_next, next_mod, cast_jobs)


IN_TN = 1536


def kernel(x, c, ctx, c_ctx, ada_w, ada_b, norm_w, ffn_a_wi, ffn_a_wo, ffn_b_wi, ffn_b_wo,
           ev_w_in, ev_w_out, ev_sink, ev_conv_w, ev_conv_b,
           od_w_in, od_w_out, od_decay_f, od_decay_b, od_gn_w, od_rpb, final_norm_w):
    assert x.shape == (1, S, D) and ctx.shape == (1, LC, D) and ada_w.shape[0] == 2
    cvec = jnp.concatenate([c, c_ctx[None, :], jnp.zeros((6, D), F32)], axis=0)
    mods = _ada(cvec, ada_w, ada_b)
    tabs = _rope_tables()

    tm_in = _tiles(M)["inp"]
    hn, h = _normmod_first(x[0], ctx[0], norm_w[0, 0], mods)
    (h, hn), (w_in, w_out) = _ffn_half(h, hn, M, mods, 0, 0, ffn_a_wi, ffn_a_wo, norm_w[0, 1], (0, 1),
                                       cast_jobs=((ev_w_in, 0), (ev_w_out, 0)))
    p = _mm_plain(hn, w_in, 0, EV_IN, tm=tm_in, tn=IN_TN, out_dtype=F32)
    att = _attn(p, ev_sink[0], tabs)
    cnv = _conv(p, ev_conv_w[0], ev_conv_b[0])
    h, hn = _out_proj(h, M, mods, 0, att, cnv, w_out, norm_w[0, 2], tm=_tiles(M)["small"])
    (h, hn), _ = _ffn_half(h, hn, M, mods, 0, 2, ffn_b_wi, ffn_b_wo, norm_w[1, 0], (1, 0))

    (h, hn), (w_in, w_out) = _ffn_half(h, hn, M, mods, 1, 0, ffn_a_wi, ffn_a_wo, norm_w[1, 1], (1, 1),
                                       cast_jobs=((od_w_in, 0), (od_w_out, 0)))
    n_ret = 2 * RH * RDK + 2 * RH * RDV
    p_ret = _mm_plain(hn, w_in, 0, n_ret, tm=tm_in, tn=IN_TN, out_dtype=F32)
    p_na = _mm_plain(hn, w_in, n_ret, OD_IN - n_ret, tm=tm_in, tn=IN_TN, out_dtype=BF16)
    ret = _retention(p_ret, od_decay_f[0], od_decay_b[0], od_gn_w[0], tabs)
    nat = _na(p_na, od_rpb[0])
    h, hn = _out_proj(h, S, mods, 1, ret, nat, w_out, norm_w[1, 2], tm=_tiles(S)["small"])
    (out,), _ = _ffn_half(h, hn, S, mods, 1, 2, ffn_b_wi, ffn_b_wo, final_norm_w, None)
    return out[None]
```

```python
import functools

import numpy as np
import jax
import jax.numpy as jnp
from jax import lax
from jax.experimental import pallas as pl
from jax.experimental.pallas import tpu as pltpu

F32 = jnp.float32
BF16 = jnp.bfloat16

D = 2048
S = 8192
LC = 256
M = S + LC
GW = 64
GROWS = S // GW
HD = 128
DFF = 5632
NMOD = 9
EPS = 1e-6
ROPE_BASE = 10000.0
NEG = -1e30

AH, AKV, AWIN = 8, 2, 128
AQB = 256
CCH = 1024
EV_IN = AH * HD + 2 * AKV * HD + 3 * CCH
RH, RDK, RDV = 4, 128, 256
RC = 256
NH, NROWS, NCOLS = 8, 8, 16
NQROWS = 4
NQB = NQROWS * GW
NWIN_ROWS = NQROWS + NROWS
NWIN = NWIN_ROWS * GW
NHPS = 4
LOG2E = float(np.log2(np.e))
OD_IN = 2 * RH * RDK + 2 * RH * RDV + 3 * NH * HD

VMEM_LIMIT = 56 * 1024 * 1024


def _cparams(sem):
    return pltpu.CompilerParams(dimension_semantics=sem, vmem_limit_bytes=VMEM_LIMIT)


def _sub(r, size):
    return pl.ds(pl.multiple_of(r * size, 16), size)


def _ada_kernel(c_ref, w_ref, b_ref, o_ref):
    cv = c_ref[...]
    a = cv * jax.nn.sigmoid(cv)
    acc = jnp.dot(a.astype(BF16), w_ref[0].astype(BF16), preferred_element_type=F32)
    o_ref[0, 0] = acc + b_ref[0]


def _ada(cvec, ada_w, ada_b):
    depth = ada_w.shape[0]
    tn = 1024
    per = D // tn
    return pl.pallas_call(
        _ada_kernel,
        grid=(depth, NMOD * per),
        in_specs=[
            pl.BlockSpec((8, D), lambda l, j: (0, 0)),
            pl.BlockSpec((1, D, tn), lambda l, j: (l, 0, j)),
            pl.BlockSpec((1, 1, tn), lambda l, j: (l, 0, j)),
        ],
        out_specs=pl.BlockSpec((1, 1, 8, tn), lambda l, j: (l, j // per, 0, j % per)),
        out_shape=jax.ShapeDtypeStruct((depth, NMOD, 8, D), F32),
        compiler_params=_cparams(("arbitrary", "arbitrary")),
        name="ada",
    )(cvec, ada_w, ada_b.reshape(depth, 1, NMOD * D))


NORM_TR = 1024
NORM_CHUNK = 16
NORM_UNROLL = 4


def _norm_rows(src_ref, n_rows, fn, unroll=NORM_UNROLL):
    group = NORM_CHUNK * unroll

    def outer(gi, carry):
        for u in range(unroll):
            start = pl.multiple_of(gi * group + u * NORM_CHUNK, NORM_CHUNK)
            rs = pl.ds(start, NORM_CHUNK)
            fn(rs, src_ref[rs, :], start)
        return carry

    lax.fori_loop(0, n_rows // group, outer, 0)


def _normmod_first_kernel(x_ref, c_ref, nw_ref, mod_ref, o_ref, h_ref):
    i = pl.program_id(0)
    kind = (i * NORM_TR >= S).astype(jnp.int32)
    n_rows = jnp.minimum(M - i * NORM_TR, NORM_TR)
    shift = mod_ref[0, 0, pl.ds(kind, 1), :]
    gain = nw_ref[...] * (1.0 + mod_ref[0, 1, pl.ds(kind, 1), :])

    def emit(rs, x, start):
        rinv = lax.rsqrt(jnp.mean(x * x, axis=-1, keepdims=True) + EPS)
        o_ref[rs, :] = ((x * rinv) * gain + shift).astype(o_ref.dtype)
        h_ref[rs, :] = x

    @pl.when(kind == 0)
    def _():
        _norm_rows(x_ref, n_rows, emit)

    @pl.when(kind == 1)
    def _():
        _norm_rows(c_ref, n_rows, emit)


def _normmod_first(x, ctx, nw, mods):
    assert S % NORM_TR == 0 and LC % (NORM_CHUNK * NORM_UNROLL) == 0 and LC <= NORM_TR
    row_spec = pl.BlockSpec((NORM_TR, D), lambda i: (i, 0))
    return pl.pallas_call(
        _normmod_first_kernel,
        grid=(pl.cdiv(M, NORM_TR),),
        in_specs=[pl.BlockSpec((NORM_TR, D), lambda i: (jnp.minimum(i, S // NORM_TR - 1), 0)),
                  pl.BlockSpec((LC, D), lambda i: (0, 0)),
                  pl.BlockSpec((1, D), lambda i: (0, 0)),
                  pl.BlockSpec((1, 3, 8, D), lambda i: (0, 0, 0, 0))],
        out_specs=[row_spec, row_spec],
        out_shape=[jax.ShapeDtypeStruct((M, D), BF16), jax.ShapeDtypeStruct((M, D), F32)],
        compiler_params=_cparams(("arbitrary",)),
        name="normmod_first",
    )(x, ctx, nw.reshape(1, D), mods)


def _cast_weights(w_refs, wb_refs):
    @pl.when(pl.program_id(1) == 0)
    def _():
        for w_ref, wb_ref in zip(w_refs, wb_refs):
            wb_ref[...] = w_ref[...].astype(BF16)


def _mm_swiglu_kernel(a_ref, wg_ref, wu_ref, wo_ref, o_ref, wob_ref, wgb_ref, wub_ref, *, tm, mm):
    _cast_weights((wg_ref, wu_ref), (wgb_ref, wub_ref))
    wob_ref[...] = wo_ref[...].astype(BF16)

    def body(r, carry):
        a = a_ref[_sub(r, mm), :]
        g = jnp.dot(a, wgb_ref[...], preferred_element_type=F32)
        u = jnp.dot(a, wub_ref[...], preferred_element_type=F32)
        o_ref[_sub(r, mm), :] = (g * jax.nn.sigmoid(g) * u).astype(o_ref.dtype)
        return carry

    lax.fori_loop(0, tm // mm, body, 0, unroll=True)


def _mm_plain_kernel(a_ref, w_ref, o_ref):
    o_ref[...] = jnp.dot(a_ref[...], w_ref[...], preferred_element_type=F32).astype(o_ref.dtype)


def _mm_swiglu(a, wi, wo, layer, *, tm, mm, tn):
    rows, k = a.shape
    assert rows % tm == 0 and tm % mm == 0 and mm % 16 == 0
    n_out = wi.shape[2] // 2
    nj, ni = n_out // tn, rows // tm
    assert wo.shape[1] % (nj * ni) == 0
    slab = wo.shape[1] // (nj * ni)
    assert slab % 16 == 0
    return pl.pallas_call(
        functools.partial(_mm_swiglu_kernel, tm=tm, mm=mm),
        grid=(nj, ni),
        in_specs=[pl.BlockSpec((tm, k), lambda j, i: (i, 0)),
                  pl.BlockSpec((None, k, tn), lambda j, i: (layer, 0, j)),
                  pl.BlockSpec((None, k, tn), lambda j, i: (layer, 0, j + nj)),
                  pl.BlockSpec((None, slab, D), lambda j, i: (layer, j * ni + i, 0))],
        out_specs=[pl.BlockSpec((tm, tn), lambda j, i: (i, j)),
                   pl.BlockSpec((slab, D), lambda j, i: (j * ni + i, 0))],
        out_shape=[jax.ShapeDtypeStruct((rows, n_out), BF16),
                   jax.ShapeDtypeStruct((wo.shape[1], D), BF16)],
        scratch_shapes=[pltpu.VMEM((k, tn), BF16)] * 2,
        compiler_params=_cparams(("arbitrary", "arbitrary")),
        name="mm_swiglu",
    )(a, wi, wi, wo)


def _mm_plain(a, wb, col0, ncols, *, tm, tn, out_dtype):
    rows, k = a.shape
    assert rows % tm == 0 and ncols % tn == 0 and col0 % tn == 0
    return pl.pallas_call(
        _mm_plain_kernel,
        grid=(ncols // tn, rows // tm),
        in_specs=[pl.BlockSpec((tm, k), lambda j, i: (i, 0)),
                  pl.BlockSpec((k, tn), lambda j, i: (0, col0 // tn + j))],
        out_specs=pl.BlockSpec((tm, tn), lambda j, i: (i, j)),
        out_shape=jax.ShapeDtypeStruct((rows, ncols), out_dtype),
        compiler_params=_cparams(("arbitrary", "arbitrary")),
        name="mm_plain",
    )(a, wb)


def _gate_rows(gate_ref, i, tm):
    rows = i * tm + lax.broadcasted_iota(jnp.int32, (tm, 1), 0)
    return jnp.where(rows >= S, gate_ref[0, 0, 1:2, :], gate_ref[0, 0, 0:1, :])


DOWN_TM = 256


def _ffn_down_kernel(*refs, n_jobs, final):
    h_ref, gate_ref, a_ref, w_ref, nw_ref = refs[:5]
    n_in = 5 if final else 6
    o_ref = refs[n_in + n_jobs]
    n_out = 1 if final else 2
    first_job_out = n_in + n_jobs + n_out
    for src_ref, dst_ref in zip(refs[n_in:n_in + n_jobs], refs[first_job_out:first_job_out + n_jobs]):
        dst_ref[...] = src_ref[...].astype(BF16)
    kind = (pl.program_id(0) * DOWN_TM >= S).astype(jnp.int32)
    acc = jnp.dot(a_ref[...], w_ref[...], preferred_element_type=F32)
    new_ref = refs[-1] if final else o_ref
    new_ref[...] = h_ref[...] + (0.5 * gate_ref[0, 0, pl.ds(kind, 1), :]) * acc
    nw = nw_ref[...]
    if final:
        def emit(rs, x, start):
            rinv = lax.rsqrt(jnp.mean(x * x, axis=-1, keepdims=True) + EPS)
            o_ref[rs, :] = x * rinv * nw
    else:
        mod_ref, hn_ref = refs[5], refs[n_in + n_jobs + 1]
        shift = mod_ref[0, 0, pl.ds(kind, 1), :]
        gain = nw * (1.0 + mod_ref[0, 1, pl.ds(kind, 1), :])

        def emit(rs, x, start):
            rinv = lax.rsqrt(jnp.mean(x * x, axis=-1, keepdims=True) + EPS)
            hn_ref[rs, :] = ((x * rinv) * gain + shift).astype(hn_ref.dtype)

    _norm_rows(new_ref, DOWN_TM, emit)


def _ffn_down(h, rows, mods, layer, gate_idx, a, wb, nw_next, next_mod=None, cast_jobs=()):
    k = a.shape[1]
    final = next_mod is None
    tm = DOWN_TM
    assert rows % tm == 0 and S % tm == 0 and tm % (NORM_CHUNK * NORM_UNROLL) == 0
    n_slabs = S // tm
    row = lambda i: (i, 0)
    job_in, job_out, job_shape = [], [], []
    for w, idx in cast_jobs:
        kk, nn = w.shape[1:]
        assert kk % n_slabs == 0 and (kk // n_slabs) % 16 == 0
        slab = kk // n_slabs
        job_in.append(pl.BlockSpec((None, slab, nn),
                                   lambda i, idx=idx: (idx, jnp.minimum(i, n_slabs - 1), 0)))
        job_out.append(pl.BlockSpec((slab, nn), lambda i: (jnp.minimum(i, n_slabs - 1), 0)))
        job_shape.append(jax.ShapeDtypeStruct((kk, nn), BF16))
    in_specs = [pl.BlockSpec((tm, D), row),
                pl.BlockSpec((1, 1, 8, D), lambda i: (layer, gate_idx, 0, 0)),
                pl.BlockSpec((tm, k), row),
                pl.BlockSpec((k, D), lambda i: (0, 0), pipeline_mode=pl.Buffered(1)),
                pl.BlockSpec((1, D), lambda i: (0, 0))]
    args = [h, mods, a, wb, nw_next.reshape(1, D)]
    out_specs = [pl.BlockSpec((tm, D), row)]
    out_shape = [jax.ShapeDtypeStruct((rows, D), F32)]
    if not final:
        in_specs.append(pl.BlockSpec((1, 3, 8, D), lambda i: (next_mod[0], next_mod[1], 0, 0)))
        args.append(mods)
        out_specs.append(pl.BlockSpec((tm, D), row))
        out_shape.append(jax.ShapeDtypeStruct((rows, D), BF16))
    n_main = len(out_shape)
    res = pl.pallas_call(
        functools.partial(_ffn_down_kernel, n_jobs=len(cast_jobs), final=final),
        grid=(rows // tm,),
        in_specs=in_specs + job_in,
        out_specs=out_specs + job_out,
        out_shape=out_shape + job_shape,
        scratch_shapes=[pltpu.VMEM((tm, D), F32)] if final else [],
        compiler_params=_cparams(("arbitrary",)),
        name="ffn_down_final" if final else "ffn_down",
    )(*args, *[w for w, _ in cast_jobs])
    return tuple(res[:n_main]), list(res[n_main:])


def _out_proj_kernel(h_ref, gate_ref, a1_ref, a2_ref, w1_ref, w2_ref, nw_ref, mod_ref,
                     o_ref, hn_ref, *, tm, unroll):
    i = pl.program_id(0)
    acc = (jnp.dot(a1_ref[...], w1_ref[...], preferred_element_type=F32)
           + jnp.dot(a2_ref[...], w2_ref[...], preferred_element_type=F32))
    o_ref[...] = h_ref[...] + _gate_rows(gate_ref, i, tm) * acc
    nw = nw_ref[...]
    gain_lat = nw * (1.0 + mod_ref[0, 1, 0:1, :])
    gain_ctx = nw * (1.0 + mod_ref[0, 1, 1:2, :])
    shift_lat = mod_ref[0, 0, 0:1, :]
    shift_ctx = mod_ref[0, 0, 1:2, :]

    def emit(rs, x, start):
        isc = (i * tm + start + lax.broadcasted_iota(jnp.int32, (NORM_CHUNK, 1), 0)) >= S
        rinv = lax.rsqrt(jnp.mean(x * x, axis=-1, keepdims=True) + EPS)
        gain = jnp.where(isc, gain_ctx, gain_lat)
        shift = jnp.where(isc, shift_ctx, shift_lat)
        hn_ref[rs, :] = ((x * rinv) * gain + shift).astype(hn_ref.dtype)

    _norm_rows(o_ref, tm, emit, unroll=unroll)


def _out_proj(h, rows, mods, layer, a1, a2, wb, nw_next, *, tm):
    k = a1.shape[1]
    assert a2.shape[1] == k and wb.shape == (2 * k, D) and rows % tm == 0
    unroll = 4 if tm % (4 * NORM_CHUNK) == 0 else 3
    row = lambda i: (i, 0)
    return pl.pallas_call(
        functools.partial(_out_proj_kernel, tm=tm, unroll=unroll),
        grid=(rows // tm,),
        in_specs=[pl.BlockSpec((tm, D), row),
                  pl.BlockSpec((1, 1, 8, D), lambda i: (layer, 5, 0, 0)),
                  pl.BlockSpec((tm, k), row),
                  pl.BlockSpec((tm, k), row),
                  pl.BlockSpec((k, D), lambda i: (0, 0)),
                  pl.BlockSpec((k, D), lambda i: (1, 0)),
                  pl.BlockSpec((1, D), lambda i: (0, 0)),
                  pl.BlockSpec((1, 3, 8, D), lambda i: (layer, 2, 0, 0))],
        out_specs=[pl.BlockSpec((tm, D), row), pl.BlockSpec((tm, D), row)],
        out_shape=[jax.ShapeDtypeStruct((rows, D), F32), jax.ShapeDtypeStruct((rows, D), BF16)],
        compiler_params=_cparams(("arbitrary",)),
        name="out_proj",
    )(h, mods, a1, a2, wb, wb, nw_next.reshape(1, D), mods)


def _rope_tables():
    t = np.arange(S)
    rows = (t // GW).astype(np.float32)
    cols = (t % GW).astype(np.float32)
    half = HD // 2
    inv = np.float32(ROPE_BASE) ** (-np.arange(0, half, 2, dtype=np.float32) / np.float32(half))
    ar = rows[:, None] * inv
    ac = cols[:, None] * inv
    cos = np.concatenate([np.cos(ar), np.cos(ar), np.cos(ac), np.cos(ac)], axis=-1)
    sin = np.concatenate([-np.sin(ar), np.sin(ar), -np.sin(ac), np.sin(ac)], axis=-1)
    lat = np.concatenate([cos, sin], axis=-1)
    ctx = np.concatenate([np.ones((LC, HD), np.float32), np.zeros((LC, HD), np.float32)], axis=-1)
    return jnp.asarray(np.concatenate([lat, ctx], axis=0).astype(np.float32))


def _rope(x, tab):
    c = tab[:, :HD]
    sg = tab[:, HD:]
    lane = lax.broadcasted_iota(jnp.int32, x.shape, 1)
    first = (lane % (HD // 2)) < (HD // 4)
    partner = jnp.where(first, pltpu.roll(x, HD - HD // 4, 1), pltpu.roll(x, HD // 4, 1))
    return x * c + partner * sg


def _mm_rope_kernel(a_ref, w_ref, tab_ref, o_ref, *, n_rot):
    acc = jnp.dot(a_ref[...], w_ref[...], preferred_element_type=F32)
    tab = tab_ref[...]
    for hh in range(acc.shape[1] // HD):
        x = acc[:, hh * HD:(hh + 1) * HD]
        o_ref[:, hh * HD:(hh + 1) * HD] = (_rope(x, tab) if hh < n_rot else x).astype(o_ref.dtype)


def _mm_rope(a, wb, ncols, tabs, n_rot, *, tm):
    rows, k = a.shape
    assert rows % tm == 0 and ncols % HD == 0
    return pl.pallas_call(
        functools.partial(_mm_rope_kernel, n_rot=n_rot),
        grid=(rows // tm,),
        in_specs=[pl.BlockSpec((tm, k), lambda i: (i, 0)),
                  pl.BlockSpec((k, ncols), lambda i: (0, 0)),
                  pl.BlockSpec((tm, 2 * HD), lambda i: (i, 0))],
        out_specs=pl.BlockSpec((tm, ncols), lambda i: (i, 0)),
        out_shape=jax.ShapeDtypeStruct((rows, ncols), BF16),
        compiler_params=_cparams(("arbitrary",)),
        name="mm_rope",
    )(a, wb, tabs)


def _attn_kernel(sink_ref, q_ref, kp_ref, kc_ref, kn_ref, vp_ref, vc_ref, vn_ref, kx_ref, vx_ref, o_ref):
    g = pl.program_id(0)
    n = pl.program_id(1)
    nlat = S // AQB
    scale2 = (HD ** -0.5) * LOG2E
    k_all = jnp.concatenate([kp_ref[...], kc_ref[...], kn_ref[...], kx_ref[...]], axis=0)
    v_all = jnp.concatenate([vp_ref[...], vc_ref[...], vn_ref[...], vx_ref[...]], axis=0)
    nloc = AQB + 2 * AWIN
    nk = nloc + LC
    r = lax.broadcasted_iota(jnp.int32, (AQB, nk), 0)
    c = lax.broadcasted_iota(jnp.int32, (AQB, nk), 1)
    rel = c - AWIN - r
    pos = n * AQB - AWIN + c
    hi = jnp.where(n < nlat, S, 0)
    ok = ((jnp.abs(rel) <= AWIN) & (pos >= 0) & (pos < hi)) | (c >= nloc)
    for hh in range(AH // AKV):
        q = q_ref[:, hh * HD:(hh + 1) * HD]
        s = lax.dot_general(q, k_all, (((1,), (1,)), ((), ())), preferred_element_type=F32) * scale2
        s = jnp.where(ok, s, NEG)
        sk = sink_ref[g * (AH // AKV) + hh] * LOG2E
        m = jnp.maximum(jnp.max(s, axis=-1, keepdims=True), sk)
        e = jnp.exp2(s - m)
        den = jnp.sum(e, axis=-1, keepdims=True) + jnp.exp2(sk - m)
        o = jnp.dot(e.astype(BF16), v_all, preferred_element_type=F32)
        o_ref[:, hh * HD:(hh + 1) * HD] = (o / den).astype(o_ref.dtype)


def _attn(p, sink):
    assert AQB == LC and AQB == 2 * AWIN
    nlat = S // AQB
    qw = (AH // AKV) * HD
    kcol = AH
    vcol = kcol + AKV
    last_w = S // AWIN - 1

    def prev(n):
        return jnp.clip(2 * n - 1, 0, last_w)

    def cur(n):
        return jnp.minimum(n, nlat - 1)

    def nxt(n):
        return jnp.clip(2 * n + 2, 0, last_w)

    return pl.pallas_call(
        _attn_kernel,
        grid=(AKV, M // AQB),
        in_specs=[
            pl.BlockSpec(memory_space=pltpu.SMEM),
            pl.BlockSpec((AQB, qw), lambda g, n: (n, g)),
            pl.BlockSpec((AWIN, HD), lambda g, n: (prev(n), kcol + g)),
            pl.BlockSpec((AQB, HD), lambda g, n: (cur(n), kcol + g)),
            pl.BlockSpec((AWIN, HD), lambda g, n: (nxt(n), kcol + g)),
            pl.BlockSpec((AWIN, HD), lambda g, n: (prev(n), vcol + g)),
            pl.BlockSpec((AQB, HD), lambda g, n: (cur(n), vcol + g)),
            pl.BlockSpec((AWIN, HD), lambda g, n: (nxt(n), vcol + g)),
            pl.BlockSpec((LC, HD), lambda g, n: (S // LC, kcol + g)),
            pl.BlockSpec((LC, HD), lambda g, n: (S // LC, vcol + g)),
        ],
        out_specs=pl.BlockSpec((AQB, qw), lambda g, n: (n, g)),
        out_shape=jax.ShapeDtypeStruct((M, AH * HD), BF16),
        compiler_params=_cparams(("arbitrary", "arbitrary")),
        name="attn",
    )(sink, p, p, p, p, p, p, p, p, p)


def _conv_kernel(b_ref, c_ref, u_ref, cp_ref, up_ref, cn_ref, un_ref, w_ref, bias_ref, o_ref, *, tr):
    i = pl.program_id(0)
    lat_blocks = S // tr
    z = c_ref[...] * u_ref[...]
    zp = cp_ref[7:8, :] * up_ref[7:8, :]
    zn = cn_ref[0:1, :] * un_ref[0:1, :]
    has_prev = jnp.logical_and(i != 0, i != lat_blocks)
    has_next = jnp.logical_and(i != lat_blocks - 1, i != M // tr - 1)
    zp = jnp.where(has_prev, zp, 0.0)
    zn = jnp.where(has_next, zn, 0.0)
    row = lax.broadcasted_iota(jnp.int32, z.shape, 0)
    z_m1 = jnp.where(row == 0, zp, pltpu.roll(z, 1, 0))
    z_p1 = jnp.where(row == tr - 1, zn, pltpu.roll(z, tr - 1, 0))
    conv = z_m1 * w_ref[0:1, :] + z * w_ref[1:2, :] + z_p1 * w_ref[2:3, :] + bias_ref[...]
    o_ref[...] = (b_ref[...] * conv).astype(o_ref.dtype)


def _conv(p, conv_w, conv_b):
    tr = 256
    assert LC % tr == 0 and S % tr == 0
    h8 = tr // 8

    def prev8(i):
        return jnp.maximum(i * h8 - 1, 0)

    def next8(i):
        return jnp.minimum((i + 1) * h8, M // 8 - 1)

    return pl.pallas_call(
        functools.partial(_conv_kernel, tr=tr),
        grid=(M // tr,),
        in_specs=[
            pl.BlockSpec((tr, CCH), lambda i: (i, 0)),
            pl.BlockSpec((tr, CCH), lambda i: (i, 1)),
            pl.BlockSpec((tr, CCH), lambda i: (i, 2)),
            pl.BlockSpec((8, CCH), lambda i: (prev8(i), 1)),
            pl.BlockSpec((8, CCH), lambda i: (prev8(i), 2)),
            pl.BlockSpec((8, CCH), lambda i: (next8(i), 1)),
            pl.BlockSpec((8, CCH), lambda i: (next8(i), 2)),
            pl.BlockSpec((3, CCH), lambda i: (0, 0)),
            pl.BlockSpec((1, CCH), lambda i: (0, 0)),
        ],
        out_specs=pl.BlockSpec((tr, CCH), lambda i: (i, 0)),
        out_shape=jax.ShapeDtypeStruct((M, CCH), BF16),
        compiler_params=_cparams(("arbitrary",)),
        name="conv",
    )(p, p, p, p, p, p, p, conv_w, conv_b.reshape(1, CCH))


def _log_sigmoid(d):
    return jnp.minimum(d, 0.0) - jnp.log(1.0 + jnp.exp(-jnp.abs(d)))


def _ret_chunk(decay_ref, q_ref, k_ref, v_ref, tab_ref, st_ref, h, backward):
    lg = _log_sigmoid(jnp.full((1, 1), decay_ref[h], F32))
    tab = tab_ref[...]
    q = _rope(q_ref[:, h * RDK:(h + 1) * RDK], tab).astype(BF16)
    k = (_rope(k_ref[:, h * RDK:(h + 1) * RDK], tab) * (RDK ** -0.5)).astype(BF16)
    v = v_ref[:, h * RDV:(h + 1) * RDV]
    ii = lax.broadcasted_iota(jnp.int32, (RC, RC), 0)
    jj = lax.broadcasted_iota(jnp.int32, (RC, RC), 1)
    ic = lax.broadcasted_iota(jnp.int32, (RC, 1), 0).astype(F32)
    if backward:
        rel = jj - ii
        keep = rel > 0
        xi = jnp.exp(lg * (RC - ic))
        zeta = jnp.exp(lg * ic)
    else:
        rel = ii - jj
        keep = rel >= 0
        xi = jnp.exp(lg * (ic + 1.0))
        zeta = jnp.exp(lg * (RC - 1.0 - ic))
    dmask = jnp.where(keep, jnp.exp(lg * jnp.where(keep, rel, 0).astype(F32)), 0.0)
    inner = lax.dot_general(q, k, (((1,), (1,)), ((), ())), preferred_element_type=F32) * dmask
    y = jnp.dot(inner.astype(BF16), v.astype(BF16), preferred_element_type=F32)
    st = st_ref[h]
    y = y + jnp.dot(q, st.astype(BF16), preferred_element_type=F32) * xi
    kv = lax.dot_general(k, (v * zeta).astype(BF16), (((0,), (0,)), ((), ())),
                         preferred_element_type=F32)
    st_ref[h] = jnp.exp(lg * float(RC)) * st + kv
    return y


def _ret_fwd_kernel(decay_ref, q_ref, k_ref, v_ref, tab_ref, y_ref, st_ref):
    @pl.when(pl.program_id(0) == 0)
    def _():
        st_ref[...] = jnp.zeros_like(st_ref)

    for h in range(RH):
        y_ref[:, h * RDV:(h + 1) * RDV] = _ret_chunk(decay_ref, q_ref, k_ref, v_ref, tab_ref,
                                                     st_ref, h, False)


def _ret_bwd_kernel(decay_ref, q_ref, k_ref, v_ref, tab_ref, yf_ref, g_ref, gnw_ref, o_ref, st_ref):
    @pl.when(pl.program_id(0) == 0)
    def _():
        st_ref[...] = jnp.zeros_like(st_ref)

    for h in range(RH):
        sl = slice(h * RDV, (h + 1) * RDV)
        y = yf_ref[:, sl] + _ret_chunk(decay_ref, q_ref, k_ref, v_ref, tab_ref, st_ref, h, True)
        mu = jnp.mean(y, axis=-1, keepdims=True)
        yc = y - mu
        var = jnp.mean(yc * yc, axis=-1, keepdims=True)
        yn = yc * lax.rsqrt(var + EPS) * gnw_ref[:, sl]
        gt = g_ref[:, sl]
        o_ref[:, sl] = (gt * jax.nn.sigmoid(gt) * yn).astype(o_ref.dtype)


def _retention(p, decay_f, decay_b, gn_w, tabs):
    nch = M // RC
    nlat = S // RC
    qw, vw = RH * RDK, RH * RDV
    fwd = lambda s: ((s + nlat) % nch)
    bwd = lambda s: (nch - 1 - s)
    smem = pl.BlockSpec(memory_space=pltpu.SMEM)

    def specs(cm):
        return [smem,
                pl.BlockSpec((RC, qw), lambda s: (cm(s), 0)),
                pl.BlockSpec((RC, qw), lambda s: (cm(s), 1)),
                pl.BlockSpec((RC, vw), lambda s: (cm(s), 1)),
                pl.BlockSpec((RC, 2 * HD), lambda s: (cm(s), 0))]

    state = pltpu.VMEM((RH, RDK, RDV), F32)
    yf = pl.pallas_call(
        _ret_fwd_kernel,
        grid=(nch,),
        in_specs=specs(fwd),
        out_specs=pl.BlockSpec((RC, vw), lambda s: (fwd(s), 0)),
        out_shape=jax.ShapeDtypeStruct((M, vw), F32),
        scratch_shapes=[state],
        compiler_params=_cparams(("arbitrary",)),
        name="ret_fwd",
    )(decay_f, p, p, p, tabs)
    return pl.pallas_call(
        _ret_bwd_kernel,
        grid=(nch,),
        in_specs=specs(bwd) + [pl.BlockSpec((RC, vw), lambda s: (bwd(s), 0)),
                               pl.BlockSpec((RC, vw), lambda s: (bwd(s), 2)),
                               pl.BlockSpec((1, vw), lambda s: (0, 0))],
        out_specs=pl.BlockSpec((RC, vw), lambda s: (bwd(s), 0)),
        out_shape=jax.ShapeDtypeStruct((M, vw), BF16),
        scratch_shapes=[state],
        compiler_params=_cparams(("arbitrary",)),
        name="ret_bwd",
    )(decay_b, p, p, p, tabs, yf, p, gn_w.reshape(1, vw))


_NA_NBLK = S // NQB
_NA_CASES = (0, 1, _NA_NBLK - 1)


def _na_window_row(b):
    return jnp.clip(NQROWS * b - NROWS // 2, 0, GROWS - NWIN_ROWS)


def _na_table_kernel(rpb_ref, o_ref):
    h = pl.program_id(0)
    n_dr, n_dc = 2 * NROWS - 1, 2 * NCOLS - 1
    cq = lax.broadcasted_iota(jnp.int32, (GW, 2 * GW), 0)
    lane = lax.broadcasted_iota(jnp.int32, (GW, 2 * GW), 1)
    right = lane >= GW
    kc = jnp.where(right, lane - GW, lane)
    dc = kc - cq + (NCOLS - 1)
    cs = jnp.clip(cq - NCOLS // 2, 0, GW - NCOLS)
    col_ok = (kc >= cs) & (kc < cs + NCOLS)
    memo = {}

    def row_scalar(a, b):
        return rpb_ref[(h * n_dr + a) * n_dc + b] if 0 <= a < n_dr else 0.0

    def pair(a0):
        if a0 not in memo:
            acc = jnp.zeros((GW, 2 * GW), F32)
            for b in range(n_dc):
                vec = jnp.where(right, row_scalar(a0 + 1, b), row_scalar(a0, b))
                acc = jnp.where(dc == b, vec, acc)
            memo[a0] = acc
        return memo[a0]

    for ci, blk in enumerate(_NA_CASES):
        w = int(np.clip(NQROWS * blk - NROWS // 2, 0, GROWS - NWIN_ROWS))
        for rl in range(NQROWS):
            r = NQROWS * blk + rl
            rs = int(np.clip(r - NROWS // 2, 0, GROWS - NROWS))
            for pr in range(NWIN_ROWS // 2):
                kr = w + 2 * pr
                ok_l = rs <= kr < rs + NROWS
                ok_r = rs <= kr + 1 < rs + NROWS
                if ok_l and ok_r:
                    ok = col_ok
                elif ok_l:
                    ok = col_ok & jnp.logical_not(right)
                elif ok_r:
                    ok = col_ok & right
                else:
                    ok = None
                if ok is None:
                    tile = jnp.full((GW, 2 * GW), NEG, F32)
                else:
                    tile = jnp.where(ok, pair(kr - r + NROWS - 1) * LOG2E, NEG)
                o_ref[ci, 0, rl * GW:(rl + 1) * GW, pr * 2 * GW:(pr + 1) * 2 * GW] = tile


def _na_tables(rpb):
    return pl.pallas_call(
        _na_table_kernel,
        grid=(NH,),
        in_specs=[pl.BlockSpec(memory_space=pltpu.SMEM)],
        out_specs=pl.BlockSpec((len(_NA_CASES), 1, NQB, NWIN), lambda h: (0, h, 0, 0)),
        out_shape=jax.ShapeDtypeStruct((len(_NA_CASES), NH, NQB, NWIN), F32),
        compiler_params=_cparams(("arbitrary",)),
        name="na_tables",
    )(rpb.reshape(-1))


def _na_kernel(q_ref, k_ref, v_ref, t_ref, o_ref):
    b = pl.program_id(1)
    scale2 = (HD ** -0.5) * LOG2E
    start = pl.multiple_of(_na_window_row(b) * GW, NQB)
    nt = (((1,), (1,)), ((), ()))
    for hh in range(NHPS):
        cs = slice(hh * HD, (hh + 1) * HD)
        q = q_ref[:, cs]
        s_loc = lax.dot_general(q, k_ref[pl.ds(start, NWIN), cs], nt,
                                preferred_element_type=F32) * scale2 + t_ref[0, hh]
        s_ctx = lax.dot_general(q, k_ref[S:M, cs], nt, preferred_element_type=F32) * scale2
        m = jnp.maximum(jnp.max(s_loc, axis=-1, keepdims=True), jnp.max(s_ctx, axis=-1, keepdims=True))
        e_loc = jnp.exp2(s_loc - m)
        e_ctx = jnp.exp2(s_ctx - m)
        den = jnp.sum(e_loc, axis=-1, keepdims=True) + jnp.sum(e_ctx, axis=-1, keepdims=True)
        o = (jnp.dot(e_loc.astype(BF16), v_ref[pl.ds(start, NWIN), cs], preferred_element_type=F32)
             + jnp.dot(e_ctx.astype(BF16), v_ref[S:M, cs], preferred_element_type=F32))
        o_ref[:, cs] = (o / den).astype(o_ref.dtype)


def _na(p, rpb):
    gw = NHPS * HD
    qcol = 0
    kcol = qcol + NH // NHPS
    vcol = kcol + NH // NHPS
    tables = _na_tables(rpb)

    def case(b):
        return jnp.minimum(b, 1) + jnp.maximum(b - (_NA_NBLK - 2), 0)

    return pl.pallas_call(
        _na_kernel,
        grid=(NH // NHPS, _NA_NBLK),
        in_specs=[
            pl.BlockSpec((NQB, gw), lambda h, b: (b, qcol + h)),
            pl.BlockSpec((M, gw), lambda h, b: (0, kcol + h)),
            pl.BlockSpec((M, gw), lambda h, b: (0, vcol + h)),
            pl.BlockSpec((1, NHPS, NQB, NWIN), lambda h, b: (case(b), h, 0, 0)),
        ],
        out_specs=pl.BlockSpec((NQB, gw), lambda h, b: (b, h)),
        out_shape=jax.ShapeDtypeStruct((S, NH * HD), BF16),
        compiler_params=_cparams(("arbitrary", "arbitrary")),
        name="na",
    )(p, p, p, tables)


def _tiles(rows):
    if rows == M:
        return dict(up=(2112, 1056), small=528, inp=1056)
    assert rows == S
    return dict(up=(2048, 1024), small=512, inp=1024)


def _ffn_half(h, hn, rows, mods, layer, third, wi, wo, nw_next, next_mod, cast_jobs=()):
    t = _tiles(rows)
    a, wob = _mm_swiglu(hn, wi, wo, layer, tm=t["up"][0], mm=t["up"][1], tn=512)
    return _ffn_down(h, rows, mods, layer, 3 * third + 2, a, wob, nw_next, next_mod, cast_jobs)


IN_TN = 1536


def kernel(x, c, ctx, c_ctx, ada_w, ada_b, norm_w, ffn_a_wi, ffn_a_wo, ffn_b_wi, ffn_b_wo,
           ev_w_in, ev_w_out, ev_sink, ev_conv_w, ev_conv_b,
           od_w_in, od_w_out, od_decay_f, od_decay_b, od_gn_w, od_rpb, final_norm_w):
    assert x.shape == (1, S, D) and ctx.shape == (1, LC, D) and ada_w.shape[0] == 2
    cvec = jnp.concatenate([c, c_ctx[None, :], jnp.zeros((6, D), F32)], axis=0)
    mods = _ada(cvec, ada_w, ada_b)
    tabs = _rope_tables()

    tm_in = _tiles(M)["inp"]
    hn, h = _normmod_first(x[0], ctx[0], norm_w[0, 0], mods)
    (h, hn), (w_in, w_out) = _ffn_half(h, hn, M, mods, 0, 0, ffn_a_wi, ffn_a_wo, norm_w[0, 1], (0, 1),
                                       cast_jobs=((ev_w_in, 0), (ev_w_out, 0)))
    n_att = (AH + 2 * AKV) * HD
    p_att = _mm_rope(hn, w_in, n_att, tabs, AH + AKV, tm=tm_in)
    p_conv = _mm_plain(hn, w_in, n_att, EV_IN - n_att, tm=tm_in, tn=IN_TN, out_dtype=F32)
    att = _attn(p_att, ev_sink[0])
    cnv = _conv(p_conv, ev_conv_w[0], ev_conv_b[0])
    h, hn = _out_proj(h, M, mods, 0, att, cnv, w_out, norm_w[0, 2], tm=_tiles(M)["small"])
    (h, hn), _ = _ffn_half(h, hn, M, mods, 0, 2, ffn_b_wi, ffn_b_wo, norm_w[1, 0], (1, 0))

    (h, hn), (w_in, w_out) = _ffn_half(h, hn, M, mods, 1, 0, ffn_a_wi, ffn_a_wo, norm_w[1, 1], (1, 1),
                                       cast_jobs=((od_w_in, 0), (od_w_out, 0)))
    n_ret = 2 * RH * RDK + 2 * RH * RDV
    p_ret = _mm_plain(hn, w_in, 0, n_ret, tm=tm_in, tn=IN_TN, out_dtype=F32)
    p_na = _mm_plain(hn, w_in, n_ret, OD_IN - n_ret, tm=tm_in, tn=IN_TN, out_dtype=BF16)
    ret = _retention(p_ret, od_decay_f[0], od_decay_b[0], od_gn_w[0], tabs)
    nat = _na(p_na, od_rpb[0])
    h, hn = _out_proj(h, S, mods, 1, ret, nat, w_out, norm_w[1, 2], tm=_tiles(S)["small"])
    (out,), _ = _ffn_half(h, hn, S, mods, 1, 2, ffn_b_wi, ffn_b_wo, final_norm_w, None)
    return out[None]
```

```python
import functools

import numpy as np
import jax
import jax.numpy as jnp
from jax import lax
from jax.experimental import pallas as pl
from jax.experimental.pallas import tpu as pltpu

F32 = jnp.float32
BF16 = jnp.bfloat16

D = 2048
S = 8192
LC = 256
M = S + LC
GW = 64
GROWS = S // GW
HD = 128
DFF = 5632
NMOD = 9
EPS = 1e-6
ROPE_BASE = 10000.0
NEG = -1e30

AH, AKV, AWIN = 8, 2, 128
AQB = 256
CCH = 1024
EV_IN = AH * HD + 2 * AKV * HD + 3 * CCH
RH, RDK, RDV = 4, 128, 256
RC = 256
NH, NROWS, NCOLS = 8, 8, 16
NQROWS = 4
NQB = NQROWS * GW
NWIN_ROWS = NQROWS + NROWS
NWIN = NWIN_ROWS * GW
NHPS = 4
LOG2E = float(np.log2(np.e))
OD_IN = 2 * RH * RDK + 2 * RH * RDV + 3 * NH * HD

VMEM_LIMIT = 56 * 1024 * 1024


def _cparams(sem):
    return pltpu.CompilerParams(dimension_semantics=sem, vmem_limit_bytes=VMEM_LIMIT)


def _sub(r, size):
    return pl.ds(pl.multiple_of(r * size, 16), size)


def _ada_kernel(c_ref, w_ref, b_ref, o_ref):
    cv = c_ref[...]
    a = cv * jax.nn.sigmoid(cv)
    acc = jnp.dot(a.astype(BF16), w_ref[0].astype(BF16), preferred_element_type=F32)
    o_ref[0, 0] = acc + b_ref[0]


def _ada(cvec, ada_w, ada_b):
    depth = ada_w.shape[0]
    tn = D
    per = D // tn
    return pl.pallas_call(
        _ada_kernel,
        grid=(depth, NMOD * per),
        in_specs=[
            pl.BlockSpec((8, D), lambda l, j: (0, 0)),
            pl.BlockSpec((1, D, tn), lambda l, j: (l, 0, j)),
            pl.BlockSpec((1, 1, tn), lambda l, j: (l, 0, j)),
        ],
        out_specs=pl.BlockSpec((1, 1, 8, tn), lambda l, j: (l, j // per, 0, j % per)),
        out_shape=jax.ShapeDtypeStruct((depth, NMOD, 8, D), F32),
        compiler_params=_cparams(("arbitrary", "arbitrary")),
        name="ada",
    )(cvec, ada_w, ada_b.reshape(depth, 1, NMOD * D))


NORM_TR = 1024
NORM_CHUNK = 16
NORM_UNROLL = 4


def _norm_rows(src_ref, n_rows, fn, unroll=NORM_UNROLL):
    group = NORM_CHUNK * unroll

    def outer(gi, carry):
        for u in range(unroll):
            start = pl.multiple_of(gi * group + u * NORM_CHUNK, NORM_CHUNK)
            rs = pl.ds(start, NORM_CHUNK)
            fn(rs, src_ref[rs, :], start)
        return carry

    lax.fori_loop(0, n_rows // group, outer, 0)


def _normmod_first_kernel(x_ref, c_ref, nw_ref, mod_ref, o_ref):
    i = pl.program_id(0)
    kind = (i * NORM_TR >= S).astype(jnp.int32)
    n_rows = jnp.minimum(M - i * NORM_TR, NORM_TR)
    shift = mod_ref[0, 0, pl.ds(kind, 1), :]
    gain = nw_ref[...] * (1.0 + mod_ref[0, 1, pl.ds(kind, 1), :])

    def emit(rs, x, start):
        rinv = lax.rsqrt(jnp.mean(x * x, axis=-1, keepdims=True) + EPS)
        o_ref[rs, :] = ((x * rinv) * gain + shift).astype(o_ref.dtype)

    @pl.when(kind == 0)
    def _():
        _norm_rows(x_ref, n_rows, emit)

    @pl.when(kind == 1)
    def _():
        _norm_rows(c_ref, n_rows, emit)


def _normmod_first(x, ctx, nw, mods):
    assert S % NORM_TR == 0 and LC % (NORM_CHUNK * NORM_UNROLL) == 0 and LC <= NORM_TR
    return pl.pallas_call(
        _normmod_first_kernel,
        grid=(pl.cdiv(M, NORM_TR),),
        in_specs=[pl.BlockSpec((NORM_TR, D), lambda i: (jnp.minimum(i, S // NORM_TR - 1), 0)),
                  pl.BlockSpec((LC, D), lambda i: (0, 0)),
                  pl.BlockSpec((1, D), lambda i: (0, 0)),
                  pl.BlockSpec((1, 3, 8, D), lambda i: (0, 0, 0, 0))],
        out_specs=pl.BlockSpec((NORM_TR, D), lambda i: (i, 0)),
        out_shape=jax.ShapeDtypeStruct((M, D), BF16),
        compiler_params=_cparams(("arbitrary",)),
        name="normmod_first",
    )(x, ctx, nw.reshape(1, D), mods)


def _cast_weights(w_refs, wb_refs):
    @pl.when(pl.program_id(1) == 0)
    def _():
        for w_ref, wb_ref in zip(w_refs, wb_refs):
            wb_ref[...] = w_ref[...].astype(BF16)


def _mm_swiglu_kernel(a_ref, wg_ref, wu_ref, wo_ref, o_ref, wob_ref, wgb_ref, wub_ref, *, tm, mm):
    _cast_weights((wg_ref, wu_ref), (wgb_ref, wub_ref))
    wob_ref[...] = wo_ref[...].astype(BF16)

    def body(r, carry):
        a = a_ref[_sub(r, mm), :]
        g = jnp.dot(a, wgb_ref[...], preferred_element_type=F32)
        u = jnp.dot(a, wub_ref[...], preferred_element_type=F32)
        o_ref[_sub(r, mm), :] = (g * jax.nn.sigmoid(g) * u).astype(o_ref.dtype)
        return carry

    lax.fori_loop(0, tm // mm, body, 0, unroll=True)


def _mm_plain_kernel(a_ref, w_ref, o_ref):
    o_ref[...] = jnp.dot(a_ref[...], w_ref[...], preferred_element_type=F32).astype(o_ref.dtype)


def _mm_swiglu(a, wi, wo, layer, *, tm, mm, tn):
    rows, k = a.shape
    assert rows % tm == 0 and tm % mm == 0 and mm % 16 == 0
    n_out = wi.shape[2] // 2
    nj, ni = n_out // tn, rows // tm
    assert wo.shape[1] % (nj * ni) == 0
    slab = wo.shape[1] // (nj * ni)
    assert slab % 16 == 0
    return pl.pallas_call(
        functools.partial(_mm_swiglu_kernel, tm=tm, mm=mm),
        grid=(nj, ni),
        in_specs=[pl.BlockSpec((tm, k), lambda j, i: (i, 0)),
                  pl.BlockSpec((None, k, tn), lambda j, i: (layer, 0, j)),
                  pl.BlockSpec((None, k, tn), lambda j, i: (layer, 0, j + nj)),
                  pl.BlockSpec((None, slab, D), lambda j, i: (layer, j * ni + i, 0))],
        out_specs=[pl.BlockSpec((tm, tn), lambda j, i: (i, j)),
                   pl.BlockSpec((slab, D), lambda j, i: (j * ni + i, 0))],
        out_shape=[jax.ShapeDtypeStruct((rows, n_out), BF16),
                   jax.ShapeDtypeStruct((wo.shape[1], D), BF16)],
        scratch_shapes=[pltpu.VMEM((k, tn), BF16)] * 2,
        compiler_params=_cparams(("arbitrary", "arbitrary")),
        name="mm_swiglu",
    )(a, wi, wi, wo)


def _mm_plain(a, wb, col0, ncols, *, tm, tn, out_dtype):
    rows, k = a.shape
    assert rows % tm == 0 and ncols % tn == 0 and col0 % tn == 0
    return pl.pallas_call(
        _mm_plain_kernel,
        grid=(ncols // tn, rows // tm),
        in_specs=[pl.BlockSpec((tm, k), lambda j, i: (i, 0)),
                  pl.BlockSpec((k, tn), lambda j, i: (0, col0 // tn + j))],
        out_specs=pl.BlockSpec((tm, tn), lambda j, i: (i, j)),
        out_shape=jax.ShapeDtypeStruct((rows, ncols), out_dtype),
        compiler_params=_cparams(("arbitrary", "arbitrary")),
        name="mm_plain",
    )(a, wb)


def _gate_rows(gate_ref, i, tm):
    rows = i * tm + lax.broadcasted_iota(jnp.int32, (tm, 1), 0)
    return jnp.where(rows >= S, gate_ref[0, 0, 1:2, :], gate_ref[0, 0, 0:1, :])


DOWN_TM = 256


def _ffn_down_kernel(*refs, n_jobs, final, split):
    it = iter(refs)
    h_ref = next(it)
    c_ref = next(it) if split else None
    gate_ref, a_ref, w_ref, nw_ref = next(it), next(it), next(it), next(it)
    mod_ref = None if final else next(it)
    job_in = [next(it) for _ in range(n_jobs)]
    o_ref = next(it)
    hn_ref = None if final else next(it)
    job_out = [next(it) for _ in range(n_jobs)]
    new_ref = next(it) if final else o_ref
    for src_ref, dst_ref in zip(job_in, job_out):
        dst_ref[...] = src_ref[...].astype(BF16)
    kind = (pl.program_id(0) * DOWN_TM >= S).astype(jnp.int32)
    acc = jnp.dot(a_ref[...], w_ref[...], preferred_element_type=F32)
    res = jnp.where(kind == 1, c_ref[...], h_ref[...]) if split else h_ref[...]
    new_ref[...] = res + (0.5 * gate_ref[0, 0, pl.ds(kind, 1), :]) * acc
    nw = nw_ref[...]
    if final:
        def emit(rs, x, start):
            rinv = lax.rsqrt(jnp.mean(x * x, axis=-1, keepdims=True) + EPS)
            o_ref[rs, :] = x * rinv * nw
    else:
        shift = mod_ref[0, 0, pl.ds(kind, 1), :]
        gain = nw * (1.0 + mod_ref[0, 1, pl.ds(kind, 1), :])

        def emit(rs, x, start):
            rinv = lax.rsqrt(jnp.mean(x * x, axis=-1, keepdims=True) + EPS)
            hn_ref[rs, :] = ((x * rinv) * gain + shift).astype(hn_ref.dtype)

    _norm_rows(new_ref, DOWN_TM, emit)


def _ffn_down(h, rows, mods, layer, gate_idx, a, wb, nw_next, next_mod=None, cast_jobs=()):
    k = a.shape[1]
    final = next_mod is None
    split = isinstance(h, tuple)
    tm = DOWN_TM
    assert rows % tm == 0 and S % tm == 0 and tm % (NORM_CHUNK * NORM_UNROLL) == 0
    assert not split or (rows == M and LC == tm)
    n_slabs = S // tm
    row = lambda i: (i, 0)
    job_in, job_out, job_shape = [], [], []
    for w, idx in cast_jobs:
        kk, nn = w.shape[1:]
        assert kk % n_slabs == 0 and (kk // n_slabs) % 16 == 0
        slab = kk // n_slabs
        job_in.append(pl.BlockSpec((None, slab, nn),
                                   lambda i, idx=idx: (idx, jnp.minimum(i, n_slabs - 1), 0)))
        job_out.append(pl.BlockSpec((slab, nn), lambda i: (jnp.minimum(i, n_slabs - 1), 0)))
        job_shape.append(jax.ShapeDtypeStruct((kk, nn), BF16))
    if split:
        h_specs = [pl.BlockSpec((tm, D), lambda i: (jnp.minimum(i, n_slabs - 1), 0)),
                   pl.BlockSpec((tm, D), lambda i: (0, 0))]
        h_args = list(h)
    else:
        h_specs, h_args = [pl.BlockSpec((tm, D), row)], [h]
    in_specs = h_specs + [pl.BlockSpec((1, 1, 8, D), lambda i: (layer, gate_idx, 0, 0)),
                          pl.BlockSpec((tm, k), row),
                          pl.BlockSpec((k, D), lambda i: (0, 0), pipeline_mode=pl.Buffered(1)),
                          pl.BlockSpec((1, D), lambda i: (0, 0))]
    args = h_args + [mods, a, wb, nw_next.reshape(1, D)]
    out_specs = [pl.BlockSpec((tm, D), row)]
    out_shape = [jax.ShapeDtypeStruct((rows, D), F32)]
    if not final:
        in_specs.append(pl.BlockSpec((1, 3, 8, D), lambda i: (next_mod[0], next_mod[1], 0, 0)))
        args.append(mods)
        out_specs.append(pl.BlockSpec((tm, D), row))
        out_shape.append(jax.ShapeDtypeStruct((rows, D), BF16))
    n_main = len(out_shape)
    res = pl.pallas_call(
        functools.partial(_ffn_down_kernel, n_jobs=len(cast_jobs), final=final, split=split),
        grid=(rows // tm,),
        in_specs=in_specs + job_in,
        out_specs=out_specs + job_out,
        out_shape=out_shape + job_shape,
        scratch_shapes=[pltpu.VMEM((tm, D), F32)] if final else [],
        compiler_params=_cparams(("arbitrary",)),
        name="ffn_down_final" if final else "ffn_down",
    )(*args, *[w for w, _ in cast_jobs])
    return tuple(res[:n_main]), list(res[n_main:])


def _out_proj_kernel(h_ref, gate_ref, a1_ref, a2_ref, w1_ref, w2_ref, nw_ref, mod_ref,
                     o_ref, hn_ref, *, tm, unroll):
    i = pl.program_id(0)
    acc = (jnp.dot(a1_ref[...], w1_ref[...], preferred_element_type=F32)
           + jnp.dot(a2_ref[...], w2_ref[...], preferred_element_type=F32))
    o_ref[...] = h_ref[...] + _gate_rows(gate_ref, i, tm) * acc
    nw = nw_ref[...]
    gain_lat = nw * (1.0 + mod_ref[0, 1, 0:1, :])
    gain_ctx = nw * (1.0 + mod_ref[0, 1, 1:2, :])
    shift_lat = mod_ref[0, 0, 0:1, :]
    shift_ctx = mod_ref[0, 0, 1:2, :]

    def emit(rs, x, start):
        isc = (i * tm + start + lax.broadcasted_iota(jnp.int32, (NORM_CHUNK, 1), 0)) >= S
        rinv = lax.rsqrt(jnp.mean(x * x, axis=-1, keepdims=True) + EPS)
        gain = jnp.where(isc, gain_ctx, gain_lat)
        shift = jnp.where(isc, shift_ctx, shift_lat)
        hn_ref[rs, :] = ((x * rinv) * gain + shift).astype(hn_ref.dtype)

    _norm_rows(o_ref, tm, emit, unroll=unroll)


def _out_proj(h, rows, mods, layer, a1, a2, wb, nw_next, *, tm):
    k = a1.shape[1]
    assert a2.shape[1] == k and wb.shape == (2 * k, D) and rows % tm == 0
    unroll = 4 if tm % (4 * NORM_CHUNK) == 0 else 3
    row = lambda i: (i, 0)
    return pl.pallas_call(
        functools.partial(_out_proj_kernel, tm=tm, unroll=unroll),
        grid=(rows // tm,),
        in_specs=[pl.BlockSpec((tm, D), row),
                  pl.BlockSpec((1, 1, 8, D), lambda i: (layer, 5, 0, 0)),
                  pl.BlockSpec((tm, k), row),
                  pl.BlockSpec((tm, k), row),
                  pl.BlockSpec((k, D), lambda i: (0, 0)),
                  pl.BlockSpec((k, D), lambda i: (1, 0)),
                  pl.BlockSpec((1, D), lambda i: (0, 0)),
                  pl.BlockSpec((1, 3, 8, D), lambda i: (layer, 2, 0, 0))],
        out_specs=[pl.BlockSpec((tm, D), row), pl.BlockSpec((tm, D), row)],
        out_shape=[jax.ShapeDtypeStruct((rows, D), F32), jax.ShapeDtypeStruct((rows, D), BF16)],
        compiler_params=_cparams(("arbitrary",)),
        name="out_proj",
    )(h, mods, a1, a2, wb, wb, nw_next.reshape(1, D), mods)


def _rope_tables():
    t = np.arange(S)
    rows = (t // GW).astype(np.float32)
    cols = (t % GW).astype(np.float32)
    half = HD // 2
    inv = np.float32(ROPE_BASE) ** (-np.arange(0, half, 2, dtype=np.float32) / np.float32(half))
    ar = rows[:, None] * inv
    ac = cols[:, None] * inv
    cos = np.concatenate([np.cos(ar), np.cos(ar), np.cos(ac), np.cos(ac)], axis=-1)
    sin = np.concatenate([-np.sin(ar), np.sin(ar), -np.sin(ac), np.sin(ac)], axis=-1)
    lat = np.concatenate([cos, sin], axis=-1)
    ctx = np.concatenate([np.ones((LC, HD), np.float32), np.zeros((LC, HD), np.float32)], axis=-1)
    return jnp.asarray(np.concatenate([lat, ctx], axis=0).astype(np.float32))


def _rope(x, tab):
    c = tab[:, :HD]
    sg = tab[:, HD:]
    lane = lax.broadcasted_iota(jnp.int32, x.shape, 1)
    first = (lane % (HD // 2)) < (HD // 4)
    partner = jnp.where(first, pltpu.roll(x, HD - HD // 4, 1), pltpu.roll(x, HD // 4, 1))
    return x * c + partner * sg


def _mm_rope_kernel(a_ref, w_ref, tab_ref, o_ref, *, n_rot):
    acc = jnp.dot(a_ref[...], w_ref[...], preferred_element_type=F32)
    tab = tab_ref[...]
    for hh in range(acc.shape[1] // HD):
        x = acc[:, hh * HD:(hh + 1) * HD]
        o_ref[:, hh * HD:(hh + 1) * HD] = (_rope(x, tab) if hh < n_rot else x).astype(o_ref.dtype)


def _mm_rope(a, wb, ncols, tabs, n_rot, *, tm):
    rows, k = a.shape
    assert rows % tm == 0 and ncols % HD == 0
    return pl.pallas_call(
        functools.partial(_mm_rope_kernel, n_rot=n_rot),
        grid=(rows // tm,),
        in_specs=[pl.BlockSpec((tm, k), lambda i: (i, 0)),
                  pl.BlockSpec((k, ncols), lambda i: (0, 0)),
                  pl.BlockSpec((tm, 2 * HD), lambda i: (i, 0))],
        out_specs=pl.BlockSpec((tm, ncols), lambda i: (i, 0)),
        out_shape=jax.ShapeDtypeStruct((rows, ncols), BF16),
        compiler_params=_cparams(("arbitrary",)),
        name="mm_rope",
    )(a, wb, tabs)


def _attn_kernel(sink_ref, q_ref, kp_ref, kc_ref, kn_ref, vp_ref, vc_ref, vn_ref, kx_ref, vx_ref, o_ref):
    g = pl.program_id(0)
    n = pl.program_id(1)
    nlat = S // AQB
    scale2 = (HD ** -0.5) * LOG2E
    k_all = jnp.concatenate([kp_ref[...], kc_ref[...], kn_ref[...], kx_ref[...]], axis=0)
    v_all = jnp.concatenate([vp_ref[...], vc_ref[...], vn_ref[...], vx_ref[...]], axis=0)
    nloc = AQB + 2 * AWIN
    nk = nloc + LC
    r = lax.broadcasted_iota(jnp.int32, (AQB, nk), 0)
    c = lax.broadcasted_iota(jnp.int32, (AQB, nk), 1)
    rel = c - AWIN - r
    pos = n * AQB - AWIN + c
    hi = jnp.where(n < nlat, S, 0)
    ok = ((jnp.abs(rel) <= AWIN) & (pos >= 0) & (pos < hi)) | (c >= nloc)
    for hh in range(AH // AKV):
        q = q_ref[:, hh * HD:(hh + 1) * HD]
        s = lax.dot_general(q, k_all, (((1,), (1,)), ((), ())), preferred_element_type=F32) * scale2
        s = jnp.where(ok, s, NEG)
        sk = sink_ref[g * (AH // AKV) + hh] * LOG2E
        m = jnp.maximum(jnp.max(s, axis=-1, keepdims=True), sk)
        e = jnp.exp2(s - m)
        den = jnp.sum(e, axis=-1, keepdims=True) + jnp.exp2(sk - m)
        o = jnp.dot(e.astype(BF16), v_all, preferred_element_type=F32)
        o_ref[:, hh * HD:(hh + 1) * HD] = (o / den).astype(o_ref.dtype)


def _attn(p, sink):
    assert AQB == LC and AQB == 2 * AWIN
    nlat = S // AQB
    qw = (AH // AKV) * HD
    kcol = AH
    vcol = kcol + AKV
    last_w = S // AWIN - 1

    def prev(n):
        return jnp.clip(2 * n - 1, 0, last_w)

    def cur(n):
        return jnp.minimum(n, nlat - 1)

    def nxt(n):
        return jnp.clip(2 * n + 2, 0, last_w)

    return pl.pallas_call(
        _attn_kernel,
        grid=(AKV, M // AQB),
        in_specs=[
            pl.BlockSpec(memory_space=pltpu.SMEM),
            pl.BlockSpec((AQB, qw), lambda g, n: (n, g)),
            pl.BlockSpec((AWIN, HD), lambda g, n: (prev(n), kcol + g)),
            pl.BlockSpec((AQB, HD), lambda g, n: (cur(n), kcol + g)),
            pl.BlockSpec((AWIN, HD), lambda g, n: (nxt(n), kcol + g)),
            pl.BlockSpec((AWIN, HD), lambda g, n: (prev(n), vcol + g)),
            pl.BlockSpec((AQB, HD), lambda g, n: (cur(n), vcol + g)),
            pl.BlockSpec((AWIN, HD), lambda g, n: (nxt(n), vcol + g)),
            pl.BlockSpec((LC, HD), lambda g, n: (S // LC, kcol + g)),
            pl.BlockSpec((LC, HD), lambda g, n: (S // LC, vcol + g)),
        ],
        out_specs=pl.BlockSpec((AQB, qw), lambda g, n: (n, g)),
        out_shape=jax.ShapeDtypeStruct((M, AH * HD), BF16),
        compiler_params=_cparams(("arbitrary", "arbitrary")),
        name="attn",
    )(sink, p, p, p, p, p, p, p, p, p)


def _conv_kernel(b_ref, c_ref, u_ref, cp_ref, up_ref, cn_ref, un_ref, w_ref, bias_ref, o_ref, *, tr):
    i = pl.program_id(0)
    lat_blocks = S // tr
    z = c_ref[...] * u_ref[...]
    zp = cp_ref[7:8, :] * up_ref[7:8, :]
    zn = cn_ref[0:1, :] * un_ref[0:1, :]
    has_prev = jnp.logical_and(i != 0, i != lat_blocks)
    has_next = jnp.logical_and(i != lat_blocks - 1, i != M // tr - 1)
    zp = jnp.where(has_prev, zp, 0.0)
    zn = jnp.where(has_next, zn, 0.0)
    row = lax.broadcasted_iota(jnp.int32, z.shape, 0)
    z_m1 = jnp.where(row == 0, zp, pltpu.roll(z, 1, 0))
    z_p1 = jnp.where(row == tr - 1, zn, pltpu.roll(z, tr - 1, 0))
    conv = z_m1 * w_ref[0:1, :] + z * w_ref[1:2, :] + z_p1 * w_ref[2:3, :] + bias_ref[...]
    o_ref[...] = (b_ref[...] * conv).astype(o_ref.dtype)


def _conv(p, conv_w, conv_b):
    tr = 256
    assert LC % tr == 0 and S % tr == 0
    h8 = tr // 8

    def prev8(i):
        return jnp.maximum(i * h8 - 1, 0)

    def next8(i):
        return jnp.minimum((i + 1) * h8, M // 8 - 1)

    return pl.pallas_call(
        functools.partial(_conv_kernel, tr=tr),
        grid=(M // tr,),
        in_specs=[
            pl.BlockSpec((tr, CCH), lambda i: (i, 0)),
            pl.BlockSpec((tr, CCH), lambda i: (i, 1)),
            pl.BlockSpec((tr, CCH), lambda i: (i, 2)),
            pl.BlockSpec((8, CCH), lambda i: (prev8(i), 1)),
            pl.BlockSpec((8, CCH), lambda i: (prev8(i), 2)),
            pl.BlockSpec((8, CCH), lambda i: (next8(i), 1)),
            pl.BlockSpec((8, CCH), lambda i: (next8(i), 2)),
            pl.BlockSpec((3, CCH), lambda i: (0, 0)),
            pl.BlockSpec((1, CCH), lambda i: (0, 0)),
        ],
        out_specs=pl.BlockSpec((tr, CCH), lambda i: (i, 0)),
        out_shape=jax.ShapeDtypeStruct((M, CCH), BF16),
        compiler_params=_cparams(("arbitrary",)),
        name="conv",
    )(p, p, p, p, p, p, p, conv_w, conv_b.reshape(1, CCH))


def _log_sigmoid(d):
    return jnp.minimum(d, 0.0) - jnp.log(1.0 + jnp.exp(-jnp.abs(d)))


def _ret_chunk(decay_ref, q_ref, k_ref, v_ref, tab_ref, st_ref, h, backward):
    lg = _log_sigmoid(jnp.full((1, 1), decay_ref[h], F32))
    tab = tab_ref[...]
    q = _rope(q_ref[:, h * RDK:(h + 1) * RDK], tab).astype(BF16)
    k = (_rope(k_ref[:, h * RDK:(h + 1) * RDK], tab) * (RDK ** -0.5)).astype(BF16)
    v = v_ref[:, h * RDV:(h + 1) * RDV]
    ii = lax.broadcasted_iota(jnp.int32, (RC, RC), 0)
    jj = lax.broadcasted_iota(jnp.int32, (RC, RC), 1)
    ic = lax.broadcasted_iota(jnp.int32, (RC, 1), 0).astype(F32)
    if backward:
        rel = jj - ii
        keep = rel > 0
        xi = jnp.exp(lg * (RC - ic))
        zeta = jnp.exp(lg * ic)
    else:
        rel = ii - jj
        keep = rel >= 0
        xi = jnp.exp(lg * (ic + 1.0))
        zeta = jnp.exp(lg * (RC - 1.0 - ic))
    dmask = jnp.where(keep, jnp.exp(lg * jnp.where(keep, rel, 0).astype(F32)), 0.0)
    inner = lax.dot_general(q, k, (((1,), (1,)), ((), ())), preferred_element_type=F32) * dmask
    y = jnp.dot(inner.astype(BF16), v.astype(BF16), preferred_element_type=F32)
    st = st_ref[h]
    y = y + jnp.dot(q, st.astype(BF16), preferred_element_type=F32) * xi
    kv = lax.dot_general(k, (v * zeta).astype(BF16), (((0,), (0,)), ((), ())),
                         preferred_element_type=F32)
    st_ref[h] = jnp.exp(lg * float(RC)) * st + kv
    return y


def _ret_fwd_kernel(decay_ref, q_ref, k_ref, v_ref, tab_ref, y_ref, st_ref):
    @pl.when(pl.program_id(0) == 0)
    def _():
        st_ref[...] = jnp.zeros_like(st_ref)

    for h in range(RH):
        y_ref[:, h * RDV:(h + 1) * RDV] = _ret_chunk(decay_ref, q_ref, k_ref, v_ref, tab_ref,
                                                     st_ref, h, False)


def _ret_bwd_kernel(decay_ref, q_ref, k_ref, v_ref, tab_ref, yf_ref, g_ref, gnw_ref, o_ref, st_ref):
    @pl.when(pl.program_id(0) == 0)
    def _():
        st_ref[...] = jnp.zeros_like(st_ref)

    for h in range(RH):
        sl = slice(h * RDV, (h + 1) * RDV)
        y = yf_ref[:, sl] + _ret_chunk(decay_ref, q_ref, k_ref, v_ref, tab_ref, st_ref, h, True)
        mu = jnp.mean(y, axis=-1, keepdims=True)
        yc = y - mu
        var = jnp.mean(yc * yc, axis=-1, keepdims=True)
        yn = yc * lax.rsqrt(var + EPS) * gnw_ref[:, sl]
        gt = g_ref[:, sl]
        o_ref[:, sl] = (gt * jax.nn.sigmoid(gt) * yn).astype(o_ref.dtype)


def _retention(p, decay_f, decay_b, gn_w, tabs):
    nch = M // RC
    nlat = S // RC
    qw, vw = RH * RDK, RH * RDV
    fwd = lambda s: ((s + nlat) % nch)
    bwd = lambda s: (nch - 1 - s)
    smem = pl.BlockSpec(memory_space=pltpu.SMEM)

    def specs(cm):
        return [smem,
                pl.BlockSpec((RC, qw), lambda s: (cm(s), 0)),
                pl.BlockSpec((RC, qw), lambda s: (cm(s), 1)),
                pl.BlockSpec((RC, vw), lambda s: (cm(s), 1)),
                pl.BlockSpec((RC, 2 * HD), lambda s: (cm(s), 0))]

    state = pltpu.VMEM((RH, RDK, RDV), F32)
    yf = pl.pallas_call(
        _ret_fwd_kernel,
        grid=(nch,),
        in_specs=specs(fwd),
        out_specs=pl.BlockSpec((RC, vw), lambda s: (fwd(s), 0)),
        out_shape=jax.ShapeDtypeStruct((M, vw), F32),
        scratch_shapes=[state],
        compiler_params=_cparams(("arbitrary",)),
        name="ret_fwd",
    )(decay_f, p, p, p, tabs)
    return pl.pallas_call(
        _ret_bwd_kernel,
        grid=(nch,),
        in_specs=specs(bwd) + [pl.BlockSpec((RC, vw), lambda s: (bwd(s), 0)),
                               pl.BlockSpec((RC, vw), lambda s: (bwd(s), 2)),
                               pl.BlockSpec((1, vw), lambda s: (0, 0))],
        out_specs=pl.BlockSpec((RC, vw), lambda s: (bwd(s), 0)),
        out_shape=jax.ShapeDtypeStruct((M, vw), BF16),
        scratch_shapes=[state],
        compiler_params=_cparams(("arbitrary",)),
        name="ret_bwd",
    )(decay_b, p, p, p, tabs, yf, p, gn_w.reshape(1, vw))


_NA_NBLK = S // NQB
_NA_CASES = (0, 1, _NA_NBLK - 1)


def _na_window_row(b):
    return jnp.clip(NQROWS * b - NROWS // 2, 0, GROWS - NWIN_ROWS)


def _na_table_kernel(rpb_ref, o_ref):
    h = pl.program_id(0)
    n_dr, n_dc = 2 * NROWS - 1, 2 * NCOLS - 1
    cq = lax.broadcasted_iota(jnp.int32, (GW, 2 * GW), 0)
    lane = lax.broadcasted_iota(jnp.int32, (GW, 2 * GW), 1)
    right = lane >= GW
    kc = jnp.where(right, lane - GW, lane)
    dc = kc - cq + (NCOLS - 1)
    cs = jnp.clip(cq - NCOLS // 2, 0, GW - NCOLS)
    col_ok = (kc >= cs) & (kc < cs + NCOLS)
    memo = {}

    def row_scalar(a, b):
        return rpb_ref[(h * n_dr + a) * n_dc + b] if 0 <= a < n_dr else 0.0

    def pair(a0):
        if a0 not in memo:
            acc = jnp.zeros((GW, 2 * GW), F32)
            for b in range(n_dc):
                vec = jnp.where(right, row_scalar(a0 + 1, b), row_scalar(a0, b))
                acc = jnp.where(dc == b, vec, acc)
            memo[a0] = acc
        return memo[a0]

    for ci, blk in enumerate(_NA_CASES):
        w = int(np.clip(NQROWS * blk - NROWS // 2, 0, GROWS - NWIN_ROWS))
        for rl in range(NQROWS):
            r = NQROWS * blk + rl
            rs = int(np.clip(r - NROWS // 2, 0, GROWS - NROWS))
            for pr in range(NWIN_ROWS // 2):
                kr = w + 2 * pr
                ok_l = rs <= kr < rs + NROWS
                ok_r = rs <= kr + 1 < rs + NROWS
                if ok_l and ok_r:
                    ok = col_ok
                elif ok_l:
                    ok = col_ok & jnp.logical_not(right)
                elif ok_r:
                    ok = col_ok & right
                else:
                    ok = None
                if ok is None:
                    tile = jnp.full((GW, 2 * GW), NEG, F32)
                else:
                    tile = jnp.where(ok, pair(kr - r + NROWS - 1) * LOG2E, NEG)
                o_ref[ci, 0, rl * GW:(rl + 1) * GW, pr * 2 * GW:(pr + 1) * 2 * GW] = tile


def _na_tables(rpb):
    return pl.pallas_call(
        _na_table_kernel,
        grid=(NH,),
        in_specs=[pl.BlockSpec(memory_space=pltpu.SMEM)],
        out_specs=pl.BlockSpec((len(_NA_CASES), 1, NQB, NWIN), lambda h: (0, h, 0, 0)),
        out_shape=jax.ShapeDtypeStruct((len(_NA_CASES), NH, NQB, NWIN), F32),
        compiler_params=_cparams(("arbitrary",)),
        name="na_tables",
    )(rpb.reshape(-1))


def _na_kernel(q_ref, k_ref, v_ref, t_ref, o_ref):
    b = pl.program_id(1)
    scale2 = (HD ** -0.5) * LOG2E
    start = pl.multiple_of(_na_window_row(b) * GW, NQB)
    nt = (((1,), (1,)), ((), ()))
    for hh in range(NHPS):
        cs = slice(hh * HD, (hh + 1) * HD)
        q = q_ref[:, cs]
        s_loc = lax.dot_general(q, k_ref[pl.ds(start, NWIN), cs], nt,
                                preferred_element_type=F32) * scale2 + t_ref[0, hh]
        s_ctx = lax.dot_general(q, k_ref[S:M, cs], nt, preferred_element_type=F32) * scale2
        m = jnp.maximum(jnp.max(s_loc, axis=-1, keepdims=True), jnp.max(s_ctx, axis=-1, keepdims=True))
        e_loc = jnp.exp2(s_loc - m)
        e_ctx = jnp.exp2(s_ctx - m)
        den = jnp.sum(e_loc, axis=-1, keepdims=True) + jnp.sum(e_ctx, axis=-1, keepdims=True)
        o = (jnp.dot(e_loc.astype(BF16), v_ref[pl.ds(start, NWIN), cs], preferred_element_type=F32)
             + jnp.dot(e_ctx.astype(BF16), v_ref[S:M, cs], preferred_element_type=F32))
        o_ref[:, cs] = (o / den).astype(o_ref.dtype)


def _na(p, rpb):
    gw = NHPS * HD
    qcol = 0
    kcol = qcol + NH // NHPS
    vcol = kcol + NH // NHPS
    tables = _na_tables(rpb)

    def case(b):
        return jnp.minimum(b, 1) + jnp.maximum(b - (_NA_NBLK - 2), 0)

    return pl.pallas_call(
        _na_kernel,
        grid=(NH // NHPS, _NA_NBLK),
        in_specs=[
            pl.BlockSpec((NQB, gw), lambda h, b: (b, qcol + h)),
            pl.BlockSpec((M, gw), lambda h, b: (0, kcol + h)),
            pl.BlockSpec((M, gw), lambda h, b: (0, vcol + h)),
            pl.BlockSpec((1, NHPS, NQB, NWIN), lambda h, b: (case(b), h, 0, 0)),
        ],
        out_specs=pl.BlockSpec((NQB, gw), lambda h, b: (b, h)),
        out_shape=jax.ShapeDtypeStruct((S, NH * HD), BF16),
        compiler_params=_cparams(("arbitrary", "arbitrary")),
        name="na",
    )(p, p, p, tables)


def _tiles(rows):
    if rows == M:
        return dict(up=(2112, 1056), small=528, inp=1056)
    assert rows == S
    return dict(up=(2048, 1024), small=512, inp=1024)


def _ffn_half(h, hn, rows, mods, layer, third, wi, wo, nw_next, next_mod, cast_jobs=()):
    t = _tiles(rows)
    a, wob = _mm_swiglu(hn, wi, wo, layer, tm=t["up"][0], mm=t["up"][1], tn=512)
    return _ffn_down(h, rows, mods, layer, 3 * third + 2, a, wob, nw_next, next_mod, cast_jobs)


IN_TN = 1536


def kernel(x, c, ctx, c_ctx, ada_w, ada_b, norm_w, ffn_a_wi, ffn_a_wo, ffn_b_wi, ffn_b_wo,
           ev_w_in, ev_w_out, ev_sink, ev_conv_w, ev_conv_b,
           od_w_in, od_w_out, od_decay_f, od_decay_b, od_gn_w, od_rpb, final_norm_w):
    assert x.shape == (1, S, D) and ctx.shape == (1, LC, D) and ada_w.shape[0] == 2
    cvec = jnp.concatenate([c, c_ctx[None, :], jnp.zeros((6, D), F32)], axis=0)
    mods = _ada(cvec, ada_w, ada_b)
    tabs = _rope_tables()

    tm_in = _tiles(M)["inp"]
    hn = _normmod_first(x[0], ctx[0], norm_w[0, 0], mods)
    (h, hn), (w_in, w_out) = _ffn_half((x[0], ctx[0]), hn, M, mods, 0, 0, ffn_a_wi, ffn_a_wo,
                                       norm_w[0, 1], (0, 1), cast_jobs=((ev_w_in, 0), (ev_w_out, 0)))
    n_att = (AH + 2 * AKV) * HD
    p_att = _mm_rope(hn, w_in, n_att, tabs, AH + AKV, tm=tm_in)
    p_conv = _mm_plain(hn, w_in, n_att, EV_IN - n_att, tm=tm_in, tn=IN_TN, out_dtype=F32)
    att = _attn(p_att, ev_sink[0])
    cnv = _conv(p_conv, ev_conv_w[0], ev_conv_b[0])
    h, hn = _out_proj(h, M, mods, 0, att, cnv, w_out, norm_w[0, 2], tm=_tiles(M)["small"])
    (h, hn), _ = _ffn_half(h, hn, M, mods, 0, 2, ffn_b_wi, ffn_b_wo, norm_w[1, 0], (1, 0))

    (h, hn), (w_in, w_out) = _ffn_half(h, hn, M, mods, 1, 0, ffn_a_wi, ffn_a_wo, norm_w[1, 1], (1, 1),
                                       cast_jobs=((od_w_in, 0), (od_w_out, 0)))
    n_ret = 2 * RH * RDK + 2 * RH * RDV
    p_ret = _mm_plain(hn, w_in, 0, n_ret, tm=tm_in, tn=IN_TN, out_dtype=F32)
    p_na = _mm_plain(hn, w_in, n_ret, OD_IN - n_ret, tm=tm_in, tn=IN_TN, out_dtype=BF16)
    ret = _retention(p_ret, od_decay_f[0], od_decay_b[0], od_gn_w[0], tabs)
    nat = _na(p_na, od_rpb[0])
    h, hn = _out_proj(h, S, mods, 1, ret, nat, w_out, norm_w[1, 2], tm=_tiles(S)["small"])
    (out,), _ = _ffn_half(h, hn, S, mods, 1, 2, ffn_b_wi, ffn_b_wo, final_norm_w, None)
    return out[None]
```

```python
import functools

import numpy as np
import jax
import jax.numpy as jnp
from jax import lax
from jax.experimental import pallas as pl
from jax.experimental.pallas import tpu as pltpu

F32 = jnp.float32
BF16 = jnp.bfloat16

D = 2048
S = 8192
LC = 256
M = S + LC
GW = 64
GROWS = S // GW
HD = 128
DFF = 5632
NMOD = 9
EPS = 1e-6
ROPE_BASE = 10000.0
NEG = -1e30

AH, AKV, AWIN = 8, 2, 128
AQB = 256
CCH = 1024
EV_IN = AH * HD + 2 * AKV * HD + 3 * CCH
RH, RDK, RDV = 4, 128, 256
RC = 256
NH, NROWS, NCOLS = 8, 8, 16
NQROWS = 4
NQB = NQROWS * GW
NWIN_ROWS = NQROWS + NROWS
NWIN = NWIN_ROWS * GW
NHPS = 4
LOG2E = float(np.log2(np.e))
OD_IN = 2 * RH * RDK + 2 * RH * RDV + 3 * NH * HD

VMEM_LIMIT = 56 * 1024 * 1024


def _cparams(sem):
    return pltpu.CompilerParams(dimension_semantics=sem, vmem_limit_bytes=VMEM_LIMIT)


def _sub(r, size):
    return pl.ds(pl.multiple_of(r * size, 16), size)


def _ada_kernel(c_ref, w_ref, b_ref, o_ref):
    cv = c_ref[...]
    a = cv * jax.nn.sigmoid(cv)
    acc = jnp.dot(a.astype(BF16), w_ref[0].astype(BF16), preferred_element_type=F32)
    o_ref[0, 0] = acc + b_ref[0]


def _ada(cvec, ada_w, ada_b):
    depth = ada_w.shape[0]
    tn = D
    per = D // tn
    return pl.pallas_call(
        _ada_kernel,
        grid=(depth, NMOD * per),
        in_specs=[
            pl.BlockSpec((8, D), lambda l, j: (0, 0)),
            pl.BlockSpec((1, D, tn), lambda l, j: (l, 0, j)),
            pl.BlockSpec((1, 1, tn), lambda l, j: (l, 0, j)),
        ],
        out_specs=pl.BlockSpec((1, 1, 8, tn), lambda l, j: (l, j // per, 0, j % per)),
        out_shape=jax.ShapeDtypeStruct((depth, NMOD, 8, D), F32),
        compiler_params=_cparams(("arbitrary", "arbitrary")),
        name="ada",
    )(cvec, ada_w, ada_b.reshape(depth, 1, NMOD * D))


NORM_TR = 1024
NORM_CHUNK = 16
NORM_UNROLL = 8


def _norm_rows(src_ref, n_rows, fn, unroll=NORM_UNROLL):
    group = NORM_CHUNK * unroll

    def outer(gi, carry):
        for u in range(unroll):
            start = pl.multiple_of(gi * group + u * NORM_CHUNK, NORM_CHUNK)
            rs = pl.ds(start, NORM_CHUNK)
            fn(rs, src_ref[rs, :], start)
        return carry

    lax.fori_loop(0, n_rows // group, outer, 0)


def _normmod_first_kernel(x_ref, c_ref, nw_ref, mod_ref, o_ref):
    i = pl.program_id(0)
    kind = (i * NORM_TR >= S).astype(jnp.int32)
    n_rows = jnp.minimum(M - i * NORM_TR, NORM_TR)
    shift = mod_ref[0, 0, pl.ds(kind, 1), :]
    gain = nw_ref[...] * (1.0 + mod_ref[0, 1, pl.ds(kind, 1), :])

    def emit(rs, x, start):
        rinv = lax.rsqrt(jnp.mean(x * x, axis=-1, keepdims=True) + EPS)
        o_ref[rs, :] = ((x * rinv) * gain + shift).astype(o_ref.dtype)

    @pl.when(kind == 0)
    def _():
        _norm_rows(x_ref, n_rows, emit)

    @pl.when(kind == 1)
    def _():
        _norm_rows(c_ref, n_rows, emit)


def _normmod_first(x, ctx, nw, mods):
    assert S % NORM_TR == 0 and LC % (NORM_CHUNK * NORM_UNROLL) == 0 and LC <= NORM_TR
    return pl.pallas_call(
        _normmod_first_kernel,
        grid=(pl.cdiv(M, NORM_TR),),
        in_specs=[pl.BlockSpec((NORM_TR, D), lambda i: (jnp.minimum(i, S // NORM_TR - 1), 0)),
                  pl.BlockSpec((LC, D), lambda i: (0, 0)),
                  pl.BlockSpec((1, D), lambda i: (0, 0)),
                  pl.BlockSpec((1, 3, 8, D), lambda i: (0, 0, 0, 0))],
        out_specs=pl.BlockSpec((NORM_TR, D), lambda i: (i, 0)),
        out_shape=jax.ShapeDtypeStruct((M, D), BF16),
        compiler_params=_cparams(("arbitrary",)),
        name="normmod_first",
    )(x, ctx, nw.reshape(1, D), mods)


def _cast_weights(w_refs, wb_refs):
    @pl.when(pl.program_id(1) == 0)
    def _():
        for w_ref, wb_ref in zip(w_refs, wb_refs):
            wb_ref[...] = w_ref[...].astype(BF16)


def _mm_swiglu_kernel(a_ref, wg_ref, wu_ref, wo_ref, o_ref, wob_ref, wgb_ref, wub_ref, *, tm, mm):
    _cast_weights((wg_ref, wu_ref), (wgb_ref, wub_ref))
    wob_ref[...] = wo_ref[...].astype(BF16)

    def body(r, carry):
        a = a_ref[_sub(r, mm), :]
        g = jnp.dot(a, wgb_ref[...], preferred_element_type=F32)
        u = jnp.dot(a, wub_ref[...], preferred_element_type=F32)
        o_ref[_sub(r, mm), :] = (g * jax.nn.sigmoid(g) * u).astype(o_ref.dtype)
        return carry

    lax.fori_loop(0, tm // mm, body, 0, unroll=True)


def _mm_plain_kernel(a_ref, w_ref, o_ref):
    o_ref[...] = jnp.dot(a_ref[...], w_ref[...], preferred_element_type=F32).astype(o_ref.dtype)


def _mm_swiglu(a, wi, wo, layer, *, tm, mm, tn):
    rows, k = a.shape
    assert rows % tm == 0 and tm % mm == 0 and mm % 16 == 0
    n_out = wi.shape[2] // 2
    nj, ni = n_out // tn, rows // tm
    assert wo.shape[1] % (nj * ni) == 0
    slab = wo.shape[1] // (nj * ni)
    assert slab % 16 == 0
    return pl.pallas_call(
        functools.partial(_mm_swiglu_kernel, tm=tm, mm=mm),
        grid=(nj, ni),
        in_specs=[pl.BlockSpec((tm, k), lambda j, i: (i, 0)),
                  pl.BlockSpec((None, k, tn), lambda j, i: (layer, 0, j)),
                  pl.BlockSpec((None, k, tn), lambda j, i: (layer, 0, j + nj)),
                  pl.BlockSpec((None, slab, D), lambda j, i: (layer, j * ni + i, 0))],
        out_specs=[pl.BlockSpec((tm, tn), lambda j, i: (i, j)),
                   pl.BlockSpec((slab, D), lambda j, i: (j * ni + i, 0))],
        out_shape=[jax.ShapeDtypeStruct((rows, n_out), BF16),
                   jax.ShapeDtypeStruct((wo.shape[1], D), BF16)],
        scratch_shapes=[pltpu.VMEM((k, tn), BF16)] * 2,
        compiler_params=_cparams(("arbitrary", "arbitrary")),
        name="mm_swiglu",
    )(a, wi, wi, wo)


def _mm_plain(a, wb, col0, ncols, *, tm, tn, out_dtype):
    rows, k = a.shape
    assert rows % tm == 0 and ncols % tn == 0 and col0 % tn == 0
    return pl.pallas_call(
        _mm_plain_kernel,
        grid=(ncols // tn, rows // tm),
        in_specs=[pl.BlockSpec((tm, k), lambda j, i: (i, 0)),
                  pl.BlockSpec((k, tn), lambda j, i: (0, col0 // tn + j))],
        out_specs=pl.BlockSpec((tm, tn), lambda j, i: (i, j)),
        out_shape=jax.ShapeDtypeStruct((rows, ncols), out_dtype),
        compiler_params=_cparams(("arbitrary", "arbitrary")),
        name="mm_plain",
    )(a, wb)


def _gate_rows(gate_ref, i, tm):
    rows = i * tm + lax.broadcasted_iota(jnp.int32, (tm, 1), 0)
    return jnp.where(rows >= S, gate_ref[0, 0, 1:2, :], gate_ref[0, 0, 0:1, :])


DOWN_TM = 256


def _ffn_down_kernel(*refs, n_jobs, final, split):
    it = iter(refs)
    h_ref = next(it)
    c_ref = next(it) if split else None
    gate_ref, a_ref, w_ref, nw_ref = next(it), next(it), next(it), next(it)
    mod_ref = None if final else next(it)
    job_in = [next(it) for _ in range(n_jobs)]
    o_ref = next(it)
    hn_ref = None if final else next(it)
    job_out = [next(it) for _ in range(n_jobs)]
    new_ref = next(it) if final else o_ref
    for src_ref, dst_ref in zip(job_in, job_out):
        dst_ref[...] = src_ref[...].astype(BF16)
    kind = (pl.program_id(0) * DOWN_TM >= S).astype(jnp.int32)
    acc = jnp.dot(a_ref[...], w_ref[...], preferred_element_type=F32)
    res = jnp.where(kind == 1, c_ref[...], h_ref[...]) if split else h_ref[...]
    new_ref[...] = res + (0.5 * gate_ref[0, 0, pl.ds(kind, 1), :]) * acc
    nw = nw_ref[...]
    if final:
        def emit(rs, x, start):
            rinv = lax.rsqrt(jnp.mean(x * x, axis=-1, keepdims=True) + EPS)
            o_ref[rs, :] = x * rinv * nw
    else:
        shift = mod_ref[0, 0, pl.ds(kind, 1), :]
        gain = nw * (1.0 + mod_ref[0, 1, pl.ds(kind, 1), :])

        def emit(rs, x, start):
            rinv = lax.rsqrt(jnp.mean(x * x, axis=-1, keepdims=True) + EPS)
            hn_ref[rs, :] = ((x * rinv) * gain + shift).astype(hn_ref.dtype)

    _norm_rows(new_ref, DOWN_TM, emit)


def _ffn_down(h, rows, mods, layer, gate_idx, a, wb, nw_next, next_mod=None, cast_jobs=()):
    k = a.shape[1]
    final = next_mod is None
    split = isinstance(h, tuple)
    tm = DOWN_TM
    assert rows % tm == 0 and S % tm == 0 and tm % (NORM_CHUNK * NORM_UNROLL) == 0
    assert not split or (rows == M and LC == tm)
    n_slabs = S // tm
    row = lambda i: (i, 0)
    job_in, job_out, job_shape = [], [], []
    for w, idx in cast_jobs:
        kk, nn = w.shape[1:]
        assert kk % n_slabs == 0 and (kk // n_slabs) % 16 == 0
        slab = kk // n_slabs
        job_in.append(pl.BlockSpec((None, slab, nn),
                                   lambda i, idx=idx: (idx, jnp.minimum(i, n_slabs - 1), 0)))
        job_out.append(pl.BlockSpec((slab, nn), lambda i: (jnp.minimum(i, n_slabs - 1), 0)))
        job_shape.append(jax.ShapeDtypeStruct((kk, nn), BF16))
    if split:
        h_specs = [pl.BlockSpec((tm, D), lambda i: (jnp.minimum(i, n_slabs - 1), 0)),
                   pl.BlockSpec((tm, D), lambda i: (0, 0))]
        h_args = list(h)
    else:
        h_specs, h_args = [pl.BlockSpec((tm, D), row)], [h]
    in_specs = h_specs + [pl.BlockSpec((1, 1, 8, D), lambda i: (layer, gate_idx, 0, 0)),
                          pl.BlockSpec((tm, k), row),
                          pl.BlockSpec((k, D), lambda i: (0, 0), pipeline_mode=pl.Buffered(1)),
                          pl.BlockSpec((1, D), lambda i: (0, 0))]
    args = h_args + [mods, a, wb, nw_next.reshape(1, D)]
    out_specs = [pl.BlockSpec((tm, D), row)]
    out_shape = [jax.ShapeDtypeStruct((rows, D), F32)]
    if not final:
        in_specs.append(pl.BlockSpec((1, 3, 8, D), lambda i: (next_mod[0], next_mod[1], 0, 0)))
        args.append(mods)
        out_specs.append(pl.BlockSpec((tm, D), row))
        out_shape.append(jax.ShapeDtypeStruct((rows, D), BF16))
    n_main = len(out_shape)
    res = pl.pallas_call(
        functools.partial(_ffn_down_kernel, n_jobs=len(cast_jobs), final=final, split=split),
        grid=(rows // tm,),
        in_specs=in_specs + job_in,
        out_specs=out_specs + job_out,
        out_shape=out_shape + job_shape,
        scratch_shapes=[pltpu.VMEM((tm, D), F32)] if final else [],
        compiler_params=_cparams(("arbitrary",)),
        name="ffn_down_final" if final else "ffn_down",
    )(*args, *[w for w, _ in cast_jobs])
    return tuple(res[:n_main]), list(res[n_main:])


def _out_proj_kernel(h_ref, gate_ref, a1_ref, a2_ref, w1_ref, w2_ref, nw_ref, mod_ref,
                     o_ref, hn_ref, *, tm, unroll):
    i = pl.program_id(0)
    acc = (jnp.dot(a1_ref[...], w1_ref[...], preferred_element_type=F32)
           + jnp.dot(a2_ref[...], w2_ref[...], preferred_element_type=F32))
    o_ref[...] = h_ref[...] + _gate_rows(gate_ref, i, tm) * acc
    nw = nw_ref[...]
    gain_lat = nw * (1.0 + mod_ref[0, 1, 0:1, :])
    gain_ctx = nw * (1.0 + mod_ref[0, 1, 1:2, :])
    shift_lat = mod_ref[0, 0, 0:1, :]
    shift_ctx = mod_ref[0, 0, 1:2, :]

    def emit(rs, x, start):
        isc = (i * tm + start + lax.broadcasted_iota(jnp.int32, (NORM_CHUNK, 1), 0)) >= S
        rinv = lax.rsqrt(jnp.mean(x * x, axis=-1, keepdims=True) + EPS)
        gain = jnp.where(isc, gain_ctx, gain_lat)
        shift = jnp.where(isc, shift_ctx, shift_lat)
        hn_ref[rs, :] = ((x * rinv) * gain + shift).astype(hn_ref.dtype)

    _norm_rows(o_ref, tm, emit, unroll=unroll)


def _out_proj(h, rows, mods, layer, a1, a2, wb, nw_next, *, tm):
    k = a1.shape[1]
    assert a2.shape[1] == k and wb.shape == (2 * k, D) and rows % tm == 0
    unroll = max(u for u in range(1, 13) if (tm // NORM_CHUNK) % u == 0)
    row = lambda i: (i, 0)
    return pl.pallas_call(
        functools.partial(_out_proj_kernel, tm=tm, unroll=unroll),
        grid=(rows // tm,),
        in_specs=[pl.BlockSpec((tm, D), row),
                  pl.BlockSpec((1, 1, 8, D), lambda i: (layer, 5, 0, 0)),
                  pl.BlockSpec((tm, k), row),
                  pl.BlockSpec((tm, k), row),
                  pl.BlockSpec((k, D), lambda i: (0, 0)),
                  pl.BlockSpec((k, D), lambda i: (1, 0)),
                  pl.BlockSpec((1, D), lambda i: (0, 0)),
                  pl.BlockSpec((1, 3, 8, D), lambda i: (layer, 2, 0, 0))],
        out_specs=[pl.BlockSpec((tm, D), row), pl.BlockSpec((tm, D), row)],
        out_shape=[jax.ShapeDtypeStruct((rows, D), F32), jax.ShapeDtypeStruct((rows, D), BF16)],
        compiler_params=_cparams(("arbitrary",)),
        name="out_proj",
    )(h, mods, a1, a2, wb, wb, nw_next.reshape(1, D), mods)


def _rope_tables():
    t = np.arange(S)
    rows = (t // GW).astype(np.float32)
    cols = (t % GW).astype(np.float32)
    half = HD // 2
    inv = np.float32(ROPE_BASE) ** (-np.arange(0, half, 2, dtype=np.float32) / np.float32(half))
    ar = rows[:, None] * inv
    ac = cols[:, None] * inv
    cos = np.concatenate([np.cos(ar), np.cos(ar), np.cos(ac), np.cos(ac)], axis=-1)
    sin = np.concatenate([-np.sin(ar), np.sin(ar), -np.sin(ac), np.sin(ac)], axis=-1)
    lat = np.concatenate([cos, sin], axis=-1)
    ctx = np.concatenate([np.ones((LC, HD), np.float32), np.zeros((LC, HD), np.float32)], axis=-1)
    return jnp.asarray(np.concatenate([lat, ctx], axis=0).astype(np.float32))


def _rope(x, tab):
    c = tab[:, :HD]
    sg = tab[:, HD:]
    lane = lax.broadcasted_iota(jnp.int32, x.shape, 1)
    first = (lane % (HD // 2)) < (HD // 4)
    partner = jnp.where(first, pltpu.roll(x, HD - HD // 4, 1), pltpu.roll(x, HD // 4, 1))
    return x * c + partner * sg


def _mm_rope_kernel(a_ref, w_ref, tab_ref, o_ref, *, n_rot, scaled):
    acc = jnp.dot(a_ref[...], w_ref[...], preferred_element_type=F32)
    tab = tab_ref[...]
    for hh in range(acc.shape[1] // HD):
        x = acc[:, hh * HD:(hh + 1) * HD]
        if hh < n_rot:
            x = _rope(x, tab)
        if scaled[0] <= hh < scaled[1]:
            x = x * (HD ** -0.5)
        o_ref[:, hh * HD:(hh + 1) * HD] = x.astype(o_ref.dtype)


def _mm_rope(a, wb, ncols, tabs, n_rot, *, tm, scaled=(0, 0)):
    rows, k = a.shape
    assert rows % tm == 0 and ncols % HD == 0
    return pl.pallas_call(
        functools.partial(_mm_rope_kernel, n_rot=n_rot, scaled=scaled),
        grid=(rows // tm,),
        in_specs=[pl.BlockSpec((tm, k), lambda i: (i, 0)),
                  pl.BlockSpec((k, ncols), lambda i: (0, 0)),
                  pl.BlockSpec((tm, 2 * HD), lambda i: (i, 0))],
        out_specs=pl.BlockSpec((tm, ncols), lambda i: (i, 0)),
        out_shape=jax.ShapeDtypeStruct((rows, ncols), BF16),
        compiler_params=_cparams(("arbitrary",)),
        name="mm_rope",
    )(a, wb, tabs)


def _attn_kernel(sink_ref, q_ref, kp_ref, kc_ref, kn_ref, vp_ref, vc_ref, vn_ref, kx_ref, vx_ref, o_ref):
    g = pl.program_id(0)
    n = pl.program_id(1)
    nlat = S // AQB
    scale2 = (HD ** -0.5) * LOG2E
    k_all = jnp.concatenate([kp_ref[...], kc_ref[...], kn_ref[...], kx_ref[...]], axis=0)
    v_all = jnp.concatenate([vp_ref[...], vc_ref[...], vn_ref[...], vx_ref[...]], axis=0)
    nloc = AQB + 2 * AWIN
    nk = nloc + LC
    r = lax.broadcasted_iota(jnp.int32, (AQB, nk), 0)
    c = lax.broadcasted_iota(jnp.int32, (AQB, nk), 1)
    rel = c - AWIN - r
    pos = n * AQB - AWIN + c
    hi = jnp.where(n < nlat, S, 0)
    ok = ((jnp.abs(rel) <= AWIN) & (pos >= 0) & (pos < hi)) | (c >= nloc)
    for hh in range(AH // AKV):
        q = q_ref[:, hh * HD:(hh + 1) * HD]
        s = lax.dot_general(q, k_all, (((1,), (1,)), ((), ())), preferred_element_type=F32) * scale2
        s = jnp.where(ok, s, NEG)
        sk = sink_ref[g * (AH // AKV) + hh] * LOG2E
        m = jnp.maximum(jnp.max(s, axis=-1, keepdims=True), sk)
        e = jnp.exp2(s - m)
        den = jnp.sum(e, axis=-1, keepdims=True) + jnp.exp2(sk - m)
        o = jnp.dot(e.astype(BF16), v_all, preferred_element_type=F32)
        o_ref[:, hh * HD:(hh + 1) * HD] = (o / den).astype(o_ref.dtype)


def _attn(p, sink):
    assert AQB == LC and AQB == 2 * AWIN
    nlat = S // AQB
    qw = (AH // AKV) * HD
    kcol = AH
    vcol = kcol + AKV
    last_w = S // AWIN - 1

    def prev(n):
        return jnp.clip(2 * n - 1, 0, last_w)

    def cur(n):
        return jnp.minimum(n, nlat - 1)

    def nxt(n):
        return jnp.clip(2 * n + 2, 0, last_w)

    return pl.pallas_call(
        _attn_kernel,
        grid=(AKV, M // AQB),
        in_specs=[
            pl.BlockSpec(memory_space=pltpu.SMEM),
            pl.BlockSpec((AQB, qw), lambda g, n: (n, g)),
            pl.BlockSpec((AWIN, HD), lambda g, n: (prev(n), kcol + g)),
            pl.BlockSpec((AQB, HD), lambda g, n: (cur(n), kcol + g)),
            pl.BlockSpec((AWIN, HD), lambda g, n: (nxt(n), kcol + g)),
            pl.BlockSpec((AWIN, HD), lambda g, n: (prev(n), vcol + g)),
            pl.BlockSpec((AQB, HD), lambda g, n: (cur(n), vcol + g)),
            pl.BlockSpec((AWIN, HD), lambda g, n: (nxt(n), vcol + g)),
            pl.BlockSpec((LC, HD), lambda g, n: (S // LC, kcol + g)),
            pl.BlockSpec((LC, HD), lambda g, n: (S // LC, vcol + g)),
        ],
        out_specs=pl.BlockSpec((AQB, qw), lambda g, n: (n, g)),
        out_shape=jax.ShapeDtypeStruct((M, AH * HD), BF16),
        compiler_params=_cparams(("arbitrary", "arbitrary")),
        name="attn",
    )(sink, p, p, p, p, p, p, p, p, p)


def _conv_kernel(b_ref, c_ref, u_ref, cp_ref, up_ref, cn_ref, un_ref, w_ref, bias_ref, o_ref, *, tr):
    i = pl.program_id(0)
    lat_blocks = S // tr
    z = c_ref[...] * u_ref[...]
    zp = cp_ref[7:8, :] * up_ref[7:8, :]
    zn = cn_ref[0:1, :] * un_ref[0:1, :]
    has_prev = jnp.logical_and(i != 0, i != lat_blocks)
    has_next = jnp.logical_and(i != lat_blocks - 1, i != M // tr - 1)
    zp = jnp.where(has_prev, zp, 0.0)
    zn = jnp.where(has_next, zn, 0.0)
    row = lax.broadcasted_iota(jnp.int32, z.shape, 0)
    z_m1 = jnp.where(row == 0, zp, pltpu.roll(z, 1, 0))
    z_p1 = jnp.where(row == tr - 1, zn, pltpu.roll(z, tr - 1, 0))
    conv = z_m1 * w_ref[0:1, :] + z * w_ref[1:2, :] + z_p1 * w_ref[2:3, :] + bias_ref[...]
    o_ref[...] = (b_ref[...] * conv).astype(o_ref.dtype)


def _conv(p, conv_w, conv_b):
    tr = 256
    assert LC % tr == 0 and S % tr == 0
    h8 = tr // 8

    def prev8(i):
        return jnp.maximum(i * h8 - 1, 0)

    def next8(i):
        return jnp.minimum((i + 1) * h8, M // 8 - 1)

    return pl.pallas_call(
        functools.partial(_conv_kernel, tr=tr),
        grid=(M // tr,),
        in_specs=[
            pl.BlockSpec((tr, CCH), lambda i: (i, 0)),
            pl.BlockSpec((tr, CCH), lambda i: (i, 1)),
            pl.BlockSpec((tr, CCH), lambda i: (i, 2)),
            pl.BlockSpec((8, CCH), lambda i: (prev8(i), 1)),
            pl.BlockSpec((8, CCH), lambda i: (prev8(i), 2)),
            pl.BlockSpec((8, CCH), lambda i: (next8(i), 1)),
            pl.BlockSpec((8, CCH), lambda i: (next8(i), 2)),
            pl.BlockSpec((3, CCH), lambda i: (0, 0)),
            pl.BlockSpec((1, CCH), lambda i: (0, 0)),
        ],
        out_specs=pl.BlockSpec((tr, CCH), lambda i: (i, 0)),
        out_shape=jax.ShapeDtypeStruct((M, CCH), BF16),
        compiler_params=_cparams(("arbitrary",)),
        name="conv",
    )(p, p, p, p, p, p, p, conv_w, conv_b.reshape(1, CCH))


def _log_sigmoid(d):
    return jnp.minimum(d, 0.0) - jnp.log(1.0 + jnp.exp(-jnp.abs(d)))


def _ret_first_step(decay_ref, st_ref, dm_ref, xi_ref, zeta_ref, backward):
    @pl.when(pl.program_id(0) == 0)
    def _():
        st_ref[...] = jnp.zeros_like(st_ref)
        ii = lax.broadcasted_iota(jnp.int32, (RC, RC), 0)
        jj = lax.broadcasted_iota(jnp.int32, (RC, RC), 1)
        ic = lax.broadcasted_iota(jnp.int32, (RC, 1), 0).astype(F32)
        for h in range(RH):
            lg = _log_sigmoid(jnp.full((1, 1), decay_ref[h], F32))
            if backward:
                rel = jj - ii
                keep = rel > 0
                xi_ref[h] = jnp.exp(lg * (RC - ic))
                zeta_ref[h] = jnp.exp(lg * ic)
            else:
                rel = ii - jj
                keep = rel >= 0
                xi_ref[h] = jnp.exp(lg * (ic + 1.0))
                zeta_ref[h] = jnp.exp(lg * (RC - 1.0 - ic))
            dm_ref[h] = jnp.where(keep, jnp.exp(lg * jnp.where(keep, rel, 0).astype(F32)), 0.0)


def _ret_chunk(decay_ref, q_ref, k_ref, v_ref, st_ref, dm_ref, xi_ref, zeta_ref, h):
    lg = _log_sigmoid(jnp.full((1, 1), decay_ref[h], F32))
    q = q_ref[:, h * RDK:(h + 1) * RDK]
    k = k_ref[:, h * RDK:(h + 1) * RDK]
    v = v_ref[:, h * RDV:(h + 1) * RDV]
    inner = lax.dot_general(q, k, (((1,), (1,)), ((), ())), preferred_element_type=F32) * dm_ref[h]
    y = jnp.dot(inner.astype(BF16), v.astype(BF16), preferred_element_type=F32)
    st = st_ref[h]
    y = y + jnp.dot(q, st.astype(BF16), preferred_element_type=F32) * xi_ref[h]
    kv = lax.dot_general(k, (v * zeta_ref[h]).astype(BF16), (((0,), (0,)), ((), ())),
                         preferred_element_type=F32)
    st_ref[h] = jnp.exp(lg * float(RC)) * st + kv
    return y


def _ret_fwd_kernel(decay_ref, q_ref, k_ref, v_ref, y_ref, st_ref, dm_ref, xi_ref, zeta_ref):
    _ret_first_step(decay_ref, st_ref, dm_ref, xi_ref, zeta_ref, False)
    for h in range(RH):
        y_ref[:, h * RDV:(h + 1) * RDV] = _ret_chunk(decay_ref, q_ref, k_ref, v_ref, st_ref,
                                                     dm_ref, xi_ref, zeta_ref, h)


def _ret_bwd_kernel(decay_ref, q_ref, k_ref, v_ref, yf_ref, g_ref, gnw_ref, o_ref,
                    st_ref, dm_ref, xi_ref, zeta_ref):
    _ret_first_step(decay_ref, st_ref, dm_ref, xi_ref, zeta_ref, True)
    for h in range(RH):
        sl = slice(h * RDV, (h + 1) * RDV)
        y = yf_ref[:, sl] + _ret_chunk(decay_ref, q_ref, k_ref, v_ref, st_ref,
                                       dm_ref, xi_ref, zeta_ref, h)
        mu = jnp.mean(y, axis=-1, keepdims=True)
        yc = y - mu
        var = jnp.mean(yc * yc, axis=-1, keepdims=True)
        yn = yc * lax.rsqrt(var + EPS) * gnw_ref[:, sl]
        gt = g_ref[:, sl]
        o_ref[:, sl] = (gt * jax.nn.sigmoid(gt) * yn).astype(o_ref.dtype)


def _retention(p_qk, p_vg, decay_f, decay_b, gn_w):
    nch = M // RC
    nlat = S // RC
    qw, vw = RH * RDK, RH * RDV
    fwd = lambda s: ((s + nlat) % nch)
    bwd = lambda s: (nch - 1 - s)
    smem = pl.BlockSpec(memory_space=pltpu.SMEM)

    def specs(cm):
        return [smem,
                pl.BlockSpec((RC, qw), lambda s: (cm(s), 0)),
                pl.BlockSpec((RC, qw), lambda s: (cm(s), 1)),
                pl.BlockSpec((RC, vw), lambda s: (cm(s), 0))]

    scratch = [pltpu.VMEM((RH, RDK, RDV), F32), pltpu.VMEM((RH, RC, RC), F32),
               pltpu.VMEM((RH, RC, 1), F32), pltpu.VMEM((RH, RC, 1), F32)]
    yf = pl.pallas_call(
        _ret_fwd_kernel,
        grid=(nch,),
        in_specs=specs(fwd),
        out_specs=pl.BlockSpec((RC, vw), lambda s: (fwd(s), 0)),
        out_shape=jax.ShapeDtypeStruct((M, vw), F32),
        scratch_shapes=scratch,
        compiler_params=_cparams(("arbitrary",)),
        name="ret_fwd",
    )(decay_f, p_qk, p_qk, p_vg)
    return pl.pallas_call(
        _ret_bwd_kernel,
        grid=(nch,),
        in_specs=specs(bwd) + [pl.BlockSpec((RC, vw), lambda s: (bwd(s), 0)),
                               pl.BlockSpec((RC, vw), lambda s: (bwd(s), 1)),
                               pl.BlockSpec((1, vw), lambda s: (0, 0))],
        out_specs=pl.BlockSpec((RC, vw), lambda s: (bwd(s), 0)),
        out_shape=jax.ShapeDtypeStruct((M, vw), BF16),
        scratch_shapes=scratch,
        compiler_params=_cparams(("arbitrary",)),
        name="ret_bwd",
    )(decay_b, p_qk, p_qk, p_vg, yf, p_vg, gn_w.reshape(1, vw))


_NA_NBLK = S // NQB
_NA_CASES = (0, 1, _NA_NBLK - 1)


def _na_window_row(b):
    return jnp.clip(NQROWS * b - NROWS // 2, 0, GROWS - NWIN_ROWS)


def _na_table_kernel(rpb_ref, o_ref):
    h = pl.program_id(0)
    n_dr, n_dc = 2 * NROWS - 1, 2 * NCOLS - 1
    cq = lax.broadcasted_iota(jnp.int32, (GW, 2 * GW), 0)
    lane = lax.broadcasted_iota(jnp.int32, (GW, 2 * GW), 1)
    right = lane >= GW
    kc = jnp.where(right, lane - GW, lane)
    dc = kc - cq + (NCOLS - 1)
    cs = jnp.clip(cq - NCOLS // 2, 0, GW - NCOLS)
    col_ok = (kc >= cs) & (kc < cs + NCOLS)
    memo = {}

    def row_scalar(a, b):
        return rpb_ref[(h * n_dr + a) * n_dc + b] if 0 <= a < n_dr else 0.0

    def pair(a0):
        if a0 not in memo:
            acc = jnp.zeros((GW, 2 * GW), F32)
            for b in range(n_dc):
                vec = jnp.where(right, row_scalar(a0 + 1, b), row_scalar(a0, b))
                acc = jnp.where(dc == b, vec, acc)
            memo[a0] = acc
        return memo[a0]

    for ci, blk in enumerate(_NA_CASES):
        w = int(np.clip(NQROWS * blk - NROWS // 2, 0, GROWS - NWIN_ROWS))
        for rl in range(NQROWS):
            r = NQROWS * blk + rl
            rs = int(np.clip(r - NROWS // 2, 0, GROWS - NROWS))
            for pr in range(NWIN_ROWS // 2):
                kr = w + 2 * pr
                ok_l = rs <= kr < rs + NROWS
                ok_r = rs <= kr + 1 < rs + NROWS
                if ok_l and ok_r:
                    ok = col_ok
                elif ok_l:
                    ok = col_ok & jnp.logical_not(right)
                elif ok_r:
                    ok = col_ok & right
                else:
                    ok = None
                if ok is None:
                    tile = jnp.full((GW, 2 * GW), NEG, F32)
                else:
                    tile = jnp.where(ok, pair(kr - r + NROWS - 1) * LOG2E, NEG)
                o_ref[ci, 0, rl * GW:(rl + 1) * GW, pr * 2 * GW:(pr + 1) * 2 * GW] = tile


def _na_tables(rpb):
    return pl.pallas_call(
        _na_table_kernel,
        grid=(NH,),
        in_specs=[pl.BlockSpec(memory_space=pltpu.SMEM)],
        out_specs=pl.BlockSpec((len(_NA_CASES), 1, NQB, NWIN), lambda h: (0, h, 0, 0)),
        out_shape=jax.ShapeDtypeStruct((len(_NA_CASES), NH, NQB, NWIN), F32),
        compiler_params=_cparams(("arbitrary",)),
        name="na_tables",
    )(rpb.reshape(-1))


def _na_kernel(q_ref, k_ref, v_ref, t_ref, o_ref):
    b = pl.program_id(1)
    scale2 = (HD ** -0.5) * LOG2E
    start = pl.multiple_of(_na_window_row(b) * GW, NQB)
    nt = (((1,), (1,)), ((), ()))
    for hh in range(NHPS):
        cs = slice(hh * HD, (hh + 1) * HD)
        q = q_ref[:, cs]
        s_loc = lax.dot_general(q, k_ref[pl.ds(start, NWIN), cs], nt,
                                preferred_element_type=F32) * scale2 + t_ref[0, hh]
        s_ctx = lax.dot_general(q, k_ref[S:M, cs], nt, preferred_element_type=F32) * scale2
        m = jnp.maximum(jnp.max(s_loc, axis=-1, keepdims=True), jnp.max(s_ctx, axis=-1, keepdims=True))
        e_loc = jnp.exp2(s_loc - m)
        e_ctx = jnp.exp2(s_ctx - m)
        den = jnp.sum(e_loc, axis=-1, keepdims=True) + jnp.sum(e_ctx, axis=-1, keepdims=True)
        o = (jnp.dot(e_loc.astype(BF16), v_ref[pl.ds(start, NWIN), cs], preferred_element_type=F32)
             + jnp.dot(e_ctx.astype(BF16), v_ref[S:M, cs], preferred_element_type=F32))
        o_ref[:, cs] = (o / den).astype(o_ref.dtype)


def _na(p, rpb):
    gw = NHPS * HD
    qcol = 0
    kcol = qcol + NH // NHPS
    vcol = kcol + NH // NHPS
    tables = _na_tables(rpb)

    def case(b):
        return jnp.minimum(b, 1) + jnp.maximum(b - (_NA_NBLK - 2), 0)

    return pl.pallas_call(
        _na_kernel,
        grid=(NH // NHPS, _NA_NBLK),
        in_specs=[
            pl.BlockSpec((NQB, gw), lambda h, b: (b, qcol + h)),
            pl.BlockSpec((M, gw), lambda h, b: (0, kcol + h)),
            pl.BlockSpec((M, gw), lambda h, b: (0, vcol + h)),
            pl.BlockSpec((1, NHPS, NQB, NWIN), lambda h, b: (case(b), h, 0, 0)),
        ],
        out_specs=pl.BlockSpec((NQB, gw), lambda h, b: (b, h)),
        out_shape=jax.ShapeDtypeStruct((S, NH * HD), BF16),
        compiler_params=_cparams(("arbitrary", "arbitrary")),
        name="na",
    )(p, p, p, tables)


def _tiles(rows):
    if rows == M:
        return dict(up=(2112, 1056), small=528, inp=1056)
    assert rows == S
    return dict(up=(2048, 1024), small=512, inp=1024)


def _ffn_half(h, hn, rows, mods, layer, third, wi, wo, nw_next, next_mod, cast_jobs=()):
    t = _tiles(rows)
    a, wob = _mm_swiglu(hn, wi, wo, layer, tm=t["up"][0], mm=t["up"][1], tn=512)
    return _ffn_down(h, rows, mods, layer, 3 * third + 2, a, wob, nw_next, next_mod, cast_jobs)


IN_TN = 1536


def kernel(x, c, ctx, c_ctx, ada_w, ada_b, norm_w, ffn_a_wi, ffn_a_wo, ffn_b_wi, ffn_b_wo,
           ev_w_in, ev_w_out, ev_sink, ev_conv_w, ev_conv_b,
           od_w_in, od_w_out, od_decay_f, od_decay_b, od_gn_w, od_rpb, final_norm_w):
    assert x.shape == (1, S, D) and ctx.shape == (1, LC, D) and ada_w.shape[0] == 2
    cvec = jnp.concatenate([c, c_ctx[None, :], jnp.zeros((6, D), F32)], axis=0)
    mods = _ada(cvec, ada_w, ada_b)
    tabs = _rope_tables()

    tm_in = _tiles(M)["inp"]
    hn = _normmod_first(x[0], ctx[0], norm_w[0, 0], mods)
    (h, hn), (w_in, w_out) = _ffn_half((x[0], ctx[0]), hn, M, mods, 0, 0, ffn_a_wi, ffn_a_wo,
                                       norm_w[0, 1], (0, 1), cast_jobs=((ev_w_in, 0), (ev_w_out, 0)))
    n_att = (AH + 2 * AKV) * HD
    p_att = _mm_rope(hn, w_in, n_att, tabs, AH + AKV, tm=tm_in)
    p_conv = _mm_plain(hn, w_in, n_att, EV_IN - n_att, tm=tm_in, tn=IN_TN, out_dtype=F32)
    att = _attn(p_att, ev_sink[0])
    cnv = _conv(p_conv, ev_conv_w[0], ev_conv_b[0])
    h, hn = _out_proj(h, M, mods, 0, att, cnv, w_out, norm_w[0, 2], tm=_tiles(M)["small"])
    (h, hn), _ = _ffn_half(h, hn, M, mods, 0, 2, ffn_b_wi, ffn_b_wo, norm_w[1, 0], (1, 0))

    (h, hn), (w_in, w_out) = _ffn_half(h, hn, M, mods, 1, 0, ffn_a_wi, ffn_a_wo, norm_w[1, 1], (1, 1),
                                       cast_jobs=((od_w_in, 0), (od_w_out, 0)))
    n_qk, n_vg = 2 * RH * RDK, 2 * RH * RDV
    p_qk = _mm_rope(hn, w_in, n_qk, tabs, 2 * RH, tm=tm_in, scaled=(RH, 2 * RH))
    p_vg = _mm_plain(hn, w_in, n_qk, n_vg, tm=tm_in, tn=n_qk, out_dtype=F32)
    p_na = _mm_plain(hn, w_in, n_qk + n_vg, OD_IN - n_qk - n_vg, tm=tm_in, tn=IN_TN, out_dtype=BF16)
    ret = _retention(p_qk, p_vg, od_decay_f[0], od_decay_b[0], od_gn_w[0])
    nat = _na(p_na, od_rpb[0])
    h, hn = _out_proj(h, S, mods, 1, ret, nat, w_out, norm_w[1, 2], tm=_tiles(S)["small"])
    (out,), _ = _ffn_half(h, hn, S, mods, 1, 2, ffn_b_wi, ffn_b_wo, final_norm_w, None)
    return out[None]
```

```python
import functools

import numpy as np
import jax
import jax.numpy as jnp
from jax import lax
from jax.experimental import pallas as pl
from jax.experimental.pallas import tpu as pltpu

F32 = jnp.float32
BF16 = jnp.bfloat16

D = 2048
S = 8192
LC = 256
M = S + LC
GW = 64
GROWS = S // GW
HD = 128
DFF = 5632
NMOD = 9
EPS = 1e-6
ROPE_BASE = 10000.0
NEG = -1e30

AH, AKV, AWIN = 8, 2, 128
AQB = 256
CCH = 1024
EV_IN = AH * HD + 2 * AKV * HD + 3 * CCH
RH, RDK, RDV = 4, 128, 256
RC = 256
NH, NROWS, NCOLS = 8, 8, 16
NQROWS = 4
NQB = NQROWS * GW
NWIN_ROWS = NQROWS + NROWS
NWIN = NWIN_ROWS * GW
NHPS = 4
LOG2E = float(np.log2(np.e))
OD_IN = 2 * RH * RDK + 2 * RH * RDV + 3 * NH * HD

VMEM_LIMIT = 56 * 1024 * 1024


def _cparams(sem):
    return pltpu.CompilerParams(dimension_semantics=sem, vmem_limit_bytes=VMEM_LIMIT)


def _sub(r, size):
    return pl.ds(pl.multiple_of(r * size, 16), size)


def _ada_kernel(c_ref, w_ref, b_ref, o_ref):
    cv = c_ref[...]
    a = cv * jax.nn.sigmoid(cv)
    acc = jnp.dot(a.astype(BF16), w_ref[0].astype(BF16), preferred_element_type=F32)
    o_ref[0, 0] = acc + b_ref[0]


def _ada(cvec, ada_w, ada_b):
    depth = ada_w.shape[0]
    tn = D
    per = D // tn
    return pl.pallas_call(
        _ada_kernel,
        grid=(depth, NMOD * per),
        in_specs=[
            pl.BlockSpec((8, D), lambda l, j: (0, 0)),
            pl.BlockSpec((1, D, tn), lambda l, j: (l, 0, j)),
            pl.BlockSpec((1, 1, tn), lambda l, j: (l, 0, j)),
        ],
        out_specs=pl.BlockSpec((1, 1, 8, tn), lambda l, j: (l, j // per, 0, j % per)),
        out_shape=jax.ShapeDtypeStruct((depth, NMOD, 8, D), F32),
        compiler_params=_cparams(("arbitrary", "arbitrary")),
        name="ada",
    )(cvec, ada_w, ada_b.reshape(depth, 1, NMOD * D))


NORM_TR = 1024
NORM_CHUNK = 16
NORM_UNROLL = 8


def _norm_rows(src_ref, n_rows, fn, unroll=NORM_UNROLL):
    group = NORM_CHUNK * unroll

    def outer(gi, carry):
        for u in range(unroll):
            start = pl.multiple_of(gi * group + u * NORM_CHUNK, NORM_CHUNK)
            rs = pl.ds(start, NORM_CHUNK)
            fn(rs, src_ref[rs, :], start)
        return carry

    lax.fori_loop(0, n_rows // group, outer, 0)


def _normmod_first_kernel(x_ref, c_ref, nw_ref, mod_ref, o_ref):
    i = pl.program_id(0)
    kind = (i * NORM_TR >= S).astype(jnp.int32)
    n_rows = jnp.minimum(M - i * NORM_TR, NORM_TR)
    shift = mod_ref[0, 0, pl.ds(kind, 1), :]
    gain = nw_ref[...] * (1.0 + mod_ref[0, 1, pl.ds(kind, 1), :])

    def emit(rs, x, start):
        rinv = lax.rsqrt(jnp.mean(x * x, axis=-1, keepdims=True) + EPS)
        o_ref[rs, :] = ((x * rinv) * gain + shift).astype(o_ref.dtype)

    @pl.when(kind == 0)
    def _():
        _norm_rows(x_ref, n_rows, emit)

    @pl.when(kind == 1)
    def _():
        _norm_rows(c_ref, n_rows, emit)


def _normmod_first(x, ctx, nw, mods):
    assert S % NORM_TR == 0 and LC % (NORM_CHUNK * NORM_UNROLL) == 0 and LC <= NORM_TR
    return pl.pallas_call(
        _normmod_first_kernel,
        grid=(pl.cdiv(M, NORM_TR),),
        in_specs=[pl.BlockSpec((NORM_TR, D), lambda i: (jnp.minimum(i, S // NORM_TR - 1), 0)),
                  pl.BlockSpec((LC, D), lambda i: (0, 0)),
                  pl.BlockSpec((1, D), lambda i: (0, 0)),
                  pl.BlockSpec((1, 3, 8, D), lambda i: (0, 0, 0, 0))],
        out_specs=pl.BlockSpec((NORM_TR, D), lambda i: (i, 0)),
        out_shape=jax.ShapeDtypeStruct((M, D), BF16),
        compiler_params=_cparams(("arbitrary",)),
        name="normmod_first",
    )(x, ctx, nw.reshape(1, D), mods)


def _cast_weights(w_refs, wb_refs):
    @pl.when(pl.program_id(1) == 0)
    def _():
        for w_ref, wb_ref in zip(w_refs, wb_refs):
            wb_ref[...] = w_ref[...].astype(BF16)


def _mm_swiglu_kernel(a_ref, wg_ref, wu_ref, wo_ref, o_ref, wob_ref, wgb_ref, wub_ref, *, tm, mm):
    _cast_weights((wg_ref, wu_ref), (wgb_ref, wub_ref))
    wob_ref[...] = wo_ref[...].astype(BF16)

    def body(r, carry):
        a = a_ref[_sub(r, mm), :]
        g = jnp.dot(a, wgb_ref[...], preferred_element_type=F32)
        u = jnp.dot(a, wub_ref[...], preferred_element_type=F32)
        o_ref[_sub(r, mm), :] = (g * jax.nn.sigmoid(g) * u).astype(o_ref.dtype)
        return carry

    lax.fori_loop(0, tm // mm, body, 0, unroll=True)


def _mm_plain_kernel(a_ref, w_ref, o_ref):
    o_ref[...] = jnp.dot(a_ref[...], w_ref[...], preferred_element_type=F32).astype(o_ref.dtype)


def _mm_swiglu(a, wi, wo, layer, *, tm, mm, tn):
    rows, k = a.shape
    assert rows % tm == 0 and tm % mm == 0 and mm % 16 == 0
    n_out = wi.shape[2] // 2
    nj, ni = n_out // tn, rows // tm
    assert wo.shape[1] % (nj * ni) == 0
    slab = wo.shape[1] // (nj * ni)
    assert slab % 16 == 0
    return pl.pallas_call(
        functools.partial(_mm_swiglu_kernel, tm=tm, mm=mm),
        grid=(nj, ni),
        in_specs=[pl.BlockSpec((tm, k), lambda j, i: (i, 0)),
                  pl.BlockSpec((None, k, tn), lambda j, i: (layer, 0, j)),
                  pl.BlockSpec((None, k, tn), lambda j, i: (layer, 0, j + nj)),
                  pl.BlockSpec((None, slab, D), lambda j, i: (layer, j * ni + i, 0))],
        out_specs=[pl.BlockSpec((tm, tn), lambda j, i: (i, j)),
                   pl.BlockSpec((slab, D), lambda j, i: (j * ni + i, 0))],
        out_shape=[jax.ShapeDtypeStruct((rows, n_out), BF16),
                   jax.ShapeDtypeStruct((wo.shape[1], D), BF16)],
        scratch_shapes=[pltpu.VMEM((k, tn), BF16)] * 2,
        compiler_params=_cparams(("arbitrary", "arbitrary")),
        name="mm_swiglu",
    )(a, wi, wi, wo)


def _mm_plain(a, wb, col0, ncols, *, tm, tn, out_dtype):
    rows, k = a.shape
    assert rows % tm == 0 and ncols % tn == 0 and col0 % tn == 0
    return pl.pallas_call(
        _mm_plain_kernel,
        grid=(ncols // tn, rows // tm),
        in_specs=[pl.BlockSpec((tm, k), lambda j, i: (i, 0)),
                  pl.BlockSpec((k, tn), lambda j, i: (0, col0 // tn + j))],
        out_specs=pl.BlockSpec((tm, tn), lambda j, i: (i, j)),
        out_shape=jax.ShapeDtypeStruct((rows, ncols), out_dtype),
        compiler_params=_cparams(("arbitrary", "arbitrary")),
        name="mm_plain",
    )(a, wb)


def _gate_rows(gate_ref, i, tm):
    rows = i * tm + lax.broadcasted_iota(jnp.int32, (tm, 1), 0)
    return jnp.where(rows >= S, gate_ref[0, 0, 1:2, :], gate_ref[0, 0, 0:1, :])


DOWN_TM = 256


def _ffn_down_kernel(*refs, n_jobs, final, split):
    it = iter(refs)
    h_ref = next(it)
    c_ref = next(it) if split else None
    gate_ref, a_ref, w_ref, nw_ref = next(it), next(it), next(it), next(it)
    mod_ref = None if final else next(it)
    job_in = [next(it) for _ in range(n_jobs)]
    o_ref = next(it)
    hn_ref = None if final else next(it)
    job_out = [next(it) for _ in range(n_jobs)]
    new_ref = next(it) if final else o_ref
    for src_ref, dst_ref in zip(job_in, job_out):
        dst_ref[...] = src_ref[...].astype(BF16)
    kind = (pl.program_id(0) * DOWN_TM >= S).astype(jnp.int32)
    acc = jnp.dot(a_ref[...], w_ref[...], preferred_element_type=F32)
    res = jnp.where(kind == 1, c_ref[...], h_ref[...]) if split else h_ref[...]
    new_ref[...] = res + (0.5 * gate_ref[0, 0, pl.ds(kind, 1), :]) * acc
    nw = nw_ref[...]
    if final:
        def emit(rs, x, start):
            rinv = lax.rsqrt(jnp.mean(x * x, axis=-1, keepdims=True) + EPS)
            o_ref[rs, :] = x * rinv * nw
    else:
        shift = mod_ref[0, 0, pl.ds(kind, 1), :]
        gain = nw * (1.0 + mod_ref[0, 1, pl.ds(kind, 1), :])

        def emit(rs, x, start):
            rinv = lax.rsqrt(jnp.mean(x * x, axis=-1, keepdims=True) + EPS)
            hn_ref[rs, :] = ((x * rinv) * gain + shift).astype(hn_ref.dtype)

    _norm_rows(new_ref, DOWN_TM, emit)


def _ffn_down(h, rows, mods, layer, gate_idx, a, wb, nw_next, next_mod=None, cast_jobs=()):
    k = a.shape[1]
    final = next_mod is None
    split = isinstance(h, tuple)
    tm = DOWN_TM
    assert rows % tm == 0 and S % tm == 0 and tm % (NORM_CHUNK * NORM_UNROLL) == 0
    assert not split or (rows == M and LC == tm)
    n_slabs = S // tm
    row = lambda i: (i, 0)
    job_in, job_out, job_shape = [], [], []
    for w, idx in cast_jobs:
        kk, nn = w.shape[1:]
        assert kk % n_slabs == 0 and (kk // n_slabs) % 16 == 0
        slab = kk // n_slabs
        job_in.append(pl.BlockSpec((None, slab, nn),
                                   lambda i, idx=idx: (idx, jnp.minimum(i, n_slabs - 1), 0)))
        job_out.append(pl.BlockSpec((slab, nn), lambda i: (jnp.minimum(i, n_slabs - 1), 0)))
        job_shape.append(jax.ShapeDtypeStruct((kk, nn), BF16))
    if split:
        h_specs = [pl.BlockSpec((tm, D), lambda i: (jnp.minimum(i, n_slabs - 1), 0)),
                   pl.BlockSpec((tm, D), lambda i: (0, 0))]
        h_args = list(h)
    else:
        h_specs, h_args = [pl.BlockSpec((tm, D), row)], [h]
    in_specs = h_specs + [pl.BlockSpec((1, 1, 8, D), lambda i: (layer, gate_idx, 0, 0)),
                          pl.BlockSpec((tm, k), row),
                          pl.BlockSpec((k, D), lambda i: (0, 0), pipeline_mode=pl.Buffered(1)),
                          pl.BlockSpec((1, D), lambda i: (0, 0))]
    args = h_args + [mods, a, wb, nw_next.reshape(1, D)]
    out_specs = [pl.BlockSpec((tm, D), row)]
    out_shape = [jax.ShapeDtypeStruct((rows, D), F32)]
    if not final:
        in_specs.append(pl.BlockSpec((1, 3, 8, D), lambda i: (next_mod[0], next_mod[1], 0, 0)))
        args.append(mods)
        out_specs.append(pl.BlockSpec((tm, D), row))
        out_shape.append(jax.ShapeDtypeStruct((rows, D), BF16))
    n_main = len(out_shape)
    res = pl.pallas_call(
        functools.partial(_ffn_down_kernel, n_jobs=len(cast_jobs), final=final, split=split),
        grid=(rows // tm,),
        in_specs=in_specs + job_in,
        out_specs=out_specs + job_out,
        out_shape=out_shape + job_shape,
        scratch_shapes=[pltpu.VMEM((tm, D), F32)] if final else [],
        compiler_params=_cparams(("arbitrary",)),
        name="ffn_down_final" if final else "ffn_down",
    )(*args, *[w for w, _ in cast_jobs])
    return tuple(res[:n_main]), list(res[n_main:])


def _out_proj_kernel(h_ref, gate_ref, a1_ref, a2_ref, w1_ref, w2_ref, nw_ref, mod_ref,
                     o_ref, hn_ref, *, tm, unroll):
    i = pl.program_id(0)
    acc = (jnp.dot(a1_ref[...], w1_ref[...], preferred_element_type=F32)
           + jnp.dot(a2_ref[...], w2_ref[...], preferred_element_type=F32))
    o_ref[...] = h_ref[...] + _gate_rows(gate_ref, i, tm) * acc
    nw = nw_ref[...]
    gain_lat = nw * (1.0 + mod_ref[0, 1, 0:1, :])
    gain_ctx = nw * (1.0 + mod_ref[0, 1, 1:2, :])
    shift_lat = mod_ref[0, 0, 0:1, :]
    shift_ctx = mod_ref[0, 0, 1:2, :]

    def emit(rs, x, start):
        isc = (i * tm + start + lax.broadcasted_iota(jnp.int32, (NORM_CHUNK, 1), 0)) >= S
        rinv = lax.rsqrt(jnp.mean(x * x, axis=-1, keepdims=True) + EPS)
        gain = jnp.where(isc, gain_ctx, gain_lat)
        shift = jnp.where(isc, shift_ctx, shift_lat)
        hn_ref[rs, :] = ((x * rinv) * gain + shift).astype(hn_ref.dtype)

    _norm_rows(o_ref, tm, emit, unroll=unroll)


def _out_proj(h, rows, mods, layer, a1, a2, wb, nw_next, *, tm):
    k = a1.shape[1]
    assert a2.shape[1] == k and wb.shape == (2 * k, D) and rows % tm == 0
    unroll = max(u for u in range(1, 13) if (tm // NORM_CHUNK) % u == 0)
    row = lambda i: (i, 0)
    return pl.pallas_call(
        functools.partial(_out_proj_kernel, tm=tm, unroll=unroll),
        grid=(rows // tm,),
        in_specs=[pl.BlockSpec((tm, D), row),
                  pl.BlockSpec((1, 1, 8, D), lambda i: (layer, 5, 0, 0)),
                  pl.BlockSpec((tm, k), row),
                  pl.BlockSpec((tm, k), row),
                  pl.BlockSpec((k, D), lambda i: (0, 0)),
                  pl.BlockSpec((k, D), lambda i: (1, 0)),
                  pl.BlockSpec((1, D), lambda i: (0, 0)),
                  pl.BlockSpec((1, 3, 8, D), lambda i: (layer, 2, 0, 0))],
        out_specs=[pl.BlockSpec((tm, D), row), pl.BlockSpec((tm, D), row)],
        out_shape=[jax.ShapeDtypeStruct((rows, D), F32), jax.ShapeDtypeStruct((rows, D), BF16)],
        compiler_params=_cparams(("arbitrary",)),
        name="out_proj",
    )(h, mods, a1, a2, wb, wb, nw_next.reshape(1, D), mods)


def _rope_tables():
    t = np.arange(S)
    rows = (t // GW).astype(np.float32)
    cols = (t % GW).astype(np.float32)
    half = HD // 2
    inv = np.float32(ROPE_BASE) ** (-np.arange(0, half, 2, dtype=np.float32) / np.float32(half))
    ar = rows[:, None] * inv
    ac = cols[:, None] * inv
    cos = np.concatenate([np.cos(ar), np.cos(ar), np.cos(ac), np.cos(ac)], axis=-1)
    sin = np.concatenate([-np.sin(ar), np.sin(ar), -np.sin(ac), np.sin(ac)], axis=-1)
    lat = np.concatenate([cos, sin], axis=-1)
    ctx = np.concatenate([np.ones((LC, HD), np.float32), np.zeros((LC, HD), np.float32)], axis=-1)
    return jnp.asarray(np.concatenate([lat, ctx], axis=0).astype(np.float32))


def _rope(x, tab):
    c = tab[:, :HD]
    sg = tab[:, HD:]
    lane = lax.broadcasted_iota(jnp.int32, x.shape, 1)
    first = (lane % (HD // 2)) < (HD // 4)
    partner = jnp.where(first, pltpu.roll(x, HD - HD // 4, 1), pltpu.roll(x, HD // 4, 1))
    return x * c + partner * sg


def _mm_rope_kernel(a_ref, w_ref, tab_ref, o_ref, *, n_rot, scaled):
    acc = jnp.dot(a_ref[...], w_ref[...], preferred_element_type=F32)
    tab = tab_ref[...]
    for hh in range(acc.shape[1] // HD):
        x = acc[:, hh * HD:(hh + 1) * HD]
        if hh < n_rot:
            x = _rope(x, tab)
        if scaled[0] <= hh < scaled[1]:
            x = x * (HD ** -0.5)
        o_ref[:, hh * HD:(hh + 1) * HD] = x.astype(o_ref.dtype)


def _mm_rope(a, wb, ncols, tabs, n_rot, *, tm, scaled=(0, 0)):
    rows, k = a.shape
    assert rows % tm == 0 and ncols % HD == 0
    return pl.pallas_call(
        functools.partial(_mm_rope_kernel, n_rot=n_rot, scaled=scaled),
        grid=(rows // tm,),
        in_specs=[pl.BlockSpec((tm, k), lambda i: (i, 0)),
                  pl.BlockSpec((k, ncols), lambda i: (0, 0)),
                  pl.BlockSpec((tm, 2 * HD), lambda i: (i, 0))],
        out_specs=pl.BlockSpec((tm, ncols), lambda i: (i, 0)),
        out_shape=jax.ShapeDtypeStruct((rows, ncols), BF16),
        compiler_params=_cparams(("arbitrary",)),
        name="mm_rope",
    )(a, wb, tabs)


def _attn_kernel(sink_ref, q_ref, kp_ref, kc_ref, kn_ref, vp_ref, vc_ref, vn_ref, kx_ref, vx_ref, o_ref):
    g = pl.program_id(0)
    n = pl.program_id(1)
    nlat = S // AQB
    scale2 = (HD ** -0.5) * LOG2E
    k_all = jnp.concatenate([kp_ref[...], kc_ref[...], kn_ref[...], kx_ref[...]], axis=0)
    v_all = jnp.concatenate([vp_ref[...], vc_ref[...], vn_ref[...], vx_ref[...]], axis=0)
    nloc = AQB + 2 * AWIN
    nk = nloc + LC
    r = lax.broadcasted_iota(jnp.int32, (AQB, nk), 0)
    c = lax.broadcasted_iota(jnp.int32, (AQB, nk), 1)
    rel = c - AWIN - r
    pos = n * AQB - AWIN + c
    hi = jnp.where(n < nlat, S, 0)
    ok = ((jnp.abs(rel) <= AWIN) & (pos >= 0) & (pos < hi)) | (c >= nloc)
    for hh in range(AH // AKV):
        q = q_ref[:, hh * HD:(hh + 1) * HD]
        s = lax.dot_general(q, k_all, (((1,), (1,)), ((), ())), preferred_element_type=F32) * scale2
        s = jnp.where(ok, s, NEG)
        sk = sink_ref[g * (AH // AKV) + hh] * LOG2E
        m = jnp.maximum(jnp.max(s, axis=-1, keepdims=True), sk)
        e = jnp.exp2(s - m)
        den = jnp.sum(e, axis=-1, keepdims=True) + jnp.exp2(sk - m)
        o = jnp.dot(e.astype(BF16), v_all, preferred_element_type=F32)
        o_ref[:, hh * HD:(hh + 1) * HD] = (o / den).astype(o_ref.dtype)


def _attn(p, sink):
    assert AQB == LC and AQB == 2 * AWIN
    nlat = S // AQB
    qw = (AH // AKV) * HD
    kcol = AH
    vcol = kcol + AKV
    last_w = S // AWIN - 1

    def prev(n):
        return jnp.clip(2 * n - 1, 0, last_w)

    def cur(n):
        return jnp.minimum(n, nlat - 1)

    def nxt(n):
        return jnp.clip(2 * n + 2, 0, last_w)

    return pl.pallas_call(
        _attn_kernel,
        grid=(AKV, M // AQB),
        in_specs=[
            pl.BlockSpec(memory_space=pltpu.SMEM),
            pl.BlockSpec((AQB, qw), lambda g, n: (n, g)),
            pl.BlockSpec((AWIN, HD), lambda g, n: (prev(n), kcol + g)),
            pl.BlockSpec((AQB, HD), lambda g, n: (cur(n), kcol + g)),
            pl.BlockSpec((AWIN, HD), lambda g, n: (nxt(n), kcol + g)),
            pl.BlockSpec((AWIN, HD), lambda g, n: (prev(n), vcol + g)),
            pl.BlockSpec((AQB, HD), lambda g, n: (cur(n), vcol + g)),
            pl.BlockSpec((AWIN, HD), lambda g, n: (nxt(n), vcol + g)),
            pl.BlockSpec((LC, HD), lambda g, n: (S // LC, kcol + g)),
            pl.BlockSpec((LC, HD), lambda g, n: (S // LC, vcol + g)),
        ],
        out_specs=pl.BlockSpec((AQB, qw), lambda g, n: (n, g)),
        out_shape=jax.ShapeDtypeStruct((M, AH * HD), BF16),
        compiler_params=_cparams(("arbitrary", "arbitrary")),
        name="attn",
    )(sink, p, p, p, p, p, p, p, p, p)


def _conv_kernel(b_ref, c_ref, u_ref, cp_ref, up_ref, cn_ref, un_ref, w_ref, bias_ref, o_ref, *, tr):
    i = pl.program_id(0)
    lat_blocks = S // tr
    z = c_ref[...] * u_ref[...]
    zp = cp_ref[7:8, :] * up_ref[7:8, :]
    zn = cn_ref[0:1, :] * un_ref[0:1, :]
    has_prev = jnp.logical_and(i != 0, i != lat_blocks)
    has_next = jnp.logical_and(i != lat_blocks - 1, i != M // tr - 1)
    zp = jnp.where(has_prev, zp, 0.0)
    zn = jnp.where(has_next, zn, 0.0)
    row = lax.broadcasted_iota(jnp.int32, z.shape, 0)
    z_m1 = jnp.where(row == 0, zp, pltpu.roll(z, 1, 0))
    z_p1 = jnp.where(row == tr - 1, zn, pltpu.roll(z, tr - 1, 0))
    conv = z_m1 * w_ref[0:1, :] + z * w_ref[1:2, :] + z_p1 * w_ref[2:3, :] + bias_ref[...]
    o_ref[...] = (b_ref[...] * conv).astype(o_ref.dtype)


def _conv(p, conv_w, conv_b):
    tr = 256
    assert LC % tr == 0 and S % tr == 0
    h8 = tr // 8

    def prev8(i):
        return jnp.maximum(i * h8 - 1, 0)

    def next8(i):
        return jnp.minimum((i + 1) * h8, M // 8 - 1)

    return pl.pallas_call(
        functools.partial(_conv_kernel, tr=tr),
        grid=(M // tr,),
        in_specs=[
            pl.BlockSpec((tr, CCH), lambda i: (i, 0)),
            pl.BlockSpec((tr, CCH), lambda i: (i, 1)),
            pl.BlockSpec((tr, CCH), lambda i: (i, 2)),
            pl.BlockSpec((8, CCH), lambda i: (prev8(i), 1)),
            pl.BlockSpec((8, CCH), lambda i: (prev8(i), 2)),
            pl.BlockSpec((8, CCH), lambda i: (next8(i), 1)),
            pl.BlockSpec((8, CCH), lambda i: (next8(i), 2)),
            pl.BlockSpec((3, CCH), lambda i: (0, 0)),
            pl.BlockSpec((1, CCH), lambda i: (0, 0)),
        ],
        out_specs=pl.BlockSpec((tr, CCH), lambda i: (i, 0)),
        out_shape=jax.ShapeDtypeStruct((M, CCH), BF16),
        compiler_params=_cparams(("arbitrary",)),
        name="conv",
    )(p, p, p, p, p, p, p, conv_w, conv_b.reshape(1, CCH))


def _log_sigmoid(d):
    return jnp.minimum(d, 0.0) - jnp.log(1.0 + jnp.exp(-jnp.abs(d)))


def _ret_first_step(decay_ref, st_ref, dm_ref, xi_ref, zeta_ref, backward):
    @pl.when(pl.program_id(0) == 0)
    def _():
        st_ref[...] = jnp.zeros_like(st_ref)
        ii = lax.broadcasted_iota(jnp.int32, (RC, RC), 0)
        jj = lax.broadcasted_iota(jnp.int32, (RC, RC), 1)
        ic = lax.broadcasted_iota(jnp.int32, (RC, 1), 0).astype(F32)
        for h in range(RH):
            lg = _log_sigmoid(jnp.full((1, 1), decay_ref[h], F32))
            if backward:
                rel = jj - ii
                keep = rel > 0
                xi_ref[h] = jnp.exp(lg * (RC - ic))
                zeta_ref[h] = jnp.exp(lg * ic)
            else:
                rel = ii - jj
                keep = rel >= 0
                xi_ref[h] = jnp.exp(lg * (ic + 1.0))
                zeta_ref[h] = jnp.exp(lg * (RC - 1.0 - ic))
            dm_ref[h] = jnp.where(keep, jnp.exp(lg * jnp.where(keep, rel, 0).astype(F32)), 0.0)


def _ret_chunk(decay_ref, q_ref, k_ref, v_ref, st_ref, dm_ref, xi_ref, zeta_ref, h):
    lg = _log_sigmoid(jnp.full((1, 1), decay_ref[h], F32))
    q = q_ref[:, h * RDK:(h + 1) * RDK]
    k = k_ref[:, h * RDK:(h + 1) * RDK]
    v = v_ref[:, h * RDV:(h + 1) * RDV]
    inner = lax.dot_general(q, k, (((1,), (1,)), ((), ())), preferred_element_type=F32) * dm_ref[h]
    y = jnp.dot(inner.astype(BF16), v.astype(BF16), preferred_element_type=F32)
    st = st_ref[h]
    y = y + jnp.dot(q, st.astype(BF16), preferred_element_type=F32) * xi_ref[h]
    kv = lax.dot_general(k, (v * zeta_ref[h]).astype(BF16), (((0,), (0,)), ((), ())),
                         preferred_element_type=F32)
    st_ref[h] = jnp.exp(lg * float(RC)) * st + kv
    return y


def _ret_fwd_kernel(decay_ref, q_ref, k_ref, v_ref, y_ref, st_ref, dm_ref, xi_ref, zeta_ref):
    _ret_first_step(decay_ref, st_ref, dm_ref, xi_ref, zeta_ref, False)
    for h in range(RH):
        y_ref[:, h * RDV:(h + 1) * RDV] = _ret_chunk(decay_ref, q_ref, k_ref, v_ref, st_ref,
                                                     dm_ref, xi_ref, zeta_ref, h)


def _ret_bwd_kernel(decay_ref, q_ref, k_ref, v_ref, yf_ref, g_ref, gnw_ref, o_ref,
                    st_ref, dm_ref, xi_ref, zeta_ref, ys_ref):
    _ret_first_step(decay_ref, st_ref, dm_ref, xi_ref, zeta_ref, True)
    for h in range(RH):
        sl = slice(h * RDV, (h + 1) * RDV)
        ys_ref[:, sl] = yf_ref[:, sl] + _ret_chunk(decay_ref, q_ref, k_ref, v_ref, st_ref,
                                                   dm_ref, xi_ref, zeta_ref, h)
    gnw = gnw_ref[...]

    def emit(rs, y, start):
        gt = g_ref[rs, :]
        for h in range(RH):
            sl = slice(h * RDV, (h + 1) * RDV)
            yh = y[:, sl]
            yc = yh - jnp.mean(yh, axis=-1, keepdims=True)
            var = jnp.mean(yc * yc, axis=-1, keepdims=True)
            yn = yc * lax.rsqrt(var + EPS) * gnw[:, sl]
            gh = gt[:, sl]
            o_ref[rs, sl] = (gh * jax.nn.sigmoid(gh) * yn).astype(o_ref.dtype)

    _norm_rows(ys_ref, RC, emit, unroll=8)


def _retention(p_qk, p_vg, decay_f, decay_b, gn_w):
    nch = M // RC
    nlat = S // RC
    qw, vw = RH * RDK, RH * RDV
    fwd = lambda s: ((s + nlat) % nch)
    bwd = lambda s: (nch - 1 - s)
    smem = pl.BlockSpec(memory_space=pltpu.SMEM)

    def specs(cm):
        return [smem,
                pl.BlockSpec((RC, qw), lambda s: (cm(s), 0)),
                pl.BlockSpec((RC, qw), lambda s: (cm(s), 1)),
                pl.BlockSpec((RC, vw), lambda s: (cm(s), 0))]

    scratch = [pltpu.VMEM((RH, RDK, RDV), F32), pltpu.VMEM((RH, RC, RC), F32),
               pltpu.VMEM((RH, RC, 1), F32), pltpu.VMEM((RH, RC, 1), F32)]
    yf = pl.pallas_call(
        _ret_fwd_kernel,
        grid=(nch,),
        in_specs=specs(fwd),
        out_specs=pl.BlockSpec((RC, vw), lambda s: (fwd(s), 0)),
        out_shape=jax.ShapeDtypeStruct((M, vw), F32),
        scratch_shapes=scratch,
        compiler_params=_cparams(("arbitrary",)),
        name="ret_fwd",
    )(decay_f, p_qk, p_qk, p_vg)
    return pl.pallas_call(
        _ret_bwd_kernel,
        grid=(nch,),
        in_specs=specs(bwd) + [pl.BlockSpec((RC, vw), lambda s: (bwd(s), 0)),
                               pl.BlockSpec((RC, vw), lambda s: (bwd(s), 1)),
                               pl.BlockSpec((1, vw), lambda s: (0, 0))],
        out_specs=pl.BlockSpec((RC, vw), lambda s: (bwd(s), 0)),
        out_shape=jax.ShapeDtypeStruct((M, vw), BF16),
        scratch_shapes=scratch + [pltpu.VMEM((RC, vw), F32)],
        compiler_params=_cparams(("arbitrary",)),
        name="ret_bwd",
    )(decay_b, p_qk, p_qk, p_vg, yf, p_vg, gn_w.reshape(1, vw))


_NA_NBLK = S // NQB
_NA_CASES = (0, 1, _NA_NBLK - 1)


def _na_window_row(b):
    return jnp.clip(NQROWS * b - NROWS // 2, 0, GROWS - NWIN_ROWS)


def _na_table_kernel(rpb_ref, o_ref):
    h = pl.program_id(0)
    n_dr, n_dc = 2 * NROWS - 1, 2 * NCOLS - 1
    cq = lax.broadcasted_iota(jnp.int32, (GW, 2 * GW), 0)
    lane = lax.broadcasted_iota(jnp.int32, (GW, 2 * GW), 1)
    right = lane >= GW
    kc = jnp.where(right, lane - GW, lane)
    dc = kc - cq + (NCOLS - 1)
    cs = jnp.clip(cq - NCOLS // 2, 0, GW - NCOLS)
    col_ok = (kc >= cs) & (kc < cs + NCOLS)
    memo = {}

    def row_scalar(a, b):
        return rpb_ref[(h * n_dr + a) * n_dc + b] if 0 <= a < n_dr else 0.0

    def pair(a0):
        if a0 not in memo:
            acc = jnp.zeros((GW, 2 * GW), F32)
            for b in range(n_dc):
                vec = jnp.where(right, row_scalar(a0 + 1, b), row_scalar(a0, b))
                acc = jnp.where(dc == b, vec, acc)
            memo[a0] = acc
        return memo[a0]

    for ci, blk in enumerate(_NA_CASES):
        w = int(np.clip(NQROWS * blk - NROWS // 2, 0, GROWS - NWIN_ROWS))
        for rl in range(NQROWS):
            r = NQROWS * blk + rl
            rs = int(np.clip(r - NROWS // 2, 0, GROWS - NROWS))
            for pr in range(NWIN_ROWS // 2):
                kr = w + 2 * pr
                ok_l = rs <= kr < rs + NROWS
                ok_r = rs <= kr + 1 < rs + NROWS
                if ok_l and ok_r:
                    ok = col_ok
                elif ok_l:
                    ok = col_ok & jnp.logical_not(right)
                elif ok_r:
                    ok = col_ok & right
                else:
                    ok = None
                if ok is None:
                    tile = jnp.full((GW, 2 * GW), NEG, F32)
                else:
                    tile = jnp.where(ok, pair(kr - r + NROWS - 1) * LOG2E, NEG)
                o_ref[ci, 0, rl * GW:(rl + 1) * GW, pr * 2 * GW:(pr + 1) * 2 * GW] = tile


def _na_tables(rpb):
    return pl.pallas_call(
        _na_table_kernel,
        grid=(NH,),
        in_specs=[pl.BlockSpec(memory_space=pltpu.SMEM)],
        out_specs=pl.BlockSpec((len(_NA_CASES), 1, NQB, NWIN), lambda h: (0, h, 0, 0)),
        out_shape=jax.ShapeDtypeStruct((len(_NA_CASES), NH, NQB, NWIN), F32),
        compiler_params=_cparams(("arbitrary",)),
        name="na_tables",
    )(rpb.reshape(-1))


def _na_kernel(q_ref, k_ref, v_ref, t_ref, o_ref):
    b = pl.program_id(1)
    scale2 = (HD ** -0.5) * LOG2E
    start = pl.multiple_of(_na_window_row(b) * GW, NQB)
    nt = (((1,), (1,)), ((), ()))
    for hh in range(NHPS):
        cs = slice(hh * HD, (hh + 1) * HD)
        q = q_ref[:, cs]
        s_loc = lax.dot_general(q, k_ref[pl.ds(start, NWIN), cs], nt,
                                preferred_element_type=F32) * scale2 + t_ref[0, hh]
        s_ctx = lax.dot_general(q, k_ref[S:M, cs], nt, preferred_element_type=F32) * scale2
        m = jnp.maximum(jnp.max(s_loc, axis=-1, keepdims=True), jnp.max(s_ctx, axis=-1, keepdims=True))
        e_loc = jnp.exp2(s_loc - m)
        e_ctx = jnp.exp2(s_ctx - m)
        den = jnp.sum(e_loc, axis=-1, keepdims=True) + jnp.sum(e_ctx, axis=-1, keepdims=True)
        o = (jnp.dot(e_loc.astype(BF16), v_ref[pl.ds(start, NWIN), cs], preferred_element_type=F32)
             + jnp.dot(e_ctx.astype(BF16), v_ref[S:M, cs], preferred_element_type=F32))
        o_ref[:, cs] = (o / den).astype(o_ref.dtype)


def _na(p, rpb):
    gw = NHPS * HD
    qcol = 0
    kcol = qcol + NH // NHPS
    vcol = kcol + NH // NHPS
    tables = _na_tables(rpb)

    def case(b):
        return jnp.minimum(b, 1) + jnp.maximum(b - (_NA_NBLK - 2), 0)

    return pl.pallas_call(
        _na_kernel,
        grid=(NH // NHPS, _NA_NBLK),
        in_specs=[
            pl.BlockSpec((NQB, gw), lambda h, b: (b, qcol + h)),
            pl.BlockSpec((M, gw), lambda h, b: (0, kcol + h)),
            pl.BlockSpec((M, gw), lambda h, b: (0, vcol + h)),
            pl.BlockSpec((1, NHPS, NQB, NWIN), lambda h, b: (case(b), h, 0, 0)),
        ],
        out_specs=pl.BlockSpec((NQB, gw), lambda h, b: (b, h)),
        out_shape=jax.ShapeDtypeStruct((S, NH * HD), BF16),
        compiler_params=_cparams(("arbitrary", "arbitrary")),
        name="na",
    )(p, p, p, tables)


def _tiles(rows):
    if rows == M:
        return dict(up=(2112, 1056), small=528, inp=1056)
    assert rows == S
    return dict(up=(2048, 1024), small=512, inp=1024)


def _ffn_half(h, hn, rows, mods, layer, third, wi, wo, nw_next, next_mod, cast_jobs=()):
    t = _tiles(rows)
    a, wob = _mm_swiglu(hn, wi, wo, layer, tm=t["up"][0], mm=t["up"][1], tn=512)
    return _ffn_down(h, rows, mods, layer, 3 * third + 2, a, wob, nw_next, next_mod, cast_jobs)


IN_TN = 1536


def kernel(x, c, ctx, c_ctx, ada_w, ada_b, norm_w, ffn_a_wi, ffn_a_wo, ffn_b_wi, ffn_b_wo,
           ev_w_in, ev_w_out, ev_sink, ev_conv_w, ev_conv_b,
           od_w_in, od_w_out, od_decay_f, od_decay_b, od_gn_w, od_rpb, final_norm_w):
    assert x.shape == (1, S, D) and ctx.shape == (1, LC, D) and ada_w.shape[0] == 2
    assert ffn_a_wi.shape == (2, D, 2 * DFF) and ev_w_in.shape == (1, D, EV_IN)
    assert od_w_in.shape == (1, D, OD_IN) and RDK == HD
    cvec = jnp.concatenate([c, c_ctx[None, :], jnp.zeros((6, D), F32)], axis=0)
    mods = _ada(cvec, ada_w, ada_b)
    tabs = _rope_tables()

    tm_in = _tiles(M)["inp"]
    hn = _normmod_first(x[0], ctx[0], norm_w[0, 0], mods)
    (h, hn), (w_in, w_out) = _ffn_half((x[0], ctx[0]), hn, M, mods, 0, 0, ffn_a_wi, ffn_a_wo,
                                       norm_w[0, 1], (0, 1), cast_jobs=((ev_w_in, 0), (ev_w_out, 0)))
    n_att = (AH + 2 * AKV) * HD
    p_att = _mm_rope(hn, w_in, n_att, tabs, AH + AKV, tm=tm_in)
    p_conv = _mm_plain(hn, w_in, n_att, EV_IN - n_att, tm=tm_in, tn=IN_TN, out_dtype=F32)
    att = _attn(p_att, ev_sink[0])
    cnv = _conv(p_conv, ev_conv_w[0], ev_conv_b[0])
    h, hn = _out_proj(h, M, mods, 0, att, cnv, w_out, norm_w[0, 2], tm=_tiles(M)["small"])
    (h, hn), _ = _ffn_half(h, hn, M, mods, 0, 2, ffn_b_wi, ffn_b_wo, norm_w[1, 0], (1, 0))

    (h, hn), (w_in, w_out) = _ffn_half(h, hn, M, mods, 1, 0, ffn_a_wi, ffn_a_wo, norm_w[1, 1], (1, 1),
                                       cast_jobs=((od_w_in, 0), (od_w_out, 0)))
    n_qk, n_vg = 2 * RH * RDK, 2 * RH * RDV
    p_qk = _mm_rope(hn, w_in, n_qk, tabs, 2 * RH, tm=tm_in, scaled=(RH, 2 * RH))
    p_vg = _mm_plain(hn, w_in, n_qk, n_vg, tm=tm_in, tn=n_qk, out_dtype=F32)
    p_na = _mm_plain(hn, w_in, n_qk + n_vg, OD_IN - n_qk - n_vg, tm=tm_in, tn=IN_TN, out_dtype=BF16)
    ret = _retention(p_qk, p_vg, od_decay_f[0], od_decay_b[0], od_gn_w[0])
    nat = _na(p_na, od_rpb[0])
    h, hn = _out_proj(h, S, mods, 1, ret, nat, w_out, norm_w[1, 2], tm=_tiles(S)["small"])
    (out,), _ = _ffn_half(h, hn, S, mods, 1, 2, ffn_b_wi, ffn_b_wo, final_norm_w, None)
    return out[None]
```

```python
import functools

import numpy as np
import jax
import jax.numpy as jnp
from jax import lax
from jax.experimental import pallas as pl
from jax.experimental.pallas import tpu as pltpu

F32 = jnp.float32
BF16 = jnp.bfloat16

D = 2048
S = 8192
LC = 256
M = S + LC
GW = 64
GROWS = S // GW
HD = 128
DFF = 5632
NMOD = 9
EPS = 1e-6
ROPE_BASE = 10000.0
NEG = -1e30

AH, AKV, AWIN = 8, 2, 128
AQB = 256
CCH = 1024
EV_IN = AH * HD + 2 * AKV * HD + 3 * CCH
RH, RDK, RDV = 4, 128, 256
RC = 256
NH, NROWS, NCOLS = 8, 8, 16
NQROWS = 4
NQB = NQROWS * GW
NWIN_ROWS = NQROWS + NROWS
NWIN = NWIN_ROWS * GW
NHPS = 4
LOG2E = float(np.log2(np.e))
OD_IN = 2 * RH * RDK + 2 * RH * RDV + 3 * NH * HD

VMEM_LIMIT = 56 * 1024 * 1024


def _cparams(sem):
    return pltpu.CompilerParams(dimension_semantics=sem, vmem_limit_bytes=VMEM_LIMIT)


def _sub(r, size):
    return pl.ds(pl.multiple_of(r * size, 16), size)


def _ada_kernel(c_ref, w_ref, b_ref, o_ref):
    cv = c_ref[...]
    a = cv * jax.nn.sigmoid(cv)
    acc = jnp.dot(a.astype(BF16), w_ref[0].astype(BF16), preferred_element_type=F32)
    o_ref[0, 0] = acc + b_ref[0]


def _ada(cvec, ada_w, ada_b):
    depth = ada_w.shape[0]
    tn = D
    per = D // tn
    return pl.pallas_call(
        _ada_kernel,
        grid=(depth, NMOD * per),
        in_specs=[
            pl.BlockSpec((8, D), lambda l, j: (0, 0)),
            pl.BlockSpec((1, D, tn), lambda l, j: (l, 0, j)),
            pl.BlockSpec((1, 1, tn), lambda l, j: (l, 0, j)),
        ],
        out_specs=pl.BlockSpec((1, 1, 8, tn), lambda l, j: (l, j // per, 0, j % per)),
        out_shape=jax.ShapeDtypeStruct((depth, NMOD, 8, D), F32),
        compiler_params=_cparams(("arbitrary", "arbitrary")),
        name="ada",
    )(cvec, ada_w, ada_b.reshape(depth, 1, NMOD * D))


NORM_TR = 1024
NORM_CHUNK = 16
NORM_UNROLL = 8


def _norm_rows(src_ref, n_rows, fn, unroll=NORM_UNROLL):
    group = NORM_CHUNK * unroll

    def outer(gi, carry):
        for u in range(unroll):
            start = pl.multiple_of(gi * group + u * NORM_CHUNK, NORM_CHUNK)
            rs = pl.ds(start, NORM_CHUNK)
            fn(rs, src_ref[rs, :], start)
        return carry

    lax.fori_loop(0, n_rows // group, outer, 0)


def _normmod_first_kernel(x_ref, c_ref, nw_ref, mod_ref, o_ref):
    i = pl.program_id(0)
    kind = (i * NORM_TR >= S).astype(jnp.int32)
    n_rows = jnp.minimum(M - i * NORM_TR, NORM_TR)
    shift = mod_ref[0, 0, pl.ds(kind, 1), :]
    gain = nw_ref[...] * (1.0 + mod_ref[0, 1, pl.ds(kind, 1), :])

    def emit(rs, x, start):
        rinv = lax.rsqrt(jnp.mean(x * x, axis=-1, keepdims=True) + EPS)
        o_ref[rs, :] = ((x * rinv) * gain + shift).astype(o_ref.dtype)

    @pl.when(kind == 0)
    def _():
        _norm_rows(x_ref, n_rows, emit)

    @pl.when(kind == 1)
    def _():
        _norm_rows(c_ref, n_rows, emit)


def _normmod_first(x, ctx, nw, mods):
    assert S % NORM_TR == 0 and LC % (NORM_CHUNK * NORM_UNROLL) == 0 and LC <= NORM_TR
    return pl.pallas_call(
        _normmod_first_kernel,
        grid=(pl.cdiv(M, NORM_TR),),
        in_specs=[pl.BlockSpec((NORM_TR, D), lambda i: (jnp.minimum(i, S // NORM_TR - 1), 0)),
                  pl.BlockSpec((LC, D), lambda i: (0, 0)),
                  pl.BlockSpec((1, D), lambda i: (0, 0)),
                  pl.BlockSpec((1, 3, 8, D), lambda i: (0, 0, 0, 0))],
        out_specs=pl.BlockSpec((NORM_TR, D), lambda i: (i, 0)),
        out_shape=jax.ShapeDtypeStruct((M, D), BF16),
        compiler_params=_cparams(("arbitrary",)),
        name="normmod_first",
    )(x, ctx, nw.reshape(1, D), mods)


def _cast_weights(w_refs, wb_refs):
    @pl.when(pl.program_id(1) == 0)
    def _():
        for w_ref, wb_ref in zip(w_refs, wb_refs):
            wb_ref[...] = w_ref[...].astype(BF16)


def _mm_swiglu_kernel(a_ref, wg_ref, wu_ref, wo_ref, o_ref, wob_ref, wgb_ref, wub_ref, *, tm, mm):
    _cast_weights((wg_ref, wu_ref), (wgb_ref, wub_ref))
    wob_ref[...] = wo_ref[...].astype(BF16)

    def body(r, carry):
        a = a_ref[_sub(r, mm), :]
        g = jnp.dot(a, wgb_ref[...], preferred_element_type=F32)
        u = jnp.dot(a, wub_ref[...], preferred_element_type=F32)
        o_ref[_sub(r, mm), :] = (g * jax.nn.sigmoid(g) * u).astype(o_ref.dtype)
        return carry

    lax.fori_loop(0, tm // mm, body, 0, unroll=True)


def _mm_plain_kernel(a_ref, w_ref, o_ref):
    o_ref[...] = jnp.dot(a_ref[...], w_ref[...], preferred_element_type=F32).astype(o_ref.dtype)


def _mm_swiglu(a, wi, wo, layer, *, tm, mm, tn):
    rows, k = a.shape
    assert rows % tm == 0 and tm % mm == 0 and mm % 16 == 0
    n_out = wi.shape[2] // 2
    nj, ni = n_out // tn, rows // tm
    assert wo.shape[1] % (nj * ni) == 0
    slab = wo.shape[1] // (nj * ni)
    assert slab % 16 == 0
    return pl.pallas_call(
        functools.partial(_mm_swiglu_kernel, tm=tm, mm=mm),
        grid=(nj, ni),
        in_specs=[pl.BlockSpec((tm, k), lambda j, i: (i, 0)),
                  pl.BlockSpec((None, k, tn), lambda j, i: (layer, 0, j)),
                  pl.BlockSpec((None, k, tn), lambda j, i: (layer, 0, j + nj)),
                  pl.BlockSpec((None, slab, D), lambda j, i: (layer, j * ni + i, 0))],
        out_specs=[pl.BlockSpec((tm, tn), lambda j, i: (i, j)),
                   pl.BlockSpec((slab, D), lambda j, i: (j * ni + i, 0))],
        out_shape=[jax.ShapeDtypeStruct((rows, n_out), BF16),
                   jax.ShapeDtypeStruct((wo.shape[1], D), BF16)],
        scratch_shapes=[pltpu.VMEM((k, tn), BF16)] * 2,
        compiler_params=_cparams(("arbitrary", "arbitrary")),
        name="mm_swiglu",
    )(a, wi, wi, wo)


def _mm_plain(a, wb, col0, ncols, *, tm, tn, out_dtype):
    rows, k = a.shape
    assert rows % tm == 0 and ncols % tn == 0 and col0 % tn == 0
    return pl.pallas_call(
        _mm_plain_kernel,
        grid=(ncols // tn, rows // tm),
        in_specs=[pl.BlockSpec((tm, k), lambda j, i: (i, 0)),
                  pl.BlockSpec((k, tn), lambda j, i: (0, col0 // tn + j))],
        out_specs=pl.BlockSpec((tm, tn), lambda j, i: (i, j)),
        out_shape=jax.ShapeDtypeStruct((rows, ncols), out_dtype),
        compiler_params=_cparams(("arbitrary", "arbitrary")),
        name="mm_plain",
    )(a, wb)


def _gate_rows(gate_ref, i, tm):
    rows = i * tm + lax.broadcasted_iota(jnp.int32, (tm, 1), 0)
    return jnp.where(rows >= S, gate_ref[0, 0, 1:2, :], gate_ref[0, 0, 0:1, :])


DOWN_TM = 256


def _ffn_down_kernel(*refs, n_jobs, final, split):
    it = iter(refs)
    h_ref = next(it)
    c_ref = next(it) if split else None
    gate_ref, a_ref, w_ref, nw_ref = next(it), next(it), next(it), next(it)
    mod_ref = None if final else next(it)
    job_in = [next(it) for _ in range(n_jobs)]
    o_ref = next(it)
    hn_ref = None if final else next(it)
    job_out = [next(it) for _ in range(n_jobs)]
    new_ref = next(it) if final else o_ref
    for src_ref, dst_ref in zip(job_in, job_out):
        dst_ref[...] = src_ref[...].astype(BF16)
    kind = (pl.program_id(0) * DOWN_TM >= S).astype(jnp.int32)
    acc = jnp.dot(a_ref[...], w_ref[...], preferred_element_type=F32)
    res = jnp.where(kind == 1, c_ref[...], h_ref[...]) if split else h_ref[...]
    new_ref[...] = res + (0.5 * gate_ref[0, 0, pl.ds(kind, 1), :]) * acc
    nw = nw_ref[...]
    if final:
        def emit(rs, x, start):
            rinv = lax.rsqrt(jnp.mean(x * x, axis=-1, keepdims=True) + EPS)
            o_ref[rs, :] = x * rinv * nw
    else:
        shift = mod_ref[0, 0, pl.ds(kind, 1), :]
        gain = nw * (1.0 + mod_ref[0, 1, pl.ds(kind, 1), :])

        def emit(rs, x, start):
            rinv = lax.rsqrt(jnp.mean(x * x, axis=-1, keepdims=True) + EPS)
            hn_ref[rs, :] = ((x * rinv) * gain + shift).astype(hn_ref.dtype)

    _norm_rows(new_ref, DOWN_TM, emit)


def _ffn_down(h, rows, mods, layer, gate_idx, a, wb, nw_next, next_mod=None, cast_jobs=()):
    k = a.shape[1]
    final = next_mod is None
    split = isinstance(h, tuple)
    tm = DOWN_TM
    assert rows % tm == 0 and S % tm == 0 and tm % (NORM_CHUNK * NORM_UNROLL) == 0
    assert not split or (rows == M and LC == tm)
    n_slabs = S // tm
    row = lambda i: (i, 0)
    job_in, job_out, job_shape = [], [], []
    for w, idx in cast_jobs:
        kk, nn = w.shape[1:]
        assert kk % n_slabs == 0 and (kk // n_slabs) % 16 == 0
        slab = kk // n_slabs
        job_in.append(pl.BlockSpec((None, slab, nn),
                                   lambda i, idx=idx: (idx, jnp.minimum(i, n_slabs - 1), 0)))
        job_out.append(pl.BlockSpec((slab, nn), lambda i: (jnp.minimum(i, n_slabs - 1), 0)))
        job_shape.append(jax.ShapeDtypeStruct((kk, nn), BF16))
    if split:
        h_specs = [pl.BlockSpec((tm, D), lambda i: (jnp.minimum(i, n_slabs - 1), 0)),
                   pl.BlockSpec((tm, D), lambda i: (0, 0))]
        h_args = list(h)
    else:
        h_specs, h_args = [pl.BlockSpec((tm, D), row)], [h]
    in_specs = h_specs + [pl.BlockSpec((1, 1, 8, D), lambda i: (layer, gate_idx, 0, 0)),
                          pl.BlockSpec((tm, k), row),
                          pl.BlockSpec((k, D), lambda i: (0, 0), pipeline_mode=pl.Buffered(1)),
                          pl.BlockSpec((1, D), lambda i: (0, 0))]
    args = h_args + [mods, a, wb, nw_next.reshape(1, D)]
    out_specs = [pl.BlockSpec((tm, D), row)]
    out_shape = [jax.ShapeDtypeStruct((rows, D), F32)]
    if not final:
        in_specs.append(pl.BlockSpec((1, 3, 8, D), lambda i: (next_mod[0], next_mod[1], 0, 0)))
        args.append(mods)
        out_specs.append(pl.BlockSpec((tm, D), row))
        out_shape.append(jax.ShapeDtypeStruct((rows, D), BF16))
    n_main = len(out_shape)
    res = pl.pallas_call(
        functools.partial(_ffn_down_kernel, n_jobs=len(cast_jobs), final=final, split=split),
        grid=(rows // tm,),
        in_specs=in_specs + job_in,
        out_specs=out_specs + job_out,
        out_shape=out_shape + job_shape,
        scratch_shapes=[pltpu.VMEM((tm, D), F32)] if final else [],
        compiler_params=_cparams(("arbitrary",)),
        name="ffn_down_final" if final else "ffn_down",
    )(*args, *[w for w, _ in cast_jobs])
    return tuple(res[:n_main]), list(res[n_main:])


def _out_proj_kernel(h_ref, gate_ref, a1_ref, a2_ref, w1_ref, w2_ref, nw_ref, mod_ref,
                     o_ref, hn_ref, *, tm, unroll, has_ctx):
    i = pl.program_id(0)
    acc = (jnp.dot(a1_ref[...], w1_ref[...], preferred_element_type=F32)
           + jnp.dot(a2_ref[...], w2_ref[...], preferred_element_type=F32))
    gate = _gate_rows(gate_ref, i, tm) if has_ctx else gate_ref[0, 0, 0:1, :]
    o_ref[...] = h_ref[...] + gate * acc
    nw = nw_ref[...]
    gain_lat = nw * (1.0 + mod_ref[0, 1, 0:1, :])
    gain_ctx = nw * (1.0 + mod_ref[0, 1, 1:2, :])
    shift_lat = mod_ref[0, 0, 0:1, :]
    shift_ctx = mod_ref[0, 0, 1:2, :]

    def emit(rs, x, start):
        rinv = lax.rsqrt(jnp.mean(x * x, axis=-1, keepdims=True) + EPS)
        gain, shift = gain_lat, shift_lat
        if has_ctx:
            isc = (i * tm + start + lax.broadcasted_iota(jnp.int32, (NORM_CHUNK, 1), 0)) >= S
            gain = jnp.where(isc, gain_ctx, gain_lat)
            shift = jnp.where(isc, shift_ctx, shift_lat)
        hn_ref[rs, :] = ((x * rinv) * gain + shift).astype(hn_ref.dtype)

    _norm_rows(o_ref, tm, emit, unroll=unroll)


def _out_proj(h, rows, mods, layer, a1, a2, wb, nw_next, *, tm):
    k = a1.shape[1]
    assert a2.shape[1] == k and wb.shape == (2 * k, D) and rows % tm == 0
    unroll = max(u for u in range(1, 13) if (tm // NORM_CHUNK) % u == 0)
    row = lambda i: (i, 0)
    return pl.pallas_call(
        functools.partial(_out_proj_kernel, tm=tm, unroll=unroll, has_ctx=rows > S),
        grid=(rows // tm,),
        in_specs=[pl.BlockSpec((tm, D), row),
                  pl.BlockSpec((1, 1, 8, D), lambda i: (layer, 5, 0, 0)),
                  pl.BlockSpec((tm, k), row),
                  pl.BlockSpec((tm, k), row),
                  pl.BlockSpec((k, D), lambda i: (0, 0)),
                  pl.BlockSpec((k, D), lambda i: (1, 0)),
                  pl.BlockSpec((1, D), lambda i: (0, 0)),
                  pl.BlockSpec((1, 3, 8, D), lambda i: (layer, 2, 0, 0))],
        out_specs=[pl.BlockSpec((tm, D), row), pl.BlockSpec((tm, D), row)],
        out_shape=[jax.ShapeDtypeStruct((rows, D), F32), jax.ShapeDtypeStruct((rows, D), BF16)],
        compiler_params=_cparams(("arbitrary",)),
        name="out_proj",
    )(h, mods, a1, a2, wb, wb, nw_next.reshape(1, D), mods)


def _rope_tables():
    t = np.arange(S)
    rows = (t // GW).astype(np.float32)
    cols = (t % GW).astype(np.float32)
    half = HD // 2
    inv = np.float32(ROPE_BASE) ** (-np.arange(0, half, 2, dtype=np.float32) / np.float32(half))
    ar = rows[:, None] * inv
    ac = cols[:, None] * inv
    cos = np.concatenate([np.cos(ar), np.cos(ar), np.cos(ac), np.cos(ac)], axis=-1)
    sin = np.concatenate([-np.sin(ar), np.sin(ar), -np.sin(ac), np.sin(ac)], axis=-1)
    lat = np.concatenate([cos, sin], axis=-1)
    ctx = np.concatenate([np.ones((LC, HD), np.float32), np.zeros((LC, HD), np.float32)], axis=-1)
    return jnp.asarray(np.concatenate([lat, ctx], axis=0).astype(np.float32))


def _rope(x, tab):
    c = tab[:, :HD]
    sg = tab[:, HD:]
    lane = lax.broadcasted_iota(jnp.int32, x.shape, 1)
    first = (lane % (HD // 2)) < (HD // 4)
    partner = jnp.where(first, pltpu.roll(x, HD - HD // 4, 1), pltpu.roll(x, HD // 4, 1))
    return x * c + partner * sg


def _mm_rope_kernel(a_ref, w_ref, tab_ref, o_ref, *, n_rot, scaled):
    acc = jnp.dot(a_ref[...], w_ref[...], preferred_element_type=F32)
    tab = tab_ref[...]
    for hh in range(acc.shape[1] // HD):
        x = acc[:, hh * HD:(hh + 1) * HD]
        if hh < n_rot:
            x = _rope(x, tab)
        if scaled[0] <= hh < scaled[1]:
            x = x * (HD ** -0.5)
        o_ref[:, hh * HD:(hh + 1) * HD] = x.astype(o_ref.dtype)


def _mm_rope(a, wb, ncols, tabs, n_rot, *, tm, scaled=(0, 0)):
    rows, k = a.shape
    assert rows % tm == 0 and ncols % HD == 0
    return pl.pallas_call(
        functools.partial(_mm_rope_kernel, n_rot=n_rot, scaled=scaled),
        grid=(rows // tm,),
        in_specs=[pl.BlockSpec((tm, k), lambda i: (i, 0)),
                  pl.BlockSpec((k, ncols), lambda i: (0, 0)),
                  pl.BlockSpec((tm, 2 * HD), lambda i: (i, 0))],
        out_specs=pl.BlockSpec((tm, ncols), lambda i: (i, 0)),
        out_shape=jax.ShapeDtypeStruct((rows, ncols), BF16),
        compiler_params=_cparams(("arbitrary",)),
        name="mm_rope",
    )(a, wb, tabs)


def _attn_kernel(sink_ref, q_ref, kp_ref, kc_ref, kn_ref, vp_ref, vc_ref, vn_ref, kx_ref, vx_ref, o_ref):
    g = pl.program_id(0)
    n = pl.program_id(1)
    nlat = S // AQB
    scale2 = (HD ** -0.5) * LOG2E
    k_all = jnp.concatenate([kp_ref[...], kc_ref[...], kn_ref[...], kx_ref[...]], axis=0)
    v_all = jnp.concatenate([vp_ref[...], vc_ref[...], vn_ref[...], vx_ref[...]], axis=0)
    nloc = AQB + 2 * AWIN
    nk = nloc + LC
    r = lax.broadcasted_iota(jnp.int32, (AQB, nk), 0)
    c = lax.broadcasted_iota(jnp.int32, (AQB, nk), 1)
    rel = c - AWIN - r
    pos = n * AQB - AWIN + c
    hi = jnp.where(n < nlat, S, 0)
    ok = ((jnp.abs(rel) <= AWIN) & (pos >= 0) & (pos < hi)) | (c >= nloc)
    for hh in range(AH // AKV):
        q = q_ref[:, hh * HD:(hh + 1) * HD]
        s = lax.dot_general(q, k_all, (((1,), (1,)), ((), ())), preferred_element_type=F32) * scale2
        s = jnp.where(ok, s, NEG)
        sk = sink_ref[g * (AH // AKV) + hh] * LOG2E
        m = jnp.maximum(jnp.max(s, axis=-1, keepdims=True), sk)
        e = jnp.exp2(s - m)
        den = jnp.sum(e, axis=-1, keepdims=True) + jnp.exp2(sk - m)
        o = jnp.dot(e.astype(BF16), v_all, preferred_element_type=F32)
        o_ref[:, hh * HD:(hh + 1) * HD] = (o / den).astype(o_ref.dtype)


def _attn(p, sink):
    assert AQB == LC and AQB == 2 * AWIN
    nlat = S // AQB
    qw = (AH // AKV) * HD
    kcol = AH
    vcol = kcol + AKV
    last_w = S // AWIN - 1

    def prev(n):
        return jnp.clip(2 * n - 1, 0, last_w)

    def cur(n):
        return jnp.minimum(n, nlat - 1)

    def nxt(n):
        return jnp.clip(2 * n + 2, 0, last_w)

    return pl.pallas_call(
        _attn_kernel,
        grid=(AKV, M // AQB),
        in_specs=[
            pl.BlockSpec(memory_space=pltpu.SMEM),
            pl.BlockSpec((AQB, qw), lambda g, n: (n, g)),
            pl.BlockSpec((AWIN, HD), lambda g, n: (prev(n), kcol + g)),
            pl.BlockSpec((AQB, HD), lambda g, n: (cur(n), kcol + g)),
            pl.BlockSpec((AWIN, HD), lambda g, n: (nxt(n), kcol + g)),
            pl.BlockSpec((AWIN, HD), lambda g, n: (prev(n), vcol + g)),
            pl.BlockSpec((AQB, HD), lambda g, n: (cur(n), vcol + g)),
            pl.BlockSpec((AWIN, HD), lambda g, n: (nxt(n), vcol + g)),
            pl.BlockSpec((LC, HD), lambda g, n: (S // LC, kcol + g)),
            pl.BlockSpec((LC, HD), lambda g, n: (S // LC, vcol + g)),
        ],
        out_specs=pl.BlockSpec((AQB, qw), lambda g, n: (n, g)),
        out_shape=jax.ShapeDtypeStruct((M, AH * HD), BF16),
        compiler_params=_cparams(("arbitrary", "arbitrary")),
        name="attn",
    )(sink, p, p, p, p, p, p, p, p, p)


def _conv_kernel(b_ref, c_ref, u_ref, cp_ref, up_ref, cn_ref, un_ref, w_ref, bias_ref, o_ref, *, tr):
    i = pl.program_id(0)
    lat_blocks = S // tr
    z = c_ref[...] * u_ref[...]
    zp = cp_ref[7:8, :] * up_ref[7:8, :]
    zn = cn_ref[0:1, :] * un_ref[0:1, :]
    has_prev = jnp.logical_and(i != 0, i != lat_blocks)
    has_next = jnp.logical_and(i != lat_blocks - 1, i != M // tr - 1)
    zp = jnp.where(has_prev, zp, 0.0)
    zn = jnp.where(has_next, zn, 0.0)
    row = lax.broadcasted_iota(jnp.int32, z.shape, 0)
    z_m1 = jnp.where(row == 0, zp, pltpu.roll(z, 1, 0))
    z_p1 = jnp.where(row == tr - 1, zn, pltpu.roll(z, tr - 1, 0))
    conv = z_m1 * w_ref[0:1, :] + z * w_ref[1:2, :] + z_p1 * w_ref[2:3, :] + bias_ref[...]
    o_ref[...] = (b_ref[...] * conv).astype(o_ref.dtype)


def _conv(p, conv_w, conv_b):
    tr = 256
    assert LC % tr == 0 and S % tr == 0
    h8 = tr // 8

    def prev8(i):
        return jnp.maximum(i * h8 - 1, 0)

    def next8(i):
        return jnp.minimum((i + 1) * h8, M // 8 - 1)

    return pl.pallas_call(
        functools.partial(_conv_kernel, tr=tr),
        grid=(M // tr,),
        in_specs=[
            pl.BlockSpec((tr, CCH), lambda i: (i, 0)),
            pl.BlockSpec((tr, CCH), lambda i: (i, 1)),
            pl.BlockSpec((tr, CCH), lambda i: (i, 2)),
            pl.BlockSpec((8, CCH), lambda i: (prev8(i), 1)),
            pl.BlockSpec((8, CCH), lambda i: (prev8(i), 2)),
            pl.BlockSpec((8, CCH), lambda i: (next8(i), 1)),
            pl.BlockSpec((8, CCH), lambda i: (next8(i), 2)),
            pl.BlockSpec((3, CCH), lambda i: (0, 0)),
            pl.BlockSpec((1, CCH), lambda i: (0, 0)),
        ],
        out_specs=pl.BlockSpec((tr, CCH), lambda i: (i, 0)),
        out_shape=jax.ShapeDtypeStruct((M, CCH), BF16),
        compiler_params=_cparams(("arbitrary",)),
        name="conv",
    )(p, p, p, p, p, p, p, conv_w, conv_b.reshape(1, CCH))


def _log_sigmoid(d):
    return jnp.minimum(d, 0.0) - jnp.log(1.0 + jnp.exp(-jnp.abs(d)))


def _ret_first_step(decay_ref, st_ref, dm_ref, xi_ref, zeta_ref, backward):
    @pl.when(pl.program_id(0) == 0)
    def _():
        st_ref[...] = jnp.zeros_like(st_ref)
        ii = lax.broadcasted_iota(jnp.int32, (RC, RC), 0)
        jj = lax.broadcasted_iota(jnp.int32, (RC, RC), 1)
        ic = lax.broadcasted_iota(jnp.int32, (RC, 1), 0).astype(F32)
        for h in range(RH):
            lg = _log_sigmoid(jnp.full((1, 1), decay_ref[h], F32))
            if backward:
                rel = jj - ii
                keep = rel > 0
                xi_ref[h] = jnp.exp(lg * (RC - ic))
                zeta_ref[h] = jnp.exp(lg * ic)
            else:
                rel = ii - jj
                keep = rel >= 0
                xi_ref[h] = jnp.exp(lg * (ic + 1.0))
                zeta_ref[h] = jnp.exp(lg * (RC - 1.0 - ic))
            dm_ref[h] = jnp.where(keep, jnp.exp(lg * jnp.where(keep, rel, 0).astype(F32)), 0.0)


def _ret_chunk(decay_ref, q_ref, k_ref, v_ref, st_ref, dm_ref, xi_ref, zeta_ref, h):
    lg = _log_sigmoid(jnp.full((1, 1), decay_ref[h], F32))
    q = q_ref[:, h * RDK:(h + 1) * RDK]
    k = k_ref[:, h * RDK:(h + 1) * RDK]
    v = v_ref[:, h * RDV:(h + 1) * RDV]
    inner = lax.dot_general(q, k, (((1,), (1,)), ((), ())), preferred_element_type=F32) * dm_ref[h]
    y = jnp.dot(inner.astype(BF16), v.astype(BF16), preferred_element_type=F32)
    st = st_ref[h]
    y = y + jnp.dot(q, st.astype(BF16), preferred_element_type=F32) * xi_ref[h]
    kv = lax.dot_general(k, (v * zeta_ref[h]).astype(BF16), (((0,), (0,)), ((), ())),
                         preferred_element_type=F32)
    st_ref[h] = jnp.exp(lg * float(RC)) * st + kv
    return y


def _ret_fwd_kernel(decay_ref, q_ref, k_ref, v_ref, y_ref, st_ref, dm_ref, xi_ref, zeta_ref):
    _ret_first_step(decay_ref, st_ref, dm_ref, xi_ref, zeta_ref, False)
    for h in range(RH):
        y_ref[:, h * RDV:(h + 1) * RDV] = _ret_chunk(decay_ref, q_ref, k_ref, v_ref, st_ref,
                                                     dm_ref, xi_ref, zeta_ref, h)


def _ret_bwd_kernel(decay_ref, q_ref, k_ref, v_ref, yf_ref, g_ref, gnw_ref, o_ref,
                    st_ref, dm_ref, xi_ref, zeta_ref, ys_ref):
    _ret_first_step(decay_ref, st_ref, dm_ref, xi_ref, zeta_ref, True)
    for h in range(RH):
        sl = slice(h * RDV, (h + 1) * RDV)
        ys_ref[:, sl] = yf_ref[:, sl] + _ret_chunk(decay_ref, q_ref, k_ref, v_ref, st_ref,
                                                   dm_ref, xi_ref, zeta_ref, h)
    gnw = gnw_ref[...]

    def emit(rs, y, start):
        gt = g_ref[rs, :]
        for h in range(RH):
            sl = slice(h * RDV, (h + 1) * RDV)
            yh = y[:, sl]
            yc = yh - jnp.mean(yh, axis=-1, keepdims=True)
            var = jnp.mean(yc * yc, axis=-1, keepdims=True)
            yn = yc * lax.rsqrt(var + EPS) * gnw[:, sl]
            gh = gt[:, sl]
            o_ref[rs, sl] = (gh * jax.nn.sigmoid(gh) * yn).astype(o_ref.dtype)

    _norm_rows(ys_ref, RC, emit, unroll=8)


def _retention(p_qk, p_vg, decay_f, decay_b, gn_w):
    nch = M // RC
    nlat = S // RC
    qw, vw = RH * RDK, RH * RDV
    fwd = lambda s: ((s + nlat) % nch)
    bwd = lambda s: (nch - 1 - s)
    smem = pl.BlockSpec(memory_space=pltpu.SMEM)

    def specs(cm):
        return [smem,
                pl.BlockSpec((RC, qw), lambda s: (cm(s), 0)),
                pl.BlockSpec((RC, qw), lambda s: (cm(s), 1)),
                pl.BlockSpec((RC, vw), lambda s: (cm(s), 0))]

    scratch = [pltpu.VMEM((RH, RDK, RDV), F32), pltpu.VMEM((RH, RC, RC), F32),
               pltpu.VMEM((RH, RC, 1), F32), pltpu.VMEM((RH, RC, 1), F32)]
    yf = pl.pallas_call(
        _ret_fwd_kernel,
        grid=(nch,),
        in_specs=specs(fwd),
        out_specs=pl.BlockSpec((RC, vw), lambda s: (fwd(s), 0)),
        out_shape=jax.ShapeDtypeStruct((M, vw), F32),
        scratch_shapes=scratch,
        compiler_params=_cparams(("arbitrary",)),
        name="ret_fwd",
    )(decay_f, p_qk, p_qk, p_vg)
    return pl.pallas_call(
        _ret_bwd_kernel,
        grid=(nch,),
        in_specs=specs(bwd) + [pl.BlockSpec((RC, vw), lambda s: (bwd(s), 0)),
                               pl.BlockSpec((RC, vw), lambda s: (bwd(s), 1)),
                               pl.BlockSpec((1, vw), lambda s: (0, 0))],
        out_specs=pl.BlockSpec((RC, vw), lambda s: (bwd(s), 0)),
        out_shape=jax.ShapeDtypeStruct((M, vw), BF16),
        scratch_shapes=scratch + [pltpu.VMEM((RC, vw), F32)],
        compiler_params=_cparams(("arbitrary",)),
        name="ret_bwd",
    )(decay_b, p_qk, p_qk, p_vg, yf, p_vg, gn_w.reshape(1, vw))


_NA_NBLK = S // NQB
_NA_CASES = (0, 1, _NA_NBLK - 1)


def _na_window_row(b):
    return jnp.clip(NQROWS * b - NROWS // 2, 0, GROWS - NWIN_ROWS)


def _na_table_kernel(rpb_ref, o_ref):
    h = pl.program_id(0)
    n_dr, n_dc = 2 * NROWS - 1, 2 * NCOLS - 1
    cq = lax.broadcasted_iota(jnp.int32, (GW, 2 * GW), 0)
    lane = lax.broadcasted_iota(jnp.int32, (GW, 2 * GW), 1)
    right = lane >= GW
    kc = jnp.where(right, lane - GW, lane)
    dc = kc - cq + (NCOLS - 1)
    cs = jnp.clip(cq - NCOLS // 2, 0, GW - NCOLS)
    col_ok = (kc >= cs) & (kc < cs + NCOLS)
    memo = {}

    def row_scalar(a, b):
        return rpb_ref[(h * n_dr + a) * n_dc + b] if 0 <= a < n_dr else 0.0

    def pair(a0):
        if a0 not in memo:
            acc = jnp.zeros((GW, 2 * GW), F32)
            for b in range(n_dc):
                vec = jnp.where(right, row_scalar(a0 + 1, b), row_scalar(a0, b))
                acc = jnp.where(dc == b, vec, acc)
            memo[a0] = acc
        return memo[a0]

    for ci, blk in enumerate(_NA_CASES):
        w = int(np.clip(NQROWS * blk - NROWS // 2, 0, GROWS - NWIN_ROWS))
        for rl in range(NQROWS):
            r = NQROWS * blk + rl
            rs = int(np.clip(r - NROWS // 2, 0, GROWS - NROWS))
            for pr in range(NWIN_ROWS // 2):
                kr = w + 2 * pr
                ok_l = rs <= kr < rs + NROWS
                ok_r = rs <= kr + 1 < rs + NROWS
                if ok_l and ok_r:
                    ok = col_ok
                elif ok_l:
                    ok = col_ok & jnp.logical_not(right)
                elif ok_r:
                    ok = col_ok & right
                else:
                    ok = None
                if ok is None:
                    tile = jnp.full((GW, 2 * GW), NEG, F32)
                else:
                    tile = jnp.where(ok, pair(kr - r + NROWS - 1) * LOG2E, NEG)
                o_ref[ci, 0, rl * GW:(rl + 1) * GW, pr * 2 * GW:(pr + 1) * 2 * GW] = tile


def _na_tables(rpb):
    return pl.pallas_call(
        _na_table_kernel,
        grid=(NH,),
        in_specs=[pl.BlockSpec(memory_space=pltpu.SMEM)],
        out_specs=pl.BlockSpec((len(_NA_CASES), 1, NQB, NWIN), lambda h: (0, h, 0, 0)),
        out_shape=jax.ShapeDtypeStruct((len(_NA_CASES), NH, NQB, NWIN), F32),
        compiler_params=_cparams(("arbitrary",)),
        name="na_tables",
    )(rpb.reshape(-1))


def _na_kernel(q_ref, k_ref, v_ref, t_ref, o_ref):
    b = pl.program_id(1)
    scale2 = (HD ** -0.5) * LOG2E
    start = pl.multiple_of(_na_window_row(b) * GW, NQB)
    nt = (((1,), (1,)), ((), ()))
    for hh in range(NHPS):
        cs = slice(hh * HD, (hh + 1) * HD)
        q = q_ref[:, cs]
        s_loc = lax.dot_general(q, k_ref[pl.ds(start, NWIN), cs], nt,
                                preferred_element_type=F32) * scale2 + t_ref[0, hh]
        s_ctx = lax.dot_general(q, k_ref[S:M, cs], nt, preferred_element_type=F32) * scale2
        m = jnp.maximum(jnp.max(s_loc, axis=-1, keepdims=True), jnp.max(s_ctx, axis=-1, keepdims=True))
        e_loc = jnp.exp2(s_loc - m)
        e_ctx = jnp.exp2(s_ctx - m)
        den = jnp.sum(e_loc, axis=-1, keepdims=True) + jnp.sum(e_ctx, axis=-1, keepdims=True)
        o = (jnp.dot(e_loc.astype(BF16), v_ref[pl.ds(start, NWIN), cs], preferred_element_type=F32)
             + jnp.dot(e_ctx.astype(BF16), v_ref[S:M, cs], preferred_element_type=F32))
        o_ref[:, cs] = (o / den).astype(o_ref.dtype)


def _na(p, rpb):
    gw = NHPS * HD
    qcol = 0
    kcol = qcol + NH // NHPS
    vcol = kcol + NH // NHPS
    tables = _na_tables(rpb)

    def case(b):
        return jnp.minimum(b, 1) + jnp.maximum(b - (_NA_NBLK - 2), 0)

    return pl.pallas_call(
        _na_kernel,
        grid=(NH // NHPS, _NA_NBLK),
        in_specs=[
            pl.BlockSpec((NQB, gw), lambda h, b: (b, qcol + h)),
            pl.BlockSpec((M, gw), lambda h, b: (0, kcol + h)),
            pl.BlockSpec((M, gw), lambda h, b: (0, vcol + h)),
            pl.BlockSpec((1, NHPS, NQB, NWIN), lambda h, b: (case(b), h, 0, 0)),
        ],
        out_specs=pl.BlockSpec((NQB, gw), lambda h, b: (b, h)),
        out_shape=jax.ShapeDtypeStruct((S, NH * HD), BF16),
        compiler_params=_cparams(("arbitrary", "arbitrary")),
        name="na",
    )(p, p, p, tables)


def _tiles(rows):
    if rows == M:
        return dict(up=(2112, 704), small=528, inp=1056)
    assert rows == S
    return dict(up=(2048, 1024), small=512, inp=1024)


def _ffn_half(h, hn, rows, mods, layer, third, wi, wo, nw_next, next_mod, cast_jobs=()):
    t = _tiles(rows)
    a, wob = _mm_swiglu(hn, wi, wo, layer, tm=t["up"][0], mm=t["up"][1], tn=512)
    return _ffn_down(h, rows, mods, layer, 3 * third + 2, a, wob, nw_next, next_mod, cast_jobs)


IN_TN = 1536


def kernel(x, c, ctx, c_ctx, ada_w, ada_b, norm_w, ffn_a_wi, ffn_a_wo, ffn_b_wi, ffn_b_wo,
           ev_w_in, ev_w_out, ev_sink, ev_conv_w, ev_conv_b,
           od_w_in, od_w_out, od_decay_f, od_decay_b, od_gn_w, od_rpb, final_norm_w):
    assert x.shape == (1, S, D) and ctx.shape == (1, LC, D) and ada_w.shape[0] == 2
    assert ffn_a_wi.shape == (2, D, 2 * DFF) and ev_w_in.shape == (1, D, EV_IN)
    assert od_w_in.shape == (1, D, OD_IN) and RDK == HD
    cvec = jnp.concatenate([c, c_ctx[None, :], jnp.zeros((6, D), F32)], axis=0)
    mods = _ada(cvec, ada_w, ada_b)
    tabs = _rope_tables()

    tm_in = _tiles(M)["inp"]
    hn = _normmod_first(x[0], ctx[0], norm_w[0, 0], mods)
    (h, hn), (w_in, w_out) = _ffn_half((x[0], ctx[0]), hn, M, mods, 0, 0, ffn_a_wi, ffn_a_wo,
                                       norm_w[0, 1], (0, 1), cast_jobs=((ev_w_in, 0), (ev_w_out, 0)))
    n_att = (AH + 2 * AKV) * HD
    p_att = _mm_rope(hn, w_in, n_att, tabs, AH + AKV, tm=tm_in)
    p_conv = _mm_plain(hn, w_in, n_att, EV_IN - n_att, tm=tm_in, tn=IN_TN, out_dtype=F32)
    att = _attn(p_att, ev_sink[0])
    cnv = _conv(p_conv, ev_conv_w[0], ev_conv_b[0])
    h, hn = _out_proj(h, M, mods, 0, att, cnv, w_out, norm_w[0, 2], tm=_tiles(M)["small"])
    (h, hn), _ = _ffn_half(h, hn, M, mods, 0, 2, ffn_b_wi, ffn_b_wo, norm_w[1, 0], (1, 0))

    (h, hn), (w_in, w_out) = _ffn_half(h, hn, M, mods, 1, 0, ffn_a_wi, ffn_a_wo, norm_w[1, 1], (1, 1),
                                       cast_jobs=((od_w_in, 0), (od_w_out, 0)))
    n_qk, n_vg = 2 * RH * RDK, 2 * RH * RDV
    p_qk = _mm_rope(hn, w_in, n_qk, tabs, 2 * RH, tm=tm_in, scaled=(RH, 2 * RH))
    p_vg = _mm_plain(hn, w_in, n_qk, n_vg, tm=tm_in, tn=n_qk, out_dtype=F32)
    n_na = OD_IN - n_qk - n_vg
    p_na = _mm_plain(hn, w_in, n_qk + n_vg, n_na, tm=tm_in, tn=n_na, out_dtype=BF16)
    ret = _retention(p_qk, p_vg, od_decay_f[0], od_decay_b[0], od_gn_w[0])
    nat = _na(p_na, od_rpb[0])
    h, hn = _out_proj(h, S, mods, 1, ret, nat, w_out, norm_w[1, 2], tm=_tiles(S)["small"])
    (out,), _ = _ffn_half(h, hn, S, mods, 1, 2, ffn_b_wi, ffn_b_wo, final_norm_w, None)
    return out[None]
```

```python
import functools

import numpy as np
import jax
import jax.numpy as jnp
from jax import lax
from jax.experimental import pallas as pl
from jax.experimental.pallas import tpu as pltpu

F32 = jnp.float32
BF16 = jnp.bfloat16

D = 2048
S = 8192
LC = 256
M = S + LC
GW = 64
GROWS = S // GW
HD = 128
DFF = 5632
NMOD = 9
EPS = 1e-6
ROPE_BASE = 10000.0
NEG = -1e30

AH, AKV, AWIN = 8, 2, 128
AQB = 256
CCH = 1024
EV_IN = AH * HD + 2 * AKV * HD + 3 * CCH
RH, RDK, RDV = 4, 128, 256
RC = 256
NH, NROWS, NCOLS = 8, 8, 16
NQROWS = 4
NQB = NQROWS * GW
NWIN_ROWS = NQROWS + NROWS
NWIN = NWIN_ROWS * GW
NHPS = 4
LOG2E = float(np.log2(np.e))
OD_IN = 2 * RH * RDK + 2 * RH * RDV + 3 * NH * HD

VMEM_LIMIT = 56 * 1024 * 1024


def _cparams(sem):
    return pltpu.CompilerParams(dimension_semantics=sem, vmem_limit_bytes=VMEM_LIMIT)


def _sub(r, size):
    return pl.ds(pl.multiple_of(r * size, 16), size)


def _ada_kernel(c_ref, w_ref, b_ref, o_ref):
    cv = c_ref[...]
    a = cv * jax.nn.sigmoid(cv)
    acc = jnp.dot(a.astype(BF16), w_ref[0].astype(BF16), preferred_element_type=F32)
    o_ref[0, 0] = acc + b_ref[0]


def _ada(cvec, ada_w, ada_b):
    depth = ada_w.shape[0]
    tn = D
    per = D // tn
    return pl.pallas_call(
        _ada_kernel,
        grid=(depth, NMOD * per),
        in_specs=[
            pl.BlockSpec((8, D), lambda l, j: (0, 0)),
            pl.BlockSpec((1, D, tn), lambda l, j: (l, 0, j)),
            pl.BlockSpec((1, 1, tn), lambda l, j: (l, 0, j)),
        ],
        out_specs=pl.BlockSpec((1, 1, 8, tn), lambda l, j: (l, j // per, 0, j % per)),
        out_shape=jax.ShapeDtypeStruct((depth, NMOD, 8, D), F32),
        compiler_params=_cparams(("arbitrary", "arbitrary")),
        name="ada",
    )(cvec, ada_w, ada_b.reshape(depth, 1, NMOD * D))


NORM_TR = 1024
NORM_CHUNK = 16
NORM_UNROLL = 8


def _norm_rows(src_ref, n_rows, fn, unroll=NORM_UNROLL):
    group = NORM_CHUNK * unroll

    def outer(gi, carry):
        for u in range(unroll):
            start = pl.multiple_of(gi * group + u * NORM_CHUNK, NORM_CHUNK)
            rs = pl.ds(start, NORM_CHUNK)
            fn(rs, src_ref[rs, :], start)
        return carry

    lax.fori_loop(0, n_rows // group, outer, 0)


def _normmod_first_kernel(x_ref, c_ref, nw_ref, mod_ref, o_ref):
    i = pl.program_id(0)
    kind = (i * NORM_TR >= S).astype(jnp.int32)
    n_rows = jnp.minimum(M - i * NORM_TR, NORM_TR)
    shift = mod_ref[0, 0, pl.ds(kind, 1), :]
    gain = nw_ref[...] * (1.0 + mod_ref[0, 1, pl.ds(kind, 1), :])

    def emit(rs, x, start):
        rinv = lax.rsqrt(jnp.mean(x * x, axis=-1, keepdims=True) + EPS)
        o_ref[rs, :] = ((x * rinv) * gain + shift).astype(o_ref.dtype)

    @pl.when(kind == 0)
    def _():
        _norm_rows(x_ref, n_rows, emit)

    @pl.when(kind == 1)
    def _():
        _norm_rows(c_ref, n_rows, emit)


def _normmod_first(x, ctx, nw, mods):
    assert S % NORM_TR == 0 and LC % (NORM_CHUNK * NORM_UNROLL) == 0 and LC <= NORM_TR
    return pl.pallas_call(
        _normmod_first_kernel,
        grid=(pl.cdiv(M, NORM_TR),),
        in_specs=[pl.BlockSpec((NORM_TR, D), lambda i: (jnp.minimum(i, S // NORM_TR - 1), 0)),
                  pl.BlockSpec((LC, D), lambda i: (0, 0)),
                  pl.BlockSpec((1, D), lambda i: (0, 0)),
                  pl.BlockSpec((1, 3, 8, D), lambda i: (0, 0, 0, 0))],
        out_specs=pl.BlockSpec((NORM_TR, D), lambda i: (i, 0)),
        out_shape=jax.ShapeDtypeStruct((M, D), BF16),
        compiler_params=_cparams(("arbitrary",)),
        name="normmod_first",
    )(x, ctx, nw.reshape(1, D), mods)


def _cast_weights(w_refs, wb_refs):
    @pl.when(pl.program_id(1) == 0)
    def _():
        for w_ref, wb_ref in zip(w_refs, wb_refs):
            wb_ref[...] = w_ref[...].astype(BF16)


def _mm_swiglu_kernel(a_ref, wg_ref, wu_ref, wo_ref, o_ref, wob_ref, wgb_ref, wub_ref, *, tm, mm):
    _cast_weights((wg_ref, wu_ref), (wgb_ref, wub_ref))
    wob_ref[...] = wo_ref[...].astype(BF16)

    def body(r, carry):
        a = a_ref[_sub(r, mm), :]
        g = jnp.dot(a, wgb_ref[...], preferred_element_type=F32)
        u = jnp.dot(a, wub_ref[...], preferred_element_type=F32)
        o_ref[_sub(r, mm), :] = (g * jax.nn.sigmoid(g) * u).astype(o_ref.dtype)
        return carry

    lax.fori_loop(0, tm // mm, body, 0, unroll=True)


def _mm_plain_kernel(a_ref, w_ref, o_ref):
    o_ref[...] = jnp.dot(a_ref[...], w_ref[...], preferred_element_type=F32).astype(o_ref.dtype)


def _mm_swiglu(a, wi, wo, layer, *, tm, mm, tn):
    rows, k = a.shape
    assert rows % tm == 0 and tm % mm == 0 and mm % 16 == 0
    n_out = wi.shape[2] // 2
    nj, ni = n_out // tn, rows // tm
    assert wo.shape[1] % (nj * ni) == 0
    slab = wo.shape[1] // (nj * ni)
    assert slab % 16 == 0
    return pl.pallas_call(
        functools.partial(_mm_swiglu_kernel, tm=tm, mm=mm),
        grid=(nj, ni),
        in_specs=[pl.BlockSpec((tm, k), lambda j, i: (i, 0)),
                  pl.BlockSpec((None, k, tn), lambda j, i: (layer, 0, j)),
                  pl.BlockSpec((None, k, tn), lambda j, i: (layer, 0, j + nj)),
                  pl.BlockSpec((None, slab, D), lambda j, i: (layer, j * ni + i, 0))],
        out_specs=[pl.BlockSpec((tm, tn), lambda j, i: (i, j)),
                   pl.BlockSpec((slab, D), lambda j, i: (j * ni + i, 0))],
        out_shape=[jax.ShapeDtypeStruct((rows, n_out), BF16),
                   jax.ShapeDtypeStruct((wo.shape[1], D), BF16)],
        scratch_shapes=[pltpu.VMEM((k, tn), BF16)] * 2,
        compiler_params=_cparams(("arbitrary", "arbitrary")),
        name="mm_swiglu",
    )(a, wi, wi, wo)


def _mm_plain(a, wb, col0, ncols, *, tm, tn, out_dtype):
    rows, k = a.shape
    assert rows % tm == 0 and ncols % tn == 0 and col0 % tn == 0
    return pl.pallas_call(
        _mm_plain_kernel,
        grid=(ncols // tn, rows // tm),
        in_specs=[pl.BlockSpec((tm, k), lambda j, i: (i, 0)),
                  pl.BlockSpec((k, tn), lambda j, i: (0, col0 // tn + j))],
        out_specs=pl.BlockSpec((tm, tn), lambda j, i: (i, j)),
        out_shape=jax.ShapeDtypeStruct((rows, ncols), out_dtype),
        compiler_params=_cparams(("arbitrary", "arbitrary")),
        name="mm_plain",
    )(a, wb)


def _gate_rows(gate_ref, i, tm):
    rows = i * tm + lax.broadcasted_iota(jnp.int32, (tm, 1), 0)
    return jnp.where(rows >= S, gate_ref[0, 0, 1:2, :], gate_ref[0, 0, 0:1, :])


DOWN_TM = 256


def _ffn_down_kernel(*refs, n_jobs, final, split):
    it = iter(refs)
    h_ref = next(it)
    c_ref = next(it) if split else None
    gate_ref, a_ref, w_ref, nw_ref = next(it), next(it), next(it), next(it)
    mod_ref = None if final else next(it)
    job_in = [next(it) for _ in range(n_jobs)]
    o_ref = next(it)
    hn_ref = None if final else next(it)
    job_out = [next(it) for _ in range(n_jobs)]
    new_ref = next(it) if final else o_ref
    for src_ref, dst_ref in zip(job_in, job_out):
        dst_ref[...] = src_ref[...].astype(BF16)
    kind = (pl.program_id(0) * DOWN_TM >= S).astype(jnp.int32)
    acc = jnp.dot(a_ref[...], w_ref[...], preferred_element_type=F32)
    res = jnp.where(kind == 1, c_ref[...], h_ref[...]) if split else h_ref[...]
    new_ref[...] = res + (0.5 * gate_ref[0, 0, pl.ds(kind, 1), :]) * acc
    nw = nw_ref[...]
    if final:
        def emit(rs, x, start):
            rinv = lax.rsqrt(jnp.mean(x * x, axis=-1, keepdims=True) + EPS)
            o_ref[rs, :] = x * rinv * nw
    else:
        shift = mod_ref[0, 0, pl.ds(kind, 1), :]
        gain = nw * (1.0 + mod_ref[0, 1, pl.ds(kind, 1), :])

        def emit(rs, x, start):
            rinv = lax.rsqrt(jnp.mean(x * x, axis=-1, keepdims=True) + EPS)
            hn_ref[rs, :] = ((x * rinv) * gain + shift).astype(hn_ref.dtype)

    _norm_rows(new_ref, DOWN_TM, emit)


def _ffn_down(h, rows, mods, layer, gate_idx, a, wb, nw_next, next_mod=None, cast_jobs=()):
    k = a.shape[1]
    final = next_mod is None
    split = isinstance(h, tuple)
    tm = DOWN_TM
    assert rows % tm == 0 and S % tm == 0 and tm % (NORM_CHUNK * NORM_UNROLL) == 0
    assert not split or (rows == M and LC == tm)
    n_slabs = S // tm
    row = lambda i: (i, 0)
    job_in, job_out, job_shape = [], [], []
    for w, idx in cast_jobs:
        kk, nn = w.shape[1:]
        assert kk % n_slabs == 0 and (kk // n_slabs) % 16 == 0
        slab = kk // n_slabs
        job_in.append(pl.BlockSpec((None, slab, nn),
                                   lambda i, idx=idx: (idx, jnp.minimum(i, n_slabs - 1), 0)))
        job_out.append(pl.BlockSpec((slab, nn), lambda i: (jnp.minimum(i, n_slabs - 1), 0)))
        job_shape.append(jax.ShapeDtypeStruct((kk, nn), BF16))
    if split:
        h_specs = [pl.BlockSpec((tm, D), lambda i: (jnp.minimum(i, n_slabs - 1), 0)),
                   pl.BlockSpec((tm, D), lambda i: (0, 0))]
        h_args = list(h)
    else:
        h_specs, h_args = [pl.BlockSpec((tm, D), row)], [h]
    in_specs = h_specs + [pl.BlockSpec((1, 1, 8, D), lambda i: (layer, gate_idx, 0, 0)),
                          pl.BlockSpec((tm, k), row),
                          pl.BlockSpec((k, D), lambda i: (0, 0), pipeline_mode=pl.Buffered(1)),
                          pl.BlockSpec((1, D), lambda i: (0, 0))]
    args = h_args + [mods, a, wb, nw_next.reshape(1, D)]
    out_specs = [pl.BlockSpec((tm, D), row)]
    out_shape = [jax.ShapeDtypeStruct((rows, D), F32)]
    if not final:
        in_specs.append(pl.BlockSpec((1, 3, 8, D), lambda i: (next_mod[0], next_mod[1], 0, 0)))
        args.append(mods)
        out_specs.append(pl.BlockSpec((tm, D), row))
        out_shape.append(jax.ShapeDtypeStruct((rows, D), BF16))
    n_main = len(out_shape)
    res = pl.pallas_call(
        functools.partial(_ffn_down_kernel, n_jobs=len(cast_jobs), final=final, split=split),
        grid=(rows // tm,),
        in_specs=in_specs + job_in,
        out_specs=out_specs + job_out,
        out_shape=out_shape + job_shape,
        scratch_shapes=[pltpu.VMEM((tm, D), F32)] if final else [],
        compiler_params=_cparams(("arbitrary",)),
        name="ffn_down_final" if final else "ffn_down",
    )(*args, *[w for w, _ in cast_jobs])
    return tuple(res[:n_main]), list(res[n_main:])


def _out_proj_kernel(h_ref, gate_ref, a1_ref, a2_ref, w1_ref, w2_ref, nw_ref, mod_ref,
                     o_ref, hn_ref, *, tm, unroll, has_ctx):
    i = pl.program_id(0)
    acc = (jnp.dot(a1_ref[...], w1_ref[...], preferred_element_type=F32)
           + jnp.dot(a2_ref[...], w2_ref[...], preferred_element_type=F32))
    gate = _gate_rows(gate_ref, i, tm) if has_ctx else gate_ref[0, 0, 0:1, :]
    o_ref[...] = h_ref[...] + gate * acc
    nw = nw_ref[...]
    gain_lat = nw * (1.0 + mod_ref[0, 1, 0:1, :])
    gain_ctx = nw * (1.0 + mod_ref[0, 1, 1:2, :])
    shift_lat = mod_ref[0, 0, 0:1, :]
    shift_ctx = mod_ref[0, 0, 1:2, :]

    def emit(rs, x, start):
        rinv = lax.rsqrt(jnp.mean(x * x, axis=-1, keepdims=True) + EPS)
        gain, shift = gain_lat, shift_lat
        if has_ctx:
            isc = (i * tm + start + lax.broadcasted_iota(jnp.int32, (NORM_CHUNK, 1), 0)) >= S
            gain = jnp.where(isc, gain_ctx, gain_lat)
            shift = jnp.where(isc, shift_ctx, shift_lat)
        hn_ref[rs, :] = ((x * rinv) * gain + shift).astype(hn_ref.dtype)

    _norm_rows(o_ref, tm, emit, unroll=unroll)


def _out_proj(h, rows, mods, layer, a1, a2, wb, nw_next, *, tm):
    k = a1.shape[1]
    assert a2.shape[1] == k and wb.shape == (2 * k, D) and rows % tm == 0
    unroll = max(u for u in range(1, 13) if (tm // NORM_CHUNK) % u == 0)
    row = lambda i: (i, 0)
    return pl.pallas_call(
        functools.partial(_out_proj_kernel, tm=tm, unroll=unroll, has_ctx=rows > S),
        grid=(rows // tm,),
        in_specs=[pl.BlockSpec((tm, D), row),
                  pl.BlockSpec((1, 1, 8, D), lambda i: (layer, 5, 0, 0)),
                  pl.BlockSpec((tm, k), row),
                  pl.BlockSpec((tm, k), row),
                  pl.BlockSpec((k, D), lambda i: (0, 0)),
                  pl.BlockSpec((k, D), lambda i: (1, 0)),
                  pl.BlockSpec((1, D), lambda i: (0, 0)),
                  pl.BlockSpec((1, 3, 8, D), lambda i: (layer, 2, 0, 0))],
        out_specs=[pl.BlockSpec((tm, D), row), pl.BlockSpec((tm, D), row)],
        out_shape=[jax.ShapeDtypeStruct((rows, D), F32), jax.ShapeDtypeStruct((rows, D), BF16)],
        compiler_params=_cparams(("arbitrary",)),
        name="out_proj",
    )(h, mods, a1, a2, wb, wb, nw_next.reshape(1, D), mods)


def _rope_tables():
    t = np.arange(S)
    rows = (t // GW).astype(np.float32)
    cols = (t % GW).astype(np.float32)
    half = HD // 2
    inv = np.float32(ROPE_BASE) ** (-np.arange(0, half, 2, dtype=np.float32) / np.float32(half))
    ar = rows[:, None] * inv
    ac = cols[:, None] * inv
    cos = np.concatenate([np.cos(ar), np.cos(ar), np.cos(ac), np.cos(ac)], axis=-1)
    sin = np.concatenate([-np.sin(ar), np.sin(ar), -np.sin(ac), np.sin(ac)], axis=-1)
    lat = np.concatenate([cos, sin], axis=-1)
    ctx = np.concatenate([np.ones((LC, HD), np.float32), np.zeros((LC, HD), np.float32)], axis=-1)
    return jnp.asarray(np.concatenate([lat, ctx], axis=0).astype(np.float32))


def _rope(x, tab):
    c = tab[:, :HD]
    sg = tab[:, HD:]
    lane = lax.broadcasted_iota(jnp.int32, x.shape, 1)
    first = (lane % (HD // 2)) < (HD // 4)
    partner = jnp.where(first, pltpu.roll(x, HD - HD // 4, 1), pltpu.roll(x, HD // 4, 1))
    return x * c + partner * sg


def _mm_rope_kernel(a_ref, w_ref, tab_ref, o_ref, *, n_rot, scaled):
    acc = jnp.dot(a_ref[...], w_ref[...], preferred_element_type=F32)
    tab = tab_ref[...]
    for hh in range(acc.shape[1] // HD):
        x = acc[:, hh * HD:(hh + 1) * HD]
        if hh < n_rot:
            x = _rope(x, tab)
        if scaled[0] <= hh < scaled[1]:
            x = x * (HD ** -0.5)
        o_ref[:, hh * HD:(hh + 1) * HD] = x.astype(o_ref.dtype)


def _mm_rope(a, wb, ncols, tabs, n_rot, *, tm, scaled=(0, 0)):
    rows, k = a.shape
    assert rows % tm == 0 and ncols % HD == 0
    return pl.pallas_call(
        functools.partial(_mm_rope_kernel, n_rot=n_rot, scaled=scaled),
        grid=(rows // tm,),
        in_specs=[pl.BlockSpec((tm, k), lambda i: (i, 0)),
                  pl.BlockSpec((k, ncols), lambda i: (0, 0)),
                  pl.BlockSpec((tm, 2 * HD), lambda i: (i, 0))],
        out_specs=pl.BlockSpec((tm, ncols), lambda i: (i, 0)),
        out_shape=jax.ShapeDtypeStruct((rows, ncols), BF16),
        compiler_params=_cparams(("arbitrary",)),
        name="mm_rope",
    )(a, wb, tabs)


def _attn_kernel(sink_ref, q_ref, kp_ref, kc_ref, kn_ref, vp_ref, vc_ref, vn_ref, kx_ref, vx_ref, o_ref):
    n = pl.program_id(0)
    nlat = S // AQB
    scale2 = (HD ** -0.5) * LOG2E
    k_all = jnp.concatenate([kp_ref[...], kc_ref[...], kn_ref[...], kx_ref[...]], axis=0)
    v_all = jnp.concatenate([vp_ref[...], vc_ref[...], vn_ref[...], vx_ref[...]], axis=0)
    nloc = AQB + 2 * AWIN
    nk = nloc + LC
    r = lax.broadcasted_iota(jnp.int32, (AQB, nk), 0)
    c = lax.broadcasted_iota(jnp.int32, (AQB, nk), 1)
    rel = c - AWIN - r
    pos = n * AQB - AWIN + c
    hi = jnp.where(n < nlat, S, 0)
    ok = ((jnp.abs(rel) <= AWIN) & (pos >= 0) & (pos < hi)) | (c >= nloc)
    for hh in range(AH):
        gs = slice((hh // (AH // AKV)) * HD, (hh // (AH // AKV) + 1) * HD)
        q = q_ref[:, hh * HD:(hh + 1) * HD]
        s = lax.dot_general(q, k_all[:, gs], (((1,), (1,)), ((), ())),
                            preferred_element_type=F32) * scale2
        s = jnp.where(ok, s, NEG)
        sk = sink_ref[hh] * LOG2E
        m = jnp.maximum(jnp.max(s, axis=-1, keepdims=True), sk)
        e = jnp.exp2(s - m)
        den = jnp.sum(e, axis=-1, keepdims=True) + jnp.exp2(sk - m)
        o = jnp.dot(e.astype(BF16), v_all[:, gs], preferred_element_type=F32)
        o_ref[:, hh * HD:(hh + 1) * HD] = (o / den).astype(o_ref.dtype)


def _attn(p, sink):
    assert AQB == LC and AQB == 2 * AWIN
    nlat = S // AQB
    qw, kw = AH * HD, AKV * HD
    kcol = qw // kw
    vcol = kcol + 1
    last_w = S // AWIN - 1

    def prev(n):
        return jnp.clip(2 * n - 1, 0, last_w)

    def cur(n):
        return jnp.minimum(n, nlat - 1)

    def nxt(n):
        return jnp.clip(2 * n + 2, 0, last_w)

    return pl.pallas_call(
        _attn_kernel,
        grid=(M // AQB,),
        in_specs=[
            pl.BlockSpec(memory_space=pltpu.SMEM),
            pl.BlockSpec((AQB, qw), lambda n: (n, 0)),
            pl.BlockSpec((AWIN, kw), lambda n: (prev(n), kcol)),
            pl.BlockSpec((AQB, kw), lambda n: (cur(n), kcol)),
            pl.BlockSpec((AWIN, kw), lambda n: (nxt(n), kcol)),
            pl.BlockSpec((AWIN, kw), lambda n: (prev(n), vcol)),
            pl.BlockSpec((AQB, kw), lambda n: (cur(n), vcol)),
            pl.BlockSpec((AWIN, kw), lambda n: (nxt(n), vcol)),
            pl.BlockSpec((LC, kw), lambda n: (S // LC, kcol)),
            pl.BlockSpec((LC, kw), lambda n: (S // LC, vcol)),
        ],
        out_specs=pl.BlockSpec((AQB, qw), lambda n: (n, 0)),
        out_shape=jax.ShapeDtypeStruct((M, AH * HD), BF16),
        compiler_params=_cparams(("arbitrary",)),
        name="attn",
    )(sink, p, p, p, p, p, p, p, p, p)


def _conv_kernel(b_ref, c_ref, u_ref, cp_ref, up_ref, cn_ref, un_ref, w_ref, bias_ref, o_ref, *, tr):
    i = pl.program_id(0)
    lat_blocks = S // tr
    z = c_ref[...] * u_ref[...]
    zp = cp_ref[7:8, :] * up_ref[7:8, :]
    zn = cn_ref[0:1, :] * un_ref[0:1, :]
    has_prev = jnp.logical_and(i != 0, i != lat_blocks)
    has_next = jnp.logical_and(i != lat_blocks - 1, i != M // tr - 1)
    zp = jnp.where(has_prev, zp, 0.0)
    zn = jnp.where(has_next, zn, 0.0)
    row = lax.broadcasted_iota(jnp.int32, z.shape, 0)
    z_m1 = jnp.where(row == 0, zp, pltpu.roll(z, 1, 0))
    z_p1 = jnp.where(row == tr - 1, zn, pltpu.roll(z, tr - 1, 0))
    conv = z_m1 * w_ref[0:1, :] + z * w_ref[1:2, :] + z_p1 * w_ref[2:3, :] + bias_ref[...]
    o_ref[...] = (b_ref[...] * conv).astype(o_ref.dtype)


def _conv(p, conv_w, conv_b):
    tr = 256
    assert LC % tr == 0 and S % tr == 0
    h8 = tr // 8

    def prev8(i):
        return jnp.maximum(i * h8 - 1, 0)

    def next8(i):
        return jnp.minimum((i + 1) * h8, M // 8 - 1)

    return pl.pallas_call(
        functools.partial(_conv_kernel, tr=tr),
        grid=(M // tr,),
        in_specs=[
            pl.BlockSpec((tr, CCH), lambda i: (i, 0)),
            pl.BlockSpec((tr, CCH), lambda i: (i, 1)),
            pl.BlockSpec((tr, CCH), lambda i: (i, 2)),
            pl.BlockSpec((8, CCH), lambda i: (prev8(i), 1)),
            pl.BlockSpec((8, CCH), lambda i: (prev8(i), 2)),
            pl.BlockSpec((8, CCH), lambda i: (next8(i), 1)),
            pl.BlockSpec((8, CCH), lambda i: (next8(i), 2)),
            pl.BlockSpec((3, CCH), lambda i: (0, 0)),
            pl.BlockSpec((1, CCH), lambda i: (0, 0)),
        ],
        out_specs=pl.BlockSpec((tr, CCH), lambda i: (i, 0)),
        out_shape=jax.ShapeDtypeStruct((M, CCH), BF16),
        compiler_params=_cparams(("arbitrary",)),
        name="conv",
    )(p, p, p, p, p, p, p, conv_w, conv_b.reshape(1, CCH))


def _log_sigmoid(d):
    return jnp.minimum(d, 0.0) - jnp.log(1.0 + jnp.exp(-jnp.abs(d)))


def _ret_first_step(decay_ref, st_ref, dm_ref, xi_ref, zeta_ref, backward):
    @pl.when(pl.program_id(0) == 0)
    def _():
        st_ref[...] = jnp.zeros_like(st_ref)
        ii = lax.broadcasted_iota(jnp.int32, (RC, RC), 0)
        jj = lax.broadcasted_iota(jnp.int32, (RC, RC), 1)
        ic = lax.broadcasted_iota(jnp.int32, (RC, 1), 0).astype(F32)
        for h in range(RH):
            lg = _log_sigmoid(jnp.full((1, 1), decay_ref[h], F32))
            if backward:
                rel = jj - ii
                keep = rel > 0
                xi_ref[h] = jnp.exp(lg * (RC - ic))
                zeta_ref[h] = jnp.exp(lg * ic)
            else:
                rel = ii - jj
                keep = rel >= 0
                xi_ref[h] = jnp.exp(lg * (ic + 1.0))
                zeta_ref[h] = jnp.exp(lg * (RC - 1.0 - ic))
            dm_ref[h] = jnp.where(keep, jnp.exp(lg * jnp.where(keep, rel, 0).astype(F32)), 0.0)


def _ret_chunk(decay_ref, q_ref, k_ref, v_ref, st_ref, dm_ref, xi_ref, zeta_ref, h):
    lg = _log_sigmoid(jnp.full((1, 1), decay_ref[h], F32))
    q = q_ref[:, h * RDK:(h + 1) * RDK]
    k = k_ref[:, h * RDK:(h + 1) * RDK]
    v = v_ref[:, h * RDV:(h + 1) * RDV]
    inner = lax.dot_general(q, k, (((1,), (1,)), ((), ())), preferred_element_type=F32) * dm_ref[h]
    y = jnp.dot(inner.astype(BF16), v.astype(BF16), preferred_element_type=F32)
    st = st_ref[h]
    y = y + jnp.dot(q, st.astype(BF16), preferred_element_type=F32) * xi_ref[h]
    kv = lax.dot_general(k, (v * zeta_ref[h]).astype(BF16), (((0,), (0,)), ((), ())),
                         preferred_element_type=F32)
    st_ref[h] = jnp.exp(lg * float(RC)) * st + kv
    return y


def _ret_fwd_kernel(decay_ref, q_ref, k_ref, v_ref, y_ref, st_ref, dm_ref, xi_ref, zeta_ref):
    _ret_first_step(decay_ref, st_ref, dm_ref, xi_ref, zeta_ref, False)
    for h in range(RH):
        y_ref[:, h * RDV:(h + 1) * RDV] = _ret_chunk(decay_ref, q_ref, k_ref, v_ref, st_ref,
                                                     dm_ref, xi_ref, zeta_ref, h)


def _ret_bwd_kernel(decay_ref, q_ref, k_ref, v_ref, yf_ref, g_ref, gnw_ref, o_ref,
                    st_ref, dm_ref, xi_ref, zeta_ref, ys_ref):
    _ret_first_step(decay_ref, st_ref, dm_ref, xi_ref, zeta_ref, True)
    for h in range(RH):
        sl = slice(h * RDV, (h + 1) * RDV)
        ys_ref[:, sl] = yf_ref[:, sl] + _ret_chunk(decay_ref, q_ref, k_ref, v_ref, st_ref,
                                                   dm_ref, xi_ref, zeta_ref, h)
    gnw = gnw_ref[...]

    def emit(rs, y, start):
        gt = g_ref[rs, :]
        for h in range(RH):
            sl = slice(h * RDV, (h + 1) * RDV)
            yh = y[:, sl]
            yc = yh - jnp.mean(yh, axis=-1, keepdims=True)
            var = jnp.mean(yc * yc, axis=-1, keepdims=True)
            yn = yc * lax.rsqrt(var + EPS) * gnw[:, sl]
            gh = gt[:, sl]
            o_ref[rs, sl] = (gh * jax.nn.sigmoid(gh) * yn).astype(o_ref.dtype)

    _norm_rows(ys_ref, RC, emit, unroll=8)


def _retention(p_qk, p_vg, decay_f, decay_b, gn_w):
    nch = M // RC
    nlat = S // RC
    qw, vw = RH * RDK, RH * RDV
    fwd = lambda s: ((s + nlat) % nch)
    bwd = lambda s: (nch - 1 - s)
    smem = pl.BlockSpec(memory_space=pltpu.SMEM)

    def specs(cm):
        return [smem,
                pl.BlockSpec((RC, qw), lambda s: (cm(s), 0)),
                pl.BlockSpec((RC, qw), lambda s: (cm(s), 1)),
                pl.BlockSpec((RC, vw), lambda s: (cm(s), 0))]

    scratch = [pltpu.VMEM((RH, RDK, RDV), F32), pltpu.VMEM((RH, RC, RC), F32),
               pltpu.VMEM((RH, RC, 1), F32), pltpu.VMEM((RH, RC, 1), F32)]
    yf = pl.pallas_call(
        _ret_fwd_kernel,
        grid=(nch,),
        in_specs=specs(fwd),
        out_specs=pl.BlockSpec((RC, vw), lambda s: (fwd(s), 0)),
        out_shape=jax.ShapeDtypeStruct((M, vw), F32),
        scratch_shapes=scratch,
        compiler_params=_cparams(("arbitrary",)),
        name="ret_fwd",
    )(decay_f, p_qk, p_qk, p_vg)
    return pl.pallas_call(
        _ret_bwd_kernel,
        grid=(nch,),
        in_specs=specs(bwd) + [pl.BlockSpec((RC, vw), lambda s: (bwd(s), 0)),
                               pl.BlockSpec((RC, vw), lambda s: (bwd(s), 1)),
                               pl.BlockSpec((1, vw), lambda s: (0, 0))],
        out_specs=pl.BlockSpec((RC, vw), lambda s: (bwd(s), 0)),
        out_shape=jax.ShapeDtypeStruct((M, vw), BF16),
        scratch_shapes=scratch + [pltpu.VMEM((RC, vw), F32)],
        compiler_params=_cparams(("arbitrary",)),
        name="ret_bwd",
    )(decay_b, p_qk, p_qk, p_vg, yf, p_vg, gn_w.reshape(1, vw))


_NA_NBLK = S // NQB
_NA_CASES = (0, 1, _NA_NBLK - 1)


def _na_window_row(b):
    return jnp.clip(NQROWS * b - NROWS // 2, 0, GROWS - NWIN_ROWS)


def _na_table_kernel(rpb_ref, o_ref):
    h = pl.program_id(0)
    n_dr, n_dc = 2 * NROWS - 1, 2 * NCOLS - 1
    cq = lax.broadcasted_iota(jnp.int32, (GW, 2 * GW), 0)
    lane = lax.broadcasted_iota(jnp.int32, (GW, 2 * GW), 1)
    right = lane >= GW
    kc = jnp.where(right, lane - GW, lane)
    dc = kc - cq + (NCOLS - 1)
    cs = jnp.clip(cq - NCOLS // 2, 0, GW - NCOLS)
    col_ok = (kc >= cs) & (kc < cs + NCOLS)
    memo = {}

    def row_scalar(a, b):
        return rpb_ref[(h * n_dr + a) * n_dc + b] if 0 <= a < n_dr else 0.0

    def pair(a0):
        if a0 not in memo:
            acc = jnp.zeros((GW, 2 * GW), F32)
            for b in range(n_dc):
                vec = jnp.where(right, row_scalar(a0 + 1, b), row_scalar(a0, b))
                acc = jnp.where(dc == b, vec, acc)
            memo[a0] = acc
        return memo[a0]

    for ci, blk in enumerate(_NA_CASES):
        w = int(np.clip(NQROWS * blk - NROWS // 2, 0, GROWS - NWIN_ROWS))
        for rl in range(NQROWS):
            r = NQROWS * blk + rl
            rs = int(np.clip(r - NROWS // 2, 0, GROWS - NROWS))
            for pr in range(NWIN_ROWS // 2):
                kr = w + 2 * pr
                ok_l = rs <= kr < rs + NROWS
                ok_r = rs <= kr + 1 < rs + NROWS
                if ok_l and ok_r:
                    ok = col_ok
                elif ok_l:
                    ok = col_ok & jnp.logical_not(right)
                elif ok_r:
                    ok = col_ok & right
                else:
                    ok = None
                if ok is None:
                    tile = jnp.full((GW, 2 * GW), NEG, F32)
                else:
                    tile = jnp.where(ok, pair(kr - r + NROWS - 1) * LOG2E, NEG)
                o_ref[ci, 0, rl * GW:(rl + 1) * GW, pr * 2 * GW:(pr + 1) * 2 * GW] = tile


def _na_tables(rpb):
    return pl.pallas_call(
        _na_table_kernel,
        grid=(NH,),
        in_specs=[pl.BlockSpec(memory_space=pltpu.SMEM)],
        out_specs=pl.BlockSpec((len(_NA_CASES), 1, NQB, NWIN), lambda h: (0, h, 0, 0)),
        out_shape=jax.ShapeDtypeStruct((len(_NA_CASES), NH, NQB, NWIN), F32),
        compiler_params=_cparams(("arbitrary",)),
        name="na_tables",
    )(rpb.reshape(-1))


def _na_kernel(q_ref, k_ref, v_ref, t_ref, o_ref):
    b = pl.program_id(1)
    scale2 = (HD ** -0.5) * LOG2E
    start = pl.multiple_of(_na_window_row(b) * GW, NQB)
    nt = (((1,), (1,)), ((), ()))
    for hh in range(NHPS):
        cs = slice(hh * HD, (hh + 1) * HD)
        q = q_ref[:, cs]
        s_loc = lax.dot_general(q, k_ref[pl.ds(start, NWIN), cs], nt,
                                preferred_element_type=F32) * scale2 + t_ref[0, hh]
        s_ctx = lax.dot_general(q, k_ref[S:M, cs], nt, preferred_element_type=F32) * scale2
        m = jnp.maximum(jnp.max(s_loc, axis=-1, keepdims=True), jnp.max(s_ctx, axis=-1, keepdims=True))
        e_loc = jnp.exp2(s_loc - m)
        e_ctx = jnp.exp2(s_ctx - m)
        den = jnp.sum(e_loc, axis=-1, keepdims=True) + jnp.sum(e_ctx, axis=-1, keepdims=True)
        o = (jnp.dot(e_loc.astype(BF16), v_ref[pl.ds(start, NWIN), cs], preferred_element_type=F32)
             + jnp.dot(e_ctx.astype(BF16), v_ref[S:M, cs], preferred_element_type=F32))
        o_ref[:, cs] = (o / den).astype(o_ref.dtype)


def _na(p, rpb):
    gw = NHPS * HD
    qcol = 0
    kcol = qcol + NH // NHPS
    vcol = kcol + NH // NHPS
    tables = _na_tables(rpb)

    def case(b):
        return jnp.minimum(b, 1) + jnp.maximum(b - (_NA_NBLK - 2), 0)

    return pl.pallas_call(
        _na_kernel,
        grid=(NH // NHPS, _NA_NBLK),
        in_specs=[
            pl.BlockSpec((NQB, gw), lambda h, b: (b, qcol + h)),
            pl.BlockSpec((M, gw), lambda h, b: (0, kcol + h)),
            pl.BlockSpec((M, gw), lambda h, b: (0, vcol + h)),
            pl.BlockSpec((1, NHPS, NQB, NWIN), lambda h, b: (case(b), h, 0, 0)),
        ],
        out_specs=pl.BlockSpec((NQB, gw), lambda h, b: (b, h)),
        out_shape=jax.ShapeDtypeStruct((S, NH * HD), BF16),
        compiler_params=_cparams(("arbitrary", "arbitrary")),
        name="na",
    )(p, p, p, tables)


def _tiles(rows):
    if rows == M:
        return dict(up=(2112, 704), small=528, inp=1056)
    assert rows == S
    return dict(up=(2048, 1024), small=512, inp=1024)


def _ffn_half(h, hn, rows, mods, layer, third, wi, wo, nw_next, next_mod, cast_jobs=()):
    t = _tiles(rows)
    a, wob = _mm_swiglu(hn, wi, wo, layer, tm=t["up"][0], mm=t["up"][1], tn=512)
    return _ffn_down(h, rows, mods, layer, 3 * third + 2, a, wob, nw_next, next_mod, cast_jobs)


IN_TN = 1536


def kernel(x, c, ctx, c_ctx, ada_w, ada_b, norm_w, ffn_a_wi, ffn_a_wo, ffn_b_wi, ffn_b_wo,
           ev_w_in, ev_w_out, ev_sink, ev_conv_w, ev_conv_b,
           od_w_in, od_w_out, od_decay_f, od_decay_b, od_gn_w, od_rpb, final_norm_w):
    assert x.shape == (1, S, D) and ctx.shape == (1, LC, D) and ada_w.shape[0] == 2
    assert ffn_a_wi.shape == (2, D, 2 * DFF) and ev_w_in.shape == (1, D, EV_IN)
    assert od_w_in.shape == (1, D, OD_IN) and RDK == HD
    cvec = jnp.concatenate([c, c_ctx[None, :], jnp.zeros((6, D), F32)], axis=0)
    mods = _ada(cvec, ada_w, ada_b)
    tabs = _rope_tables()

    tm_in = _tiles(M)["inp"]
    hn = _normmod_first(x[0], ctx[0], norm_w[0, 0], mods)
    (h, hn), (w_in, w_out) = _ffn_half((x[0], ctx[0]), hn, M, mods, 0, 0, ffn_a_wi, ffn_a_wo,
                                       norm_w[0, 1], (0, 1), cast_jobs=((ev_w_in, 0), (ev_w_out, 0)))
    n_att = (AH + 2 * AKV) * HD
    p_att = _mm_rope(hn, w_in, n_att, tabs, AH + AKV, tm=tm_in)
    p_conv = _mm_plain(hn, w_in, n_att, EV_IN - n_att, tm=tm_in, tn=IN_TN, out_dtype=F32)
    att = _attn(p_att, ev_sink[0])
    cnv = _conv(p_conv, ev_conv_w[0], ev_conv_b[0])
    h, hn = _out_proj(h, M, mods, 0, att, cnv, w_out, norm_w[0, 2], tm=_tiles(M)["small"])
    (h, hn), _ = _ffn_half(h, hn, M, mods, 0, 2, ffn_b_wi, ffn_b_wo, norm_w[1, 0], (1, 0))

    (h, hn), (w_in, w_out) = _ffn_half(h, hn, M, mods, 1, 0, ffn_a_wi, ffn_a_wo, norm_w[1, 1], (1, 1),
                                       cast_jobs=((od_w_in, 0), (od_w_out, 0)))
    n_qk, n_vg = 2 * RH * RDK, 2 * RH * RDV
    p_qk = _mm_rope(hn, w_in, n_qk, tabs, 2 * RH, tm=tm_in, scaled=(RH, 2 * RH))
    p_vg = _mm_plain(hn, w_in, n_qk, n_vg, tm=tm_in, tn=n_qk, out_dtype=F32)
    n_na = OD_IN - n_qk - n_vg
    p_na = _mm_plain(hn, w_in, n_qk + n_vg, n_na, tm=tm_in, tn=n_na, out_dtype=BF16)
    ret = _retention(p_qk, p_vg, od_decay_f[0], od_decay_b[0], od_gn_w[0])
    nat = _na(p_na, od_rpb[0])
    h, hn = _out_proj(h, S, mods, 1, ret, nat, w_out, norm_w[1, 2], tm=_tiles(S)["small"])
    (out,), _ = _ffn_half(h, hn, S, mods, 1, 2, ffn_b_wi, ffn_b_wo, final_norm_w, None)
    return out[None]
```

```python
import functools

import numpy as np
import jax
import jax.numpy as jnp
from jax import lax
from jax.experimental import pallas as pl
from jax.experimental.pallas import tpu as pltpu

F32 = jnp.float32
BF16 = jnp.bfloat16

D = 2048
S = 8192
LC = 256
M = S + LC
GW = 64
GROWS = S // GW
HD = 128
DFF = 5632
NMOD = 9
EPS = 1e-6
ROPE_BASE = 10000.0
NEG = -1e30

AH, AKV, AWIN = 8, 2, 128
AQB = 256
CCH = 1024
EV_IN = AH * HD + 2 * AKV * HD + 3 * CCH
RH, RDK, RDV = 4, 128, 256
RC = 256
NH, NROWS, NCOLS = 8, 8, 16
NQROWS = 4
NQB = NQROWS * GW
NWIN_ROWS = NQROWS + NROWS
NWIN = NWIN_ROWS * GW
NHPS = 4
LOG2E = float(np.log2(np.e))
OD_IN = 2 * RH * RDK + 2 * RH * RDV + 3 * NH * HD

VMEM_LIMIT = 56 * 1024 * 1024


def _cparams(sem):
    return pltpu.CompilerParams(dimension_semantics=sem, vmem_limit_bytes=VMEM_LIMIT)


def _sub(r, size):
    return pl.ds(pl.multiple_of(r * size, 16), size)


def _ada_kernel(c_ref, w_ref, b_ref, o_ref):
    cv = c_ref[...]
    a = cv * jax.nn.sigmoid(cv)
    acc = jnp.dot(a.astype(BF16), w_ref[0].astype(BF16), preferred_element_type=F32)
    o_ref[0, 0] = acc + b_ref[0]


def _ada(cvec, ada_w, ada_b):
    depth = ada_w.shape[0]
    tn = D
    per = D // tn
    return pl.pallas_call(
        _ada_kernel,
        grid=(depth, NMOD * per),
        in_specs=[
            pl.BlockSpec((8, D), lambda l, j: (0, 0)),
            pl.BlockSpec((1, D, tn), lambda l, j: (l, 0, j)),
            pl.BlockSpec((1, 1, tn), lambda l, j: (l, 0, j)),
        ],
        out_specs=pl.BlockSpec((1, 1, 8, tn), lambda l, j: (l, j // per, 0, j % per)),
        out_shape=jax.ShapeDtypeStruct((depth, NMOD, 8, D), F32),
        compiler_params=_cparams(("arbitrary", "arbitrary")),
        name="ada",
    )(cvec, ada_w, ada_b.reshape(depth, 1, NMOD * D))


NORM_TR = 1024
NORM_CHUNK = 16
NORM_UNROLL = 8


def _norm_rows(src_ref, n_rows, fn, unroll=NORM_UNROLL):
    group = NORM_CHUNK * unroll

    def outer(gi, carry):
        for u in range(unroll):
            start = pl.multiple_of(gi * group + u * NORM_CHUNK, NORM_CHUNK)
            rs = pl.ds(start, NORM_CHUNK)
            fn(rs, src_ref[rs, :], start)
        return carry

    lax.fori_loop(0, n_rows // group, outer, 0)


def _normmod_first_kernel(x_ref, c_ref, nw_ref, mod_ref, o_ref):
    i = pl.program_id(0)
    kind = (i * NORM_TR >= S).astype(jnp.int32)
    n_rows = jnp.minimum(M - i * NORM_TR, NORM_TR)
    shift = mod_ref[0, 0, pl.ds(kind, 1), :]
    gain = nw_ref[...] * (1.0 + mod_ref[0, 1, pl.ds(kind, 1), :])

    def emit(rs, x, start):
        rinv = lax.rsqrt(jnp.mean(x * x, axis=-1, keepdims=True) + EPS)
        o_ref[rs, :] = ((x * rinv) * gain + shift).astype(o_ref.dtype)

    @pl.when(kind == 0)
    def _():
        _norm_rows(x_ref, n_rows, emit)

    @pl.when(kind == 1)
    def _():
        _norm_rows(c_ref, n_rows, emit)


def _normmod_first(x, ctx, nw, mods):
    assert S % NORM_TR == 0 and LC % (NORM_CHUNK * NORM_UNROLL) == 0 and LC <= NORM_TR
    return pl.pallas_call(
        _normmod_first_kernel,
        grid=(pl.cdiv(M, NORM_TR),),
        in_specs=[pl.BlockSpec((NORM_TR, D), lambda i: (jnp.minimum(i, S // NORM_TR - 1), 0)),
                  pl.BlockSpec((LC, D), lambda i: (0, 0)),
                  pl.BlockSpec((1, D), lambda i: (0, 0)),
                  pl.BlockSpec((1, 3, 8, D), lambda i: (0, 0, 0, 0))],
        out_specs=pl.BlockSpec((NORM_TR, D), lambda i: (i, 0)),
        out_shape=jax.ShapeDtypeStruct((M, D), BF16),
        compiler_params=_cparams(("arbitrary",)),
        name="normmod_first",
    )(x, ctx, nw.reshape(1, D), mods)


def _cast_weights(w_refs, wb_refs):
    @pl.when(pl.program_id(1) == 0)
    def _():
        for w_ref, wb_ref in zip(w_refs, wb_refs):
            wb_ref[...] = w_ref[...].astype(BF16)


def _mm_swiglu_kernel(a_ref, wg_ref, wu_ref, wo_ref, o_ref, wob_ref, wgb_ref, wub_ref, *, tm, mm):
    _cast_weights((wg_ref, wu_ref), (wgb_ref, wub_ref))
    wob_ref[...] = wo_ref[...].astype(BF16)

    def body(r, carry):
        a = a_ref[_sub(r, mm), :]
        g = jnp.dot(a, wgb_ref[...], preferred_element_type=F32)
        u = jnp.dot(a, wub_ref[...], preferred_element_type=F32)
        o_ref[_sub(r, mm), :] = (g * jax.nn.sigmoid(g) * u).astype(o_ref.dtype)
        return carry

    lax.fori_loop(0, tm // mm, body, 0, unroll=True)


def _mm_plain_kernel(a_ref, w_ref, o_ref):
    o_ref[...] = jnp.dot(a_ref[...], w_ref[...], preferred_element_type=F32).astype(o_ref.dtype)


def _mm_swiglu(a, wi, wo, layer, *, tm, mm, tn):
    rows, k = a.shape
    assert rows % tm == 0 and tm % mm == 0 and mm % 16 == 0
    n_out = wi.shape[2] // 2
    nj, ni = n_out // tn, rows // tm
    assert wo.shape[1] % (nj * ni) == 0
    slab = wo.shape[1] // (nj * ni)
    assert slab % 16 == 0
    return pl.pallas_call(
        functools.partial(_mm_swiglu_kernel, tm=tm, mm=mm),
        grid=(nj, ni),
        in_specs=[pl.BlockSpec((tm, k), lambda j, i: (i, 0)),
                  pl.BlockSpec((None, k, tn), lambda j, i: (layer, 0, j)),
                  pl.BlockSpec((None, k, tn), lambda j, i: (layer, 0, j + nj)),
                  pl.BlockSpec((None, slab, D), lambda j, i: (layer, j * ni + i, 0))],
        out_specs=[pl.BlockSpec((tm, tn), lambda j, i: (i, j)),
                   pl.BlockSpec((slab, D), lambda j, i: (j * ni + i, 0))],
        out_shape=[jax.ShapeDtypeStruct((rows, n_out), BF16),
                   jax.ShapeDtypeStruct((wo.shape[1], D), BF16)],
        scratch_shapes=[pltpu.VMEM((k, tn), BF16)] * 2,
        compiler_params=_cparams(("arbitrary", "arbitrary")),
        name="mm_swiglu",
    )(a, wi, wi, wo)


def _mm_plain(a, wb, col0, ncols, *, tm, tn, out_dtype):
    rows, k = a.shape
    assert rows % tm == 0 and ncols % tn == 0 and col0 % tn == 0
    return pl.pallas_call(
        _mm_plain_kernel,
        grid=(ncols // tn, rows // tm),
        in_specs=[pl.BlockSpec((tm, k), lambda j, i: (i, 0)),
                  pl.BlockSpec((k, tn), lambda j, i: (0, col0 // tn + j))],
        out_specs=pl.BlockSpec((tm, tn), lambda j, i: (i, j)),
        out_shape=jax.ShapeDtypeStruct((rows, ncols), out_dtype),
        compiler_params=_cparams(("arbitrary", "arbitrary")),
        name="mm_plain",
    )(a, wb)


def _gate_rows(gate_ref, i, tm):
    rows = i * tm + lax.broadcasted_iota(jnp.int32, (tm, 1), 0)
    return jnp.where(rows >= S, gate_ref[0, 0, 1:2, :], gate_ref[0, 0, 0:1, :])


DOWN_TM = 256


def _ffn_down_kernel(*refs, n_jobs, final, split):
    it = iter(refs)
    h_ref = next(it)
    c_ref = next(it) if split else None
    gate_ref, a_ref, w_ref, nw_ref = next(it), next(it), next(it), next(it)
    mod_ref = None if final else next(it)
    job_in = [next(it) for _ in range(n_jobs)]
    o_ref = next(it)
    hn_ref = None if final else next(it)
    job_out = [next(it) for _ in range(n_jobs)]
    new_ref = next(it) if final else o_ref
    for src_ref, dst_ref in zip(job_in, job_out):
        dst_ref[...] = src_ref[...].astype(BF16)
    kind = (pl.program_id(0) * DOWN_TM >= S).astype(jnp.int32)
    acc = jnp.dot(a_ref[...], w_ref[...], preferred_element_type=F32)
    res = jnp.where(kind == 1, c_ref[...], h_ref[...]) if split else h_ref[...]
    new_ref[...] = res + (0.5 * gate_ref[0, 0, pl.ds(kind, 1), :]) * acc
    nw = nw_ref[...]
    if final:
        def emit(rs, x, start):
            rinv = lax.rsqrt(jnp.mean(x * x, axis=-1, keepdims=True) + EPS)
            o_ref[rs, :] = x * rinv * nw
    else:
        shift = mod_ref[0, 0, pl.ds(kind, 1), :]
        gain = nw * (1.0 + mod_ref[0, 1, pl.ds(kind, 1), :])

        def emit(rs, x, start):
            rinv = lax.rsqrt(jnp.mean(x * x, axis=-1, keepdims=True) + EPS)
            hn_ref[rs, :] = ((x * rinv) * gain + shift).astype(hn_ref.dtype)

    _norm_rows(new_ref, DOWN_TM, emit)


def _ffn_down(h, rows, mods, layer, gate_idx, a, wb, nw_next, next_mod=None, cast_jobs=()):
    k = a.shape[1]
    final = next_mod is None
    split = isinstance(h, tuple)
    tm = DOWN_TM
    assert rows % tm == 0 and S % tm == 0 and tm % (NORM_CHUNK * NORM_UNROLL) == 0
    assert not split or (rows == M and LC == tm)
    n_slabs = S // tm
    row = lambda i: (i, 0)
    job_in, job_out, job_shape = [], [], []
    for w, idx in cast_jobs:
        kk, nn = w.shape[1:]
        assert kk % n_slabs == 0 and (kk // n_slabs) % 16 == 0
        slab = kk // n_slabs
        job_in.append(pl.BlockSpec((None, slab, nn),
                                   lambda i, idx=idx: (idx, jnp.minimum(i, n_slabs - 1), 0)))
        job_out.append(pl.BlockSpec((slab, nn), lambda i: (jnp.minimum(i, n_slabs - 1), 0)))
        job_shape.append(jax.ShapeDtypeStruct((kk, nn), BF16))
    if split:
        h_specs = [pl.BlockSpec((tm, D), lambda i: (jnp.minimum(i, n_slabs - 1), 0)),
                   pl.BlockSpec((tm, D), lambda i: (0, 0))]
        h_args = list(h)
    else:
        h_specs, h_args = [pl.BlockSpec((tm, D), row)], [h]
    in_specs = h_specs + [pl.BlockSpec((1, 1, 8, D), lambda i: (layer, gate_idx, 0, 0)),
                          pl.BlockSpec((tm, k), row),
                          pl.BlockSpec((k, D), lambda i: (0, 0), pipeline_mode=pl.Buffered(1)),
                          pl.BlockSpec((1, D), lambda i: (0, 0))]
    args = h_args + [mods, a, wb, nw_next.reshape(1, D)]
    out_specs = [pl.BlockSpec((tm, D), row)]
    out_shape = [jax.ShapeDtypeStruct((rows, D), F32)]
    if not final:
        in_specs.append(pl.BlockSpec((1, 3, 8, D), lambda i: (next_mod[0], next_mod[1], 0, 0)))
        args.append(mods)
        out_specs.append(pl.BlockSpec((tm, D), row))
        out_shape.append(jax.ShapeDtypeStruct((rows, D), BF16))
    n_main = len(out_shape)
    res = pl.pallas_call(
        functools.partial(_ffn_down_kernel, n_jobs=len(cast_jobs), final=final, split=split),
        grid=(rows // tm,),
        in_specs=in_specs + job_in,
        out_specs=out_specs + job_out,
        out_shape=out_shape + job_shape,
        scratch_shapes=[pltpu.VMEM((tm, D), F32)] if final else [],
        compiler_params=_cparams(("arbitrary",)),
        name="ffn_down_final" if final else "ffn_down",
    )(*args, *[w for w, _ in cast_jobs])
    return tuple(res[:n_main]), list(res[n_main:])


def _out_proj_kernel(h_ref, gate_ref, a1_ref, a2_ref, w1_ref, w2_ref, nw_ref, mod_ref,
                     o_ref, hn_ref, *, tm, unroll, has_ctx):
    i = pl.program_id(0)
    acc = (jnp.dot(a1_ref[...], w1_ref[...], preferred_element_type=F32)
           + jnp.dot(a2_ref[...], w2_ref[...], preferred_element_type=F32))
    gate = _gate_rows(gate_ref, i, tm) if has_ctx else gate_ref[0, 0, 0:1, :]
    o_ref[...] = h_ref[...] + gate * acc
    nw = nw_ref[...]
    gain_lat = nw * (1.0 + mod_ref[0, 1, 0:1, :])
    gain_ctx = nw * (1.0 + mod_ref[0, 1, 1:2, :])
    shift_lat = mod_ref[0, 0, 0:1, :]
    shift_ctx = mod_ref[0, 0, 1:2, :]

    def emit(rs, x, start):
        rinv = lax.rsqrt(jnp.mean(x * x, axis=-1, keepdims=True) + EPS)
        gain, shift = gain_lat, shift_lat
        if has_ctx:
            isc = (i * tm + start + lax.broadcasted_iota(jnp.int32, (NORM_CHUNK, 1), 0)) >= S
            gain = jnp.where(isc, gain_ctx, gain_lat)
            shift = jnp.where(isc, shift_ctx, shift_lat)
        hn_ref[rs, :] = ((x * rinv) * gain + shift).astype(hn_ref.dtype)

    _norm_rows(o_ref, tm, emit, unroll=unroll)


def _out_proj(h, rows, mods, layer, a1, a2, wb, nw_next, *, tm):
    k = a1.shape[1]
    assert a2.shape[1] == k and wb.shape == (2 * k, D) and rows % tm == 0
    unroll = max(u for u in range(1, 13) if (tm // NORM_CHUNK) % u == 0)
    row = lambda i: (i, 0)
    return pl.pallas_call(
        functools.partial(_out_proj_kernel, tm=tm, unroll=unroll, has_ctx=rows > S),
        grid=(rows // tm,),
        in_specs=[pl.BlockSpec((tm, D), row),
                  pl.BlockSpec((1, 1, 8, D), lambda i: (layer, 5, 0, 0)),
                  pl.BlockSpec((tm, k), row),
                  pl.BlockSpec((tm, k), row),
                  pl.BlockSpec((k, D), lambda i: (0, 0)),
                  pl.BlockSpec((k, D), lambda i: (1, 0)),
                  pl.BlockSpec((1, D), lambda i: (0, 0)),
                  pl.BlockSpec((1, 3, 8, D), lambda i: (layer, 2, 0, 0))],
        out_specs=[pl.BlockSpec((tm, D), row), pl.BlockSpec((tm, D), row)],
        out_shape=[jax.ShapeDtypeStruct((rows, D), F32), jax.ShapeDtypeStruct((rows, D), BF16)],
        compiler_params=_cparams(("arbitrary",)),
        name="out_proj",
    )(h, mods, a1, a2, wb, wb, nw_next.reshape(1, D), mods)


def _rope_tables():
    t = np.arange(S)
    rows = (t // GW).astype(np.float32)
    cols = (t % GW).astype(np.float32)
    half = HD // 2
    inv = np.float32(ROPE_BASE) ** (-np.arange(0, half, 2, dtype=np.float32) / np.float32(half))
    ar = rows[:, None] * inv
    ac = cols[:, None] * inv
    cos = np.concatenate([np.cos(ar), np.cos(ar), np.cos(ac), np.cos(ac)], axis=-1)
    sin = np.concatenate([-np.sin(ar), np.sin(ar), -np.sin(ac), np.sin(ac)], axis=-1)
    lat = np.concatenate([cos, sin], axis=-1)
    ctx = np.concatenate([np.ones((LC, HD), np.float32), np.zeros((LC, HD), np.float32)], axis=-1)
    return jnp.asarray(np.concatenate([lat, ctx], axis=0).astype(np.float32))


def _rope(x, tab):
    c = tab[:, :HD]
    sg = tab[:, HD:]
    lane = lax.broadcasted_iota(jnp.int32, x.shape, 1)
    first = (lane % (HD // 2)) < (HD // 4)
    partner = jnp.where(first, pltpu.roll(x, HD - HD // 4, 1), pltpu.roll(x, HD // 4, 1))
    return x * c + partner * sg


def _mm_rope_kernel(a_ref, w_ref, tab_ref, o_ref, *, n_rot, scaled):
    acc = jnp.dot(a_ref[...], w_ref[...], preferred_element_type=F32)
    tab = tab_ref[...]
    for hh in range(acc.shape[1] // HD):
        x = acc[:, hh * HD:(hh + 1) * HD]
        if hh < n_rot:
            x = _rope(x, tab)
        if scaled[0] <= hh < scaled[1]:
            x = x * (HD ** -0.5)
        o_ref[:, hh * HD:(hh + 1) * HD] = x.astype(o_ref.dtype)


def _mm_rope(a, wb, ncols, tabs, n_rot, *, tm, scaled=(0, 0)):
    rows, k = a.shape
    assert rows % tm == 0 and ncols % HD == 0
    return pl.pallas_call(
        functools.partial(_mm_rope_kernel, n_rot=n_rot, scaled=scaled),
        grid=(rows // tm,),
        in_specs=[pl.BlockSpec((tm, k), lambda i: (i, 0)),
                  pl.BlockSpec((k, ncols), lambda i: (0, 0)),
                  pl.BlockSpec((tm, 2 * HD), lambda i: (i, 0))],
        out_specs=pl.BlockSpec((tm, ncols), lambda i: (i, 0)),
        out_shape=jax.ShapeDtypeStruct((rows, ncols), BF16),
        compiler_params=_cparams(("arbitrary",)),
        name="mm_rope",
    )(a, wb, tabs)


def _attn_kernel(sink_ref, q_ref, kp_ref, kc_ref, kn_ref, vp_ref, vc_ref, vn_ref, kx_ref, vx_ref, o_ref):
    n = pl.program_id(0)
    nlat = S // AQB
    scale2 = (HD ** -0.5) * LOG2E
    k_all = jnp.concatenate([kp_ref[...], kc_ref[...], kn_ref[...], kx_ref[...]], axis=0)
    v_all = jnp.concatenate([vp_ref[...], vc_ref[...], vn_ref[...], vx_ref[...]], axis=0)
    nloc = AQB + 2 * AWIN
    nk = nloc + LC
    r = lax.broadcasted_iota(jnp.int32, (AQB, nk), 0)
    c = lax.broadcasted_iota(jnp.int32, (AQB, nk), 1)
    rel = c - AWIN - r
    pos = n * AQB - AWIN + c
    hi = jnp.where(n < nlat, S, 0)
    ok = ((jnp.abs(rel) <= AWIN) & (pos >= 0) & (pos < hi)) | (c >= nloc)
    for hh in range(AH):
        gs = slice((hh // (AH // AKV)) * HD, (hh // (AH // AKV) + 1) * HD)
        q = q_ref[:, hh * HD:(hh + 1) * HD]
        s = lax.dot_general(q, k_all[:, gs], (((1,), (1,)), ((), ())),
                            preferred_element_type=F32) * scale2
        s = jnp.where(ok, s, NEG)
        sk = sink_ref[hh] * LOG2E
        m = jnp.maximum(jnp.max(s, axis=-1, keepdims=True), sk)
        e = jnp.exp2(s - m)
        den = jnp.sum(e, axis=-1, keepdims=True) + jnp.exp2(sk - m)
        o = jnp.dot(e.astype(BF16), v_all[:, gs], preferred_element_type=F32)
        o_ref[:, hh * HD:(hh + 1) * HD] = (o / den).astype(o_ref.dtype)


def _attn(p, sink):
    assert AQB == LC and AQB == 2 * AWIN
    nlat = S // AQB
    qw, kw = AH * HD, AKV * HD
    kcol = qw // kw
    vcol = kcol + 1
    last_w = S // AWIN - 1

    def prev(n):
        return jnp.clip(2 * n - 1, 0, last_w)

    def cur(n):
        return jnp.minimum(n, nlat - 1)

    def nxt(n):
        return jnp.clip(2 * n + 2, 0, last_w)

    return pl.pallas_call(
        _attn_kernel,
        grid=(M // AQB,),
        in_specs=[
            pl.BlockSpec(memory_space=pltpu.SMEM),
            pl.BlockSpec((AQB, qw), lambda n: (n, 0)),
            pl.BlockSpec((AWIN, kw), lambda n: (prev(n), kcol)),
            pl.BlockSpec((AQB, kw), lambda n: (cur(n), kcol)),
            pl.BlockSpec((AWIN, kw), lambda n: (nxt(n), kcol)),
            pl.BlockSpec((AWIN, kw), lambda n: (prev(n), vcol)),
            pl.BlockSpec((AQB, kw), lambda n: (cur(n), vcol)),
            pl.BlockSpec((AWIN, kw), lambda n: (nxt(n), vcol)),
            pl.BlockSpec((LC, kw), lambda n: (S // LC, kcol)),
            pl.BlockSpec((LC, kw), lambda n: (S // LC, vcol)),
        ],
        out_specs=pl.BlockSpec((AQB, qw), lambda n: (n, 0)),
        out_shape=jax.ShapeDtypeStruct((M, AH * HD), BF16),
        compiler_params=_cparams(("arbitrary",)),
        name="attn",
    )(sink, p, p, p, p, p, p, p, p, p)


def _conv_kernel(b_ref, c_ref, u_ref, cp_ref, up_ref, cn_ref, un_ref, w_ref, bias_ref, o_ref, *, tr):
    z = c_ref[...] * u_ref[...]
    zp = cp_ref[7:8, :] * up_ref[7:8, :]
    zn = cn_ref[0:1, :] * un_ref[0:1, :]
    row = lax.broadcasted_iota(jnp.int32, z.shape, 0)
    g = pl.program_id(0) * tr + row
    z_m1 = jnp.where(row == 0, zp, pltpu.roll(z, 1, 0))
    z_p1 = jnp.where(row == tr - 1, zn, pltpu.roll(z, tr - 1, 0))
    z_m1 = jnp.where((g == 0) | (g == S), 0.0, z_m1)
    z_p1 = jnp.where((g == S - 1) | (g == M - 1), 0.0, z_p1)
    conv = z_m1 * w_ref[0:1, :] + z * w_ref[1:2, :] + z_p1 * w_ref[2:3, :] + bias_ref[...]
    o_ref[...] = (b_ref[...] * conv).astype(o_ref.dtype)


def _conv(p, conv_w, conv_b):
    tr = 1056
    assert M % tr == 0 and tr % 8 == 0
    h8 = tr // 8

    def prev8(i):
        return jnp.maximum(i * h8 - 1, 0)

    def next8(i):
        return jnp.minimum((i + 1) * h8, M // 8 - 1)

    return pl.pallas_call(
        functools.partial(_conv_kernel, tr=tr),
        grid=(M // tr,),
        in_specs=[
            pl.BlockSpec((tr, CCH), lambda i: (i, 0)),
            pl.BlockSpec((tr, CCH), lambda i: (i, 1)),
            pl.BlockSpec((tr, CCH), lambda i: (i, 2)),
            pl.BlockSpec((8, CCH), lambda i: (prev8(i), 1)),
            pl.BlockSpec((8, CCH), lambda i: (prev8(i), 2)),
            pl.BlockSpec((8, CCH), lambda i: (next8(i), 1)),
            pl.BlockSpec((8, CCH), lambda i: (next8(i), 2)),
            pl.BlockSpec((3, CCH), lambda i: (0, 0)),
            pl.BlockSpec((1, CCH), lambda i: (0, 0)),
        ],
        out_specs=pl.BlockSpec((tr, CCH), lambda i: (i, 0)),
        out_shape=jax.ShapeDtypeStruct((M, CCH), BF16),
        compiler_params=_cparams(("arbitrary",)),
        name="conv",
    )(p, p, p, p, p, p, p, conv_w, conv_b.reshape(1, CCH))


def _log_sigmoid(d):
    return jnp.minimum(d, 0.0) - jnp.log(1.0 + jnp.exp(-jnp.abs(d)))


def _ret_first_step(decay_ref, st_ref, dm_ref, xi_ref, zeta_ref, backward):
    @pl.when(pl.program_id(0) == 0)
    def _():
        st_ref[...] = jnp.zeros_like(st_ref)
        ii = lax.broadcasted_iota(jnp.int32, (RC, RC), 0)
        jj = lax.broadcasted_iota(jnp.int32, (RC, RC), 1)
        ic = lax.broadcasted_iota(jnp.int32, (RC, 1), 0).astype(F32)
        for h in range(RH):
            lg = _log_sigmoid(jnp.full((1, 1), decay_ref[h], F32))
            if backward:
                rel = jj - ii
                keep = rel > 0
                xi_ref[h] = jnp.exp(lg * (RC - ic))
                zeta_ref[h] = jnp.exp(lg * ic)
            else:
                rel = ii - jj
                keep = rel >= 0
                xi_ref[h] = jnp.exp(lg * (ic + 1.0))
                zeta_ref[h] = jnp.exp(lg * (RC - 1.0 - ic))
            dm_ref[h] = jnp.where(keep, jnp.exp(lg * jnp.where(keep, rel, 0).astype(F32)), 0.0)


def _ret_chunk(decay_ref, q_ref, k_ref, v_ref, st_ref, dm_ref, xi_ref, zeta_ref, h):
    lg = _log_sigmoid(jnp.full((1, 1), decay_ref[h], F32))
    q = q_ref[:, h * RDK:(h + 1) * RDK]
    k = k_ref[:, h * RDK:(h + 1) * RDK]
    v = v_ref[:, h * RDV:(h + 1) * RDV]
    inner = lax.dot_general(q, k, (((1,), (1,)), ((), ())), preferred_element_type=F32) * dm_ref[h]
    y = jnp.dot(inner.astype(BF16), v.astype(BF16), preferred_element_type=F32)
    st = st_ref[h]
    y = y + jnp.dot(q, st.astype(BF16), preferred_element_type=F32) * xi_ref[h]
    kv = lax.dot_general(k, (v * zeta_ref[h]).astype(BF16), (((0,), (0,)), ((), ())),
                         preferred_element_type=F32)
    st_ref[h] = jnp.exp(lg * float(RC)) * st + kv
    return y


def _ret_fwd_kernel(decay_ref, q_ref, k_ref, v_ref, y_ref, st_ref, dm_ref, xi_ref, zeta_ref):
    _ret_first_step(decay_ref, st_ref, dm_ref, xi_ref, zeta_ref, False)
    for h in range(RH):
        y_ref[:, h * RDV:(h + 1) * RDV] = _ret_chunk(decay_ref, q_ref, k_ref, v_ref, st_ref,
                                                     dm_ref, xi_ref, zeta_ref, h)


def _ret_bwd_kernel(decay_ref, q_ref, k_ref, v_ref, yf_ref, g_ref, gnw_ref, o_ref,
                    st_ref, dm_ref, xi_ref, zeta_ref, ys_ref):
    _ret_first_step(decay_ref, st_ref, dm_ref, xi_ref, zeta_ref, True)
    for h in range(RH):
        sl = slice(h * RDV, (h + 1) * RDV)
        ys_ref[:, sl] = yf_ref[:, sl] + _ret_chunk(decay_ref, q_ref, k_ref, v_ref, st_ref,
                                                   dm_ref, xi_ref, zeta_ref, h)
    gnw = gnw_ref[...]

    def emit(rs, y, start):
        gt = g_ref[rs, :]
        for h in range(RH):
            sl = slice(h * RDV, (h + 1) * RDV)
            yh = y[:, sl]
            yc = yh - jnp.mean(yh, axis=-1, keepdims=True)
            var = jnp.mean(yc * yc, axis=-1, keepdims=True)
            yn = yc * lax.rsqrt(var + EPS) * gnw[:, sl]
            gh = gt[:, sl]
            o_ref[rs, sl] = (gh * jax.nn.sigmoid(gh) * yn).astype(o_ref.dtype)

    _norm_rows(ys_ref, RC, emit, unroll=8)


def _retention(p_qk, p_vg, decay_f, decay_b, gn_w):
    nch = M // RC
    nlat = S // RC
    qw, vw = RH * RDK, RH * RDV
    fwd = lambda s: ((s + nlat) % nch)
    bwd = lambda s: (nch - 1 - s)
    smem = pl.BlockSpec(memory_space=pltpu.SMEM)

    def specs(cm):
        return [smem,
                pl.BlockSpec((RC, qw), lambda s: (cm(s), 0)),
                pl.BlockSpec((RC, qw), lambda s: (cm(s), 1)),
                pl.BlockSpec((RC, vw), lambda s: (cm(s), 0))]

    scratch = [pltpu.VMEM((RH, RDK, RDV), F32), pltpu.VMEM((RH, RC, RC), F32),
               pltpu.VMEM((RH, RC, 1), F32), pltpu.VMEM((RH, RC, 1), F32)]
    yf = pl.pallas_call(
        _ret_fwd_kernel,
        grid=(nch,),
        in_specs=specs(fwd),
        out_specs=pl.BlockSpec((RC, vw), lambda s: (fwd(s), 0)),
        out_shape=jax.ShapeDtypeStruct((M, vw), F32),
        scratch_shapes=scratch,
        compiler_params=_cparams(("arbitrary",)),
        name="ret_fwd",
    )(decay_f, p_qk, p_qk, p_vg)
    return pl.pallas_call(
        _ret_bwd_kernel,
        grid=(nch,),
        in_specs=specs(bwd) + [pl.BlockSpec((RC, vw), lambda s: (bwd(s), 0)),
                               pl.BlockSpec((RC, vw), lambda s: (bwd(s), 1)),
                               pl.BlockSpec((1, vw), lambda s: (0, 0))],
        out_specs=pl.BlockSpec((RC, vw), lambda s: (bwd(s), 0)),
        out_shape=jax.ShapeDtypeStruct((M, vw), BF16),
        scratch_shapes=scratch + [pltpu.VMEM((RC, vw), F32)],
        compiler_params=_cparams(("arbitrary",)),
        name="ret_bwd",
    )(decay_b, p_qk, p_qk, p_vg, yf, p_vg, gn_w.reshape(1, vw))


_NA_NBLK = S // NQB
_NA_CASES = (0, 1, _NA_NBLK - 1)


def _na_window_row(b):
    return jnp.clip(NQROWS * b - NROWS // 2, 0, GROWS - NWIN_ROWS)


def _na_table_kernel(rpb_ref, o_ref):
    h = pl.program_id(0)
    n_dr, n_dc = 2 * NROWS - 1, 2 * NCOLS - 1
    cq = lax.broadcasted_iota(jnp.int32, (GW, 2 * GW), 0)
    lane = lax.broadcasted_iota(jnp.int32, (GW, 2 * GW), 1)
    right = lane >= GW
    kc = jnp.where(right, lane - GW, lane)
    dc = kc - cq + (NCOLS - 1)
    cs = jnp.clip(cq - NCOLS // 2, 0, GW - NCOLS)
    col_ok = (kc >= cs) & (kc < cs + NCOLS)
    memo = {}

    def row_scalar(a, b):
        return rpb_ref[(h * n_dr + a) * n_dc + b] if 0 <= a < n_dr else 0.0

    def pair(a0):
        if a0 not in memo:
            acc = jnp.zeros((GW, 2 * GW), F32)
            for b in range(n_dc):
                vec = jnp.where(right, row_scalar(a0 + 1, b), row_scalar(a0, b))
                acc = jnp.where(dc == b, vec, acc)
            memo[a0] = acc
        return memo[a0]

    for ci, blk in enumerate(_NA_CASES):
        w = int(np.clip(NQROWS * blk - NROWS // 2, 0, GROWS - NWIN_ROWS))
        for rl in range(NQROWS):
            r = NQROWS * blk + rl
            rs = int(np.clip(r - NROWS // 2, 0, GROWS - NROWS))
            for pr in range(NWIN_ROWS // 2):
                kr = w + 2 * pr
                ok_l = rs <= kr < rs + NROWS
                ok_r = rs <= kr + 1 < rs + NROWS
                if ok_l and ok_r:
                    ok = col_ok
                elif ok_l:
                    ok = col_ok & jnp.logical_not(right)
                elif ok_r:
                    ok = col_ok & right
                else:
                    ok = None
                if ok is None:
                    tile = jnp.full((GW, 2 * GW), NEG, F32)
                else:
                    tile = jnp.where(ok, pair(kr - r + NROWS - 1) * LOG2E, NEG)
                o_ref[ci, 0, rl * GW:(rl + 1) * GW, pr * 2 * GW:(pr + 1) * 2 * GW] = tile


def _na_tables(rpb):
    return pl.pallas_call(
        _na_table_kernel,
        grid=(NH,),
        in_specs=[pl.BlockSpec(memory_space=pltpu.SMEM)],
        out_specs=pl.BlockSpec((len(_NA_CASES), 1, NQB, NWIN), lambda h: (0, h, 0, 0)),
        out_shape=jax.ShapeDtypeStruct((len(_NA_CASES), NH, NQB, NWIN), F32),
        compiler_params=_cparams(("arbitrary",)),
        name="na_tables",
    )(rpb.reshape(-1))


def _na_kernel(q_ref, k_ref, v_ref, t_ref, o_ref):
    b = pl.program_id(1)
    scale2 = (HD ** -0.5) * LOG2E
    start = pl.multiple_of(_na_window_row(b) * GW, NQB)
    nt = (((1,), (1,)), ((), ()))
    for hh in range(NHPS):
        cs = slice(hh * HD, (hh + 1) * HD)
        q = q_ref[:, cs]
        s_loc = lax.dot_general(q, k_ref[pl.ds(start, NWIN), cs], nt,
                                preferred_element_type=F32) * scale2 + t_ref[0, hh]
        s_ctx = lax.dot_general(q, k_ref[S:M, cs], nt, preferred_element_type=F32) * scale2
        m = jnp.maximum(jnp.max(s_loc, axis=-1, keepdims=True), jnp.max(s_ctx, axis=-1, keepdims=True))
        e_loc = jnp.exp2(s_loc - m)
        e_ctx = jnp.exp2(s_ctx - m)
        den = jnp.sum(e_loc, axis=-1, keepdims=True) + jnp.sum(e_ctx, axis=-1, keepdims=True)
        o = (jnp.dot(e_loc.astype(BF16), v_ref[pl.ds(start, NWIN), cs], preferred_element_type=F32)
             + jnp.dot(e_ctx.astype(BF16), v_ref[S:M, cs], preferred_element_type=F32))
        o_ref[:, cs] = (o / den).astype(o_ref.dtype)


def _na(p, rpb):
    gw = NHPS * HD
    qcol = 0
    kcol = qcol + NH // NHPS
    vcol = kcol + NH // NHPS
    tables = _na_tables(rpb)

    def case(b):
        return jnp.minimum(b, 1) + jnp.maximum(b - (_NA_NBLK - 2), 0)

    return pl.pallas_call(
        _na_kernel,
        grid=(NH // NHPS, _NA_NBLK),
        in_specs=[
            pl.BlockSpec((NQB, gw), lambda h, b: (b, qcol + h)),
            pl.BlockSpec((M, gw), lambda h, b: (0, kcol + h)),
            pl.BlockSpec((M, gw), lambda h, b: (0, vcol + h)),
            pl.BlockSpec((1, NHPS, NQB, NWIN), lambda h, b: (case(b), h, 0, 0)),
        ],
        out_specs=pl.BlockSpec((NQB, gw), lambda h, b: (b, h)),
        out_shape=jax.ShapeDtypeStruct((S, NH * HD), BF16),
        compiler_params=_cparams(("arbitrary", "arbitrary")),
        name="na",
    )(p, p, p, tables)


def _tiles(rows):
    if rows == M:
        return dict(up=(2112, 704), small=528, inp=1056)
    assert rows == S
    return dict(up=(2048, 1024), small=512, inp=1024)


def _ffn_half(h, hn, rows, mods, layer, third, wi, wo, nw_next, next_mod, cast_jobs=()):
    t = _tiles(rows)
    a, wob = _mm_swiglu(hn, wi, wo, layer, tm=t["up"][0], mm=t["up"][1], tn=512)
    return _ffn_down(h, rows, mods, layer, 3 * third + 2, a, wob, nw_next, next_mod, cast_jobs)


IN_TN = 1536


def kernel(x, c, ctx, c_ctx, ada_w, ada_b, norm_w, ffn_a_wi, ffn_a_wo, ffn_b_wi, ffn_b_wo,
           ev_w_in, ev_w_out, ev_sink, ev_conv_w, ev_conv_b,
           od_w_in, od_w_out, od_decay_f, od_decay_b, od_gn_w, od_rpb, final_norm_w):
    assert x.shape == (1, S, D) and ctx.shape == (1, LC, D) and ada_w.shape[0] == 2
    assert ffn_a_wi.shape == (2, D, 2 * DFF) and ev_w_in.shape == (1, D, EV_IN)
    assert od_w_in.shape == (1, D, OD_IN) and RDK == HD
    cvec = jnp.concatenate([c, c_ctx[None, :], jnp.zeros((6, D), F32)], axis=0)
    mods = _ada(cvec, ada_w, ada_b)
    tabs = _rope_tables()

    tm_in = _tiles(M)["inp"]
    hn = _normmod_first(x[0], ctx[0], norm_w[0, 0], mods)
    (h, hn), (w_in, w_out) = _ffn_half((x[0], ctx[0]), hn, M, mods, 0, 0, ffn_a_wi, ffn_a_wo,
                                       norm_w[0, 1], (0, 1), cast_jobs=((ev_w_in, 0), (ev_w_out, 0)))
    n_att = (AH + 2 * AKV) * HD
    p_att = _mm_rope(hn, w_in, n_att, tabs, AH + AKV, tm=tm_in)
    p_conv = _mm_plain(hn, w_in, n_att, EV_IN - n_att, tm=tm_in, tn=IN_TN, out_dtype=F32)
    att = _attn(p_att, ev_sink[0])
    cnv = _conv(p_conv, ev_conv_w[0], ev_conv_b[0])
    h, hn = _out_proj(h, M, mods, 0, att, cnv, w_out, norm_w[0, 2], tm=_tiles(M)["small"])
    (h, hn), _ = _ffn_half(h, hn, M, mods, 0, 2, ffn_b_wi, ffn_b_wo, norm_w[1, 0], (1, 0))

    (h, hn), (w_in, w_out) = _ffn_half(h, hn, M, mods, 1, 0, ffn_a_wi, ffn_a_wo, norm_w[1, 1], (1, 1),
                                       cast_jobs=((od_w_in, 0), (od_w_out, 0)))
    n_qk, n_vg = 2 * RH * RDK, 2 * RH * RDV
    p_qk = _mm_rope(hn, w_in, n_qk, tabs, 2 * RH, tm=tm_in, scaled=(RH, 2 * RH))
    p_vg = _mm_plain(hn, w_in, n_qk, n_vg, tm=tm_in, tn=n_qk, out_dtype=F32)
    n_na = OD_IN - n_qk - n_vg
    p_na = _mm_plain(hn, w_in, n_qk + n_vg, n_na, tm=tm_in, tn=n_na, out_dtype=BF16)
    ret = _retention(p_qk, p_vg, od_decay_f[0], od_decay_b[0], od_gn_w[0])
    nat = _na(p_na, od_rpb[0])
    h, hn = _out_proj(h, S, mods, 1, ret, nat, w_out, norm_w[1, 2], tm=_tiles(S)["small"])
    (out,), _ = _ffn_half(h, hn, S, mods, 1, 2, ffn_b_wi, ffn_b_wo, final_norm_w, None)
    return out[None]
```

```python
import functools

import numpy as np
import jax
import jax.numpy as jnp
from jax import lax
from jax.experimental import pallas as pl
from jax.experimental.pallas import tpu as pltpu

F32 = jnp.float32
BF16 = jnp.bfloat16

D = 2048
S = 8192
LC = 256
M = S + LC
GW = 64
GROWS = S // GW
HD = 128
DFF = 5632
NMOD = 9
EPS = 1e-6
ROPE_BASE = 10000.0
NEG = -1e30

AH, AKV, AWIN = 8, 2, 128
AQB = 256
CCH = 1024
EV_IN = AH * HD + 2 * AKV * HD + 3 * CCH
RH, RDK, RDV = 4, 128, 256
RC = 256
NH, NROWS, NCOLS = 8, 8, 16
NQROWS = 4
NQB = NQROWS * GW
NWIN_ROWS = NQROWS + NROWS
NWIN = NWIN_ROWS * GW
NHPS = 8
LOG2E = float(np.log2(np.e))
OD_IN = 2 * RH * RDK + 2 * RH * RDV + 3 * NH * HD

VMEM_LIMIT = 56 * 1024 * 1024


def _cparams(sem):
    return pltpu.CompilerParams(dimension_semantics=sem, vmem_limit_bytes=VMEM_LIMIT)


def _sub(r, size):
    return pl.ds(pl.multiple_of(r * size, 16), size)


def _ada_kernel(c_ref, w_ref, b_ref, o_ref):
    cv = c_ref[...]
    a = cv * jax.nn.sigmoid(cv)
    acc = jnp.dot(a.astype(BF16), w_ref[0].astype(BF16), preferred_element_type=F32)
    o_ref[0, 0] = acc + b_ref[0]


def _ada(cvec, ada_w, ada_b):
    depth = ada_w.shape[0]
    tn = D
    per = D // tn
    return pl.pallas_call(
        _ada_kernel,
        grid=(depth, NMOD * per),
        in_specs=[
            pl.BlockSpec((8, D), lambda l, j: (0, 0)),
            pl.BlockSpec((1, D, tn), lambda l, j: (l, 0, j)),
            pl.BlockSpec((1, 1, tn), lambda l, j: (l, 0, j)),
        ],
        out_specs=pl.BlockSpec((1, 1, 8, tn), lambda l, j: (l, j // per, 0, j % per)),
        out_shape=jax.ShapeDtypeStruct((depth, NMOD, 8, D), F32),
        compiler_params=_cparams(("arbitrary", "arbitrary")),
        name="ada",
    )(cvec, ada_w, ada_b.reshape(depth, 1, NMOD * D))


NORM_TR = 1024
NORM_CHUNK = 16
NORM_UNROLL = 8


def _norm_rows(src_ref, n_rows, fn, unroll=NORM_UNROLL):
    group = NORM_CHUNK * unroll

    def outer(gi, carry):
        for u in range(unroll):
            start = pl.multiple_of(gi * group + u * NORM_CHUNK, NORM_CHUNK)
            rs = pl.ds(start, NORM_CHUNK)
            fn(rs, src_ref[rs, :], start)
        return carry

    lax.fori_loop(0, n_rows // group, outer, 0)


def _normmod_first_kernel(x_ref, c_ref, nw_ref, mod_ref, o_ref):
    i = pl.program_id(0)
    kind = (i * NORM_TR >= S).astype(jnp.int32)
    n_rows = jnp.minimum(M - i * NORM_TR, NORM_TR)
    shift = mod_ref[0, 0, pl.ds(kind, 1), :]
    gain = nw_ref[...] * (1.0 + mod_ref[0, 1, pl.ds(kind, 1), :])

    def emit(rs, x, start):
        rinv = lax.rsqrt(jnp.mean(x * x, axis=-1, keepdims=True) + EPS)
        o_ref[rs, :] = ((x * rinv) * gain + shift).astype(o_ref.dtype)

    @pl.when(kind == 0)
    def _():
        _norm_rows(x_ref, n_rows, emit)

    @pl.when(kind == 1)
    def _():
        _norm_rows(c_ref, n_rows, emit)


def _normmod_first(x, ctx, nw, mods):
    assert S % NORM_TR == 0 and LC % (NORM_CHUNK * NORM_UNROLL) == 0 and LC <= NORM_TR
    return pl.pallas_call(
        _normmod_first_kernel,
        grid=(pl.cdiv(M, NORM_TR),),
        in_specs=[pl.BlockSpec((NORM_TR, D), lambda i: (jnp.minimum(i, S // NORM_TR - 1), 0)),
                  pl.BlockSpec((LC, D), lambda i: (0, 0)),
                  pl.BlockSpec((1, D), lambda i: (0, 0)),
                  pl.BlockSpec((1, 3, 8, D), lambda i: (0, 0, 0, 0))],
        out_specs=pl.BlockSpec((NORM_TR, D), lambda i: (i, 0)),
        out_shape=jax.ShapeDtypeStruct((M, D), BF16),
        compiler_params=_cparams(("arbitrary",)),
        name="normmod_first",
    )(x, ctx, nw.reshape(1, D), mods)


def _cast_weights(w_refs, wb_refs):
    @pl.when(pl.program_id(1) == 0)
    def _():
        for w_ref, wb_ref in zip(w_refs, wb_refs):
            wb_ref[...] = w_ref[...].astype(BF16)


def _mm_swiglu_kernel(a_ref, wg_ref, wu_ref, wo_ref, o_ref, wob_ref, wgb_ref, wub_ref, *, tm, mm):
    _cast_weights((wg_ref, wu_ref), (wgb_ref, wub_ref))
    wob_ref[...] = wo_ref[...].astype(BF16)

    def body(r, carry):
        a = a_ref[_sub(r, mm), :]
        g = jnp.dot(a, wgb_ref[...], preferred_element_type=F32)
        u = jnp.dot(a, wub_ref[...], preferred_element_type=F32)
        o_ref[_sub(r, mm), :] = (g * jax.nn.sigmoid(g) * u).astype(o_ref.dtype)
        return carry

    lax.fori_loop(0, tm // mm, body, 0, unroll=True)


def _mm_plain_kernel(a_ref, w_ref, o_ref):
    o_ref[...] = jnp.dot(a_ref[...], w_ref[...], preferred_element_type=F32).astype(o_ref.dtype)


def _mm_swiglu(a, wi, wo, layer, *, tm, mm, tn):
    rows, k = a.shape
    assert rows % tm == 0 and tm % mm == 0 and mm % 16 == 0
    n_out = wi.shape[2] // 2
    nj, ni = n_out // tn, rows // tm
    assert wo.shape[1] % (nj * ni) == 0
    slab = wo.shape[1] // (nj * ni)
    assert slab % 16 == 0
    return pl.pallas_call(
        functools.partial(_mm_swiglu_kernel, tm=tm, mm=mm),
        grid=(nj, ni),
        in_specs=[pl.BlockSpec((tm, k), lambda j, i: (i, 0)),
                  pl.BlockSpec((None, k, tn), lambda j, i: (layer, 0, j)),
                  pl.BlockSpec((None, k, tn), lambda j, i: (layer, 0, j + nj)),
                  pl.BlockSpec((None, slab, D), lambda j, i: (layer, j * ni + i, 0))],
        out_specs=[pl.BlockSpec((tm, tn), lambda j, i: (i, j)),
                   pl.BlockSpec((slab, D), lambda j, i: (j * ni + i, 0))],
        out_shape=[jax.ShapeDtypeStruct((rows, n_out), BF16),
                   jax.ShapeDtypeStruct((wo.shape[1], D), BF16)],
        scratch_shapes=[pltpu.VMEM((k, tn), BF16)] * 2,
        compiler_params=_cparams(("arbitrary", "arbitrary")),
        name="mm_swiglu",
    )(a, wi, wi, wo)


def _mm_plain(a, wb, col0, ncols, *, tm, tn, out_dtype):
    rows, k = a.shape
    assert rows % tm == 0 and ncols % tn == 0 and col0 % tn == 0
    return pl.pallas_call(
        _mm_plain_kernel,
        grid=(ncols // tn, rows // tm),
        in_specs=[pl.BlockSpec((tm, k), lambda j, i: (i, 0)),
                  pl.BlockSpec((k, tn), lambda j, i: (0, col0 // tn + j))],
        out_specs=pl.BlockSpec((tm, tn), lambda j, i: (i, j)),
        out_shape=jax.ShapeDtypeStruct((rows, ncols), out_dtype),
        compiler_params=_cparams(("arbitrary", "arbitrary")),
        name="mm_plain",
    )(a, wb)


def _gate_rows(gate_ref, i, tm):
    rows = i * tm + lax.broadcasted_iota(jnp.int32, (tm, 1), 0)
    return jnp.where(rows >= S, gate_ref[0, 0, 1:2, :], gate_ref[0, 0, 0:1, :])


DOWN_TM = 256


def _ffn_down_kernel(*refs, n_jobs, final, split):
    it = iter(refs)
    h_ref = next(it)
    c_ref = next(it) if split else None
    gate_ref, a_ref, w_ref, nw_ref = next(it), next(it), next(it), next(it)
    mod_ref = None if final else next(it)
    job_in = [next(it) for _ in range(n_jobs)]
    o_ref = next(it)
    hn_ref = None if final else next(it)
    job_out = [next(it) for _ in range(n_jobs)]
    new_ref = next(it) if final else o_ref
    for src_ref, dst_ref in zip(job_in, job_out):
        dst_ref[...] = src_ref[...].astype(BF16)
    kind = (pl.program_id(0) * DOWN_TM >= S).astype(jnp.int32)
    acc = jnp.dot(a_ref[...], w_ref[...], preferred_element_type=F32)
    res = jnp.where(kind == 1, c_ref[...], h_ref[...]) if split else h_ref[...]
    new_ref[...] = res + (0.5 * gate_ref[0, 0, pl.ds(kind, 1), :]) * acc
    nw = nw_ref[...]
    if final:
        def emit(rs, x, start):
            rinv = lax.rsqrt(jnp.mean(x * x, axis=-1, keepdims=True) + EPS)
            o_ref[rs, :] = x * rinv * nw
    else:
        shift = mod_ref[0, 0, pl.ds(kind, 1), :]
        gain = nw * (1.0 + mod_ref[0, 1, pl.ds(kind, 1), :])

        def emit(rs, x, start):
            rinv = lax.rsqrt(jnp.mean(x * x, axis=-1, keepdims=True) + EPS)
            hn_ref[rs, :] = ((x * rinv) * gain + shift).astype(hn_ref.dtype)

    _norm_rows(new_ref, DOWN_TM, emit)


def _ffn_down(h, rows, mods, layer, gate_idx, a, wb, nw_next, next_mod=None, cast_jobs=()):
    k = a.shape[1]
    final = next_mod is None
    split = isinstance(h, tuple)
    tm = DOWN_TM
    assert rows % tm == 0 and S % tm == 0 and tm % (NORM_CHUNK * NORM_UNROLL) == 0
    assert not split or (rows == M and LC == tm)
    n_slabs = S // tm
    row = lambda i: (i, 0)
    job_in, job_out, job_shape = [], [], []
    for w, idx in cast_jobs:
        kk, nn = w.shape[1:]
        assert kk % n_slabs == 0 and (kk // n_slabs) % 16 == 0
        slab = kk // n_slabs
        job_in.append(pl.BlockSpec((None, slab, nn),
                                   lambda i, idx=idx: (idx, jnp.minimum(i, n_slabs - 1), 0)))
        job_out.append(pl.BlockSpec((slab, nn), lambda i: (jnp.minimum(i, n_slabs - 1), 0)))
        job_shape.append(jax.ShapeDtypeStruct((kk, nn), BF16))
    if split:
        h_specs = [pl.BlockSpec((tm, D), lambda i: (jnp.minimum(i, n_slabs - 1), 0)),
                   pl.BlockSpec((tm, D), lambda i: (0, 0))]
        h_args = list(h)
    else:
        h_specs, h_args = [pl.BlockSpec((tm, D), row)], [h]
    in_specs = h_specs + [pl.BlockSpec((1, 1, 8, D), lambda i: (layer, gate_idx, 0, 0)),
                          pl.BlockSpec((tm, k), row),
                          pl.BlockSpec((k, D), lambda i: (0, 0), pipeline_mode=pl.Buffered(1)),
                          pl.BlockSpec((1, D), lambda i: (0, 0))]
    args = h_args + [mods, a, wb, nw_next.reshape(1, D)]
    out_specs = [pl.BlockSpec((tm, D), row)]
    out_shape = [jax.ShapeDtypeStruct((rows, D), F32)]
    if not final:
        in_specs.append(pl.BlockSpec((1, 3, 8, D), lambda i: (next_mod[0], next_mod[1], 0, 0)))
        args.append(mods)
        out_specs.append(pl.BlockSpec((tm, D), row))
        out_shape.append(jax.ShapeDtypeStruct((rows, D), BF16))
    n_main = len(out_shape)
    res = pl.pallas_call(
        functools.partial(_ffn_down_kernel, n_jobs=len(cast_jobs), final=final, split=split),
        grid=(rows // tm,),
        in_specs=in_specs + job_in,
        out_specs=out_specs + job_out,
        out_shape=out_shape + job_shape,
        scratch_shapes=[pltpu.VMEM((tm, D), F32)] if final else [],
        compiler_params=_cparams(("arbitrary",)),
        name="ffn_down_final" if final else "ffn_down",
    )(*args, *[w for w, _ in cast_jobs])
    return tuple(res[:n_main]), list(res[n_main:])


def _out_proj_kernel(h_ref, gate_ref, a1_ref, a2_ref, w1_ref, w2_ref, nw_ref, mod_ref,
                     o_ref, hn_ref, *, tm, unroll, has_ctx):
    i = pl.program_id(0)
    acc = (jnp.dot(a1_ref[...], w1_ref[...], preferred_element_type=F32)
           + jnp.dot(a2_ref[...], w2_ref[...], preferred_element_type=F32))
    gate = _gate_rows(gate_ref, i, tm) if has_ctx else gate_ref[0, 0, 0:1, :]
    o_ref[...] = h_ref[...] + gate * acc
    nw = nw_ref[...]
    gain_lat = nw * (1.0 + mod_ref[0, 1, 0:1, :])
    gain_ctx = nw * (1.0 + mod_ref[0, 1, 1:2, :])
    shift_lat = mod_ref[0, 0, 0:1, :]
    shift_ctx = mod_ref[0, 0, 1:2, :]

    def emit(rs, x, start):
        rinv = lax.rsqrt(jnp.mean(x * x, axis=-1, keepdims=True) + EPS)
        gain, shift = gain_lat, shift_lat
        if has_ctx:
            isc = (i * tm + start + lax.broadcasted_iota(jnp.int32, (NORM_CHUNK, 1), 0)) >= S
            gain = jnp.where(isc, gain_ctx, gain_lat)
            shift = jnp.where(isc, shift_ctx, shift_lat)
        hn_ref[rs, :] = ((x * rinv) * gain + shift).astype(hn_ref.dtype)

    _norm_rows(o_ref, tm, emit, unroll=unroll)


def _out_proj(h, rows, mods, layer, a1, a2, wb, nw_next, *, tm):
    k = a1.shape[1]
    assert a2.shape[1] == k and wb.shape == (2 * k, D) and rows % tm == 0
    unroll = max(u for u in range(1, 13) if (tm // NORM_CHUNK) % u == 0)
    row = lambda i: (i, 0)
    return pl.pallas_call(
        functools.partial(_out_proj_kernel, tm=tm, unroll=unroll, has_ctx=rows > S),
        grid=(rows // tm,),
        in_specs=[pl.BlockSpec((tm, D), row),
                  pl.BlockSpec((1, 1, 8, D), lambda i: (layer, 5, 0, 0)),
                  pl.BlockSpec((tm, k), row),
                  pl.BlockSpec((tm, k), row),
                  pl.BlockSpec((k, D), lambda i: (0, 0)),
                  pl.BlockSpec((k, D), lambda i: (1, 0)),
                  pl.BlockSpec((1, D), lambda i: (0, 0)),
                  pl.BlockSpec((1, 3, 8, D), lambda i: (layer, 2, 0, 0))],
        out_specs=[pl.BlockSpec((tm, D), row), pl.BlockSpec((tm, D), row)],
        out_shape=[jax.ShapeDtypeStruct((rows, D), F32), jax.ShapeDtypeStruct((rows, D), BF16)],
        compiler_params=_cparams(("arbitrary",)),
        name="out_proj",
    )(h, mods, a1, a2, wb, wb, nw_next.reshape(1, D), mods)


def _rope_tables():
    t = np.arange(S)
    rows = (t // GW).astype(np.float32)
    cols = (t % GW).astype(np.float32)
    half = HD // 2
    inv = np.float32(ROPE_BASE) ** (-np.arange(0, half, 2, dtype=np.float32) / np.float32(half))
    ar = rows[:, None] * inv
    ac = cols[:, None] * inv
    cos = np.concatenate([np.cos(ar), np.cos(ar), np.cos(ac), np.cos(ac)], axis=-1)
    sin = np.concatenate([-np.sin(ar), np.sin(ar), -np.sin(ac), np.sin(ac)], axis=-1)
    lat = np.concatenate([cos, sin], axis=-1)
    ctx = np.concatenate([np.ones((LC, HD), np.float32), np.zeros((LC, HD), np.float32)], axis=-1)
    return jnp.asarray(np.concatenate([lat, ctx], axis=0).astype(np.float32))


def _rope(x, tab):
    c = tab[:, :HD]
    sg = tab[:, HD:]
    lane = lax.broadcasted_iota(jnp.int32, x.shape, 1)
    first = (lane % (HD // 2)) < (HD // 4)
    partner = jnp.where(first, pltpu.roll(x, HD - HD // 4, 1), pltpu.roll(x, HD // 4, 1))
    return x * c + partner * sg


def _mm_rope_kernel(a_ref, w_ref, tab_ref, o_ref, *, n_rot, scaled):
    acc = jnp.dot(a_ref[...], w_ref[...], preferred_element_type=F32)
    tab = tab_ref[...]
    for hh in range(acc.shape[1] // HD):
        x = acc[:, hh * HD:(hh + 1) * HD]
        if hh < n_rot:
            x = _rope(x, tab)
        if scaled[0] <= hh < scaled[1]:
            x = x * (HD ** -0.5)
        o_ref[:, hh * HD:(hh + 1) * HD] = x.astype(o_ref.dtype)


def _mm_rope(a, wb, ncols, tabs, n_rot, *, tm, scaled=(0, 0)):
    rows, k = a.shape
    assert rows % tm == 0 and ncols % HD == 0
    return pl.pallas_call(
        functools.partial(_mm_rope_kernel, n_rot=n_rot, scaled=scaled),
        grid=(rows // tm,),
        in_specs=[pl.BlockSpec((tm, k), lambda i: (i, 0)),
                  pl.BlockSpec((k, ncols), lambda i: (0, 0)),
                  pl.BlockSpec((tm, 2 * HD), lambda i: (i, 0))],
        out_specs=pl.BlockSpec((tm, ncols), lambda i: (i, 0)),
        out_shape=jax.ShapeDtypeStruct((rows, ncols), BF16),
        compiler_params=_cparams(("arbitrary",)),
        name="mm_rope",
    )(a, wb, tabs)


def _attn_kernel(sink_ref, q_ref, kp_ref, kc_ref, kn_ref, vp_ref, vc_ref, vn_ref, kx_ref, vx_ref, o_ref):
    n = pl.program_id(0)
    nlat = S // AQB
    scale2 = (HD ** -0.5) * LOG2E
    k_all = jnp.concatenate([kp_ref[...], kc_ref[...], kn_ref[...], kx_ref[...]], axis=0)
    v_all = jnp.concatenate([vp_ref[...], vc_ref[...], vn_ref[...], vx_ref[...]], axis=0)
    nloc = AQB + 2 * AWIN
    nk = nloc + LC
    r = lax.broadcasted_iota(jnp.int32, (AQB, nk), 0)
    c = lax.broadcasted_iota(jnp.int32, (AQB, nk), 1)
    rel = c - AWIN - r
    pos = n * AQB - AWIN + c
    hi = jnp.where(n < nlat, S, 0)
    ok = ((jnp.abs(rel) <= AWIN) & (pos >= 0) & (pos < hi)) | (c >= nloc)
    for hh in range(AH):
        gs = slice((hh // (AH // AKV)) * HD, (hh // (AH // AKV) + 1) * HD)
        q = q_ref[:, hh * HD:(hh + 1) * HD]
        s = lax.dot_general(q, k_all[:, gs], (((1,), (1,)), ((), ())),
                            preferred_element_type=F32) * scale2
        s = jnp.where(ok, s, NEG)
        sk = sink_ref[hh] * LOG2E
        m = jnp.maximum(jnp.max(s, axis=-1, keepdims=True), sk)
        e = jnp.exp2(s - m)
        den = jnp.sum(e, axis=-1, keepdims=True) + jnp.exp2(sk - m)
        o = jnp.dot(e.astype(BF16), v_all[:, gs], preferred_element_type=F32)
        o_ref[:, hh * HD:(hh + 1) * HD] = (o / den).astype(o_ref.dtype)


def _attn(p, sink):
    assert AQB == LC and AQB == 2 * AWIN
    nlat = S // AQB
    qw, kw = AH * HD, AKV * HD
    kcol = qw // kw
    vcol = kcol + 1
    last_w = S // AWIN - 1

    def prev(n):
        return jnp.clip(2 * n - 1, 0, last_w)

    def cur(n):
        return jnp.minimum(n, nlat - 1)

    def nxt(n):
        return jnp.clip(2 * n + 2, 0, last_w)

    return pl.pallas_call(
        _attn_kernel,
        grid=(M // AQB,),
        in_specs=[
            pl.BlockSpec(memory_space=pltpu.SMEM),
            pl.BlockSpec((AQB, qw), lambda n: (n, 0)),
            pl.BlockSpec((AWIN, kw), lambda n: (prev(n), kcol)),
            pl.BlockSpec((AQB, kw), lambda n: (cur(n), kcol)),
            pl.BlockSpec((AWIN, kw), lambda n: (nxt(n), kcol)),
            pl.BlockSpec((AWIN, kw), lambda n: (prev(n), vcol)),
            pl.BlockSpec((AQB, kw), lambda n: (cur(n), vcol)),
            pl.BlockSpec((AWIN, kw), lambda n: (nxt(n), vcol)),
            pl.BlockSpec((LC, kw), lambda n: (S // LC, kcol)),
            pl.BlockSpec((LC, kw), lambda n: (S // LC, vcol)),
        ],
        out_specs=pl.BlockSpec((AQB, qw), lambda n: (n, 0)),
        out_shape=jax.ShapeDtypeStruct((M, AH * HD), BF16),
        compiler_params=_cparams(("arbitrary",)),
        name="attn",
    )(sink, p, p, p, p, p, p, p, p, p)


def _conv_kernel(b_ref, c_ref, u_ref, cp_ref, up_ref, cn_ref, un_ref, w_ref, bias_ref, o_ref, *, tr):
    z = c_ref[...] * u_ref[...]
    zp = cp_ref[7:8, :] * up_ref[7:8, :]
    zn = cn_ref[0:1, :] * un_ref[0:1, :]
    row = lax.broadcasted_iota(jnp.int32, z.shape, 0)
    g = pl.program_id(0) * tr + row
    z_m1 = jnp.where(row == 0, zp, pltpu.roll(z, 1, 0))
    z_p1 = jnp.where(row == tr - 1, zn, pltpu.roll(z, tr - 1, 0))
    z_m1 = jnp.where((g == 0) | (g == S), 0.0, z_m1)
    z_p1 = jnp.where((g == S - 1) | (g == M - 1), 0.0, z_p1)
    conv = z_m1 * w_ref[0:1, :] + z * w_ref[1:2, :] + z_p1 * w_ref[2:3, :] + bias_ref[...]
    o_ref[...] = (b_ref[...] * conv).astype(o_ref.dtype)


def _conv(p, conv_w, conv_b):
    tr = 1056
    assert M % tr == 0 and tr % 8 == 0
    h8 = tr // 8

    def prev8(i):
        return jnp.maximum(i * h8 - 1, 0)

    def next8(i):
        return jnp.minimum((i + 1) * h8, M // 8 - 1)

    return pl.pallas_call(
        functools.partial(_conv_kernel, tr=tr),
        grid=(M // tr,),
        in_specs=[
            pl.BlockSpec((tr, CCH), lambda i: (i, 0)),
            pl.BlockSpec((tr, CCH), lambda i: (i, 1)),
            pl.BlockSpec((tr, CCH), lambda i: (i, 2)),
            pl.BlockSpec((8, CCH), lambda i: (prev8(i), 1)),
            pl.BlockSpec((8, CCH), lambda i: (prev8(i), 2)),
            pl.BlockSpec((8, CCH), lambda i: (next8(i), 1)),
            pl.BlockSpec((8, CCH), lambda i: (next8(i), 2)),
            pl.BlockSpec((3, CCH), lambda i: (0, 0)),
            pl.BlockSpec((1, CCH), lambda i: (0, 0)),
        ],
        out_specs=pl.BlockSpec((tr, CCH), lambda i: (i, 0)),
        out_shape=jax.ShapeDtypeStruct((M, CCH), BF16),
        compiler_params=_cparams(("arbitrary",)),
        name="conv",
    )(p, p, p, p, p, p, p, conv_w, conv_b.reshape(1, CCH))


def _log_sigmoid(d):
    return jnp.minimum(d, 0.0) - jnp.log(1.0 + jnp.exp(-jnp.abs(d)))


def _ret_first_step(decay_ref, st_ref, dm_ref, xi_ref, zeta_ref, backward):
    @pl.when(pl.program_id(0) == 0)
    def _():
        st_ref[...] = jnp.zeros_like(st_ref)
        ii = lax.broadcasted_iota(jnp.int32, (RC, RC), 0)
        jj = lax.broadcasted_iota(jnp.int32, (RC, RC), 1)
        ic = lax.broadcasted_iota(jnp.int32, (RC, 1), 0).astype(F32)
        for h in range(RH):
            lg = _log_sigmoid(jnp.full((1, 1), decay_ref[h], F32))
            if backward:
                rel = jj - ii
                keep = rel > 0
                xi_ref[h] = jnp.exp(lg * (RC - ic))
                zeta_ref[h] = jnp.exp(lg * ic)
            else:
                rel = ii - jj
                keep = rel >= 0
                xi_ref[h] = jnp.exp(lg * (ic + 1.0))
                zeta_ref[h] = jnp.exp(lg * (RC - 1.0 - ic))
            dm_ref[h] = jnp.where(keep, jnp.exp(lg * jnp.where(keep, rel, 0).astype(F32)), 0.0)


def _ret_chunk(decay_ref, q_ref, k_ref, v_ref, st_ref, dm_ref, xi_ref, zeta_ref, h):
    lg = _log_sigmoid(jnp.full((1, 1), decay_ref[h], F32))
    q = q_ref[:, h * RDK:(h + 1) * RDK]
    k = k_ref[:, h * RDK:(h + 1) * RDK]
    v = v_ref[:, h * RDV:(h + 1) * RDV]
    inner = lax.dot_general(q, k, (((1,), (1,)), ((), ())), preferred_element_type=F32) * dm_ref[h]
    y = jnp.dot(inner.astype(BF16), v.astype(BF16), preferred_element_type=F32)
    st = st_ref[h]
    y = y + jnp.dot(q, st.astype(BF16), preferred_element_type=F32) * xi_ref[h]
    kv = lax.dot_general(k, (v * zeta_ref[h]).astype(BF16), (((0,), (0,)), ((), ())),
                         preferred_element_type=F32)
    st_ref[h] = jnp.exp(lg * float(RC)) * st + kv
    return y


def _ret_fwd_kernel(decay_ref, q_ref, k_ref, v_ref, y_ref, st_ref, dm_ref, xi_ref, zeta_ref):
    _ret_first_step(decay_ref, st_ref, dm_ref, xi_ref, zeta_ref, False)
    for h in range(RH):
        y_ref[:, h * RDV:(h + 1) * RDV] = _ret_chunk(decay_ref, q_ref, k_ref, v_ref, st_ref,
                                                     dm_ref, xi_ref, zeta_ref, h)


def _ret_bwd_kernel(decay_ref, q_ref, k_ref, v_ref, yf_ref, g_ref, gnw_ref, o_ref,
                    st_ref, dm_ref, xi_ref, zeta_ref, ys_ref):
    _ret_first_step(decay_ref, st_ref, dm_ref, xi_ref, zeta_ref, True)
    for h in range(RH):
        sl = slice(h * RDV, (h + 1) * RDV)
        ys_ref[:, sl] = yf_ref[:, sl] + _ret_chunk(decay_ref, q_ref, k_ref, v_ref, st_ref,
                                                   dm_ref, xi_ref, zeta_ref, h)
    gnw = gnw_ref[...]

    def emit(rs, y, start):
        gt = g_ref[rs, :]
        for h in range(RH):
            sl = slice(h * RDV, (h + 1) * RDV)
            yh = y[:, sl]
            yc = yh - jnp.mean(yh, axis=-1, keepdims=True)
            var = jnp.mean(yc * yc, axis=-1, keepdims=True)
            yn = yc * lax.rsqrt(var + EPS) * gnw[:, sl]
            gh = gt[:, sl]
            o_ref[rs, sl] = (gh * jax.nn.sigmoid(gh) * yn).astype(o_ref.dtype)

    _norm_rows(ys_ref, RC, emit, unroll=8)


def _retention(p_qk, p_vg, decay_f, decay_b, gn_w):
    nch = M // RC
    nlat = S // RC
    qw, vw = RH * RDK, RH * RDV
    fwd = lambda s: ((s + nlat) % nch)
    bwd = lambda s: (nch - 1 - s)
    smem = pl.BlockSpec(memory_space=pltpu.SMEM)

    def specs(cm):
        return [smem,
                pl.BlockSpec((RC, qw), lambda s: (cm(s), 0)),
                pl.BlockSpec((RC, qw), lambda s: (cm(s), 1)),
                pl.BlockSpec((RC, vw), lambda s: (cm(s), 0))]

    scratch = [pltpu.VMEM((RH, RDK, RDV), F32), pltpu.VMEM((RH, RC, RC), F32),
               pltpu.VMEM((RH, RC, 1), F32), pltpu.VMEM((RH, RC, 1), F32)]
    yf = pl.pallas_call(
        _ret_fwd_kernel,
        grid=(nch,),
        in_specs=specs(fwd),
        out_specs=pl.BlockSpec((RC, vw), lambda s: (fwd(s), 0)),
        out_shape=jax.ShapeDtypeStruct((M, vw), F32),
        scratch_shapes=scratch,
        compiler_params=_cparams(("arbitrary",)),
        name="ret_fwd",
    )(decay_f, p_qk, p_qk, p_vg)
    return pl.pallas_call(
        _ret_bwd_kernel,
        grid=(nch,),
        in_specs=specs(bwd) + [pl.BlockSpec((RC, vw), lambda s: (bwd(s), 0)),
                               pl.BlockSpec((RC, vw), lambda s: (bwd(s), 1)),
                               pl.BlockSpec((1, vw), lambda s: (0, 0))],
        out_specs=pl.BlockSpec((RC, vw), lambda s: (bwd(s), 0)),
        out_shape=jax.ShapeDtypeStruct((M, vw), BF16),
        scratch_shapes=scratch + [pltpu.VMEM((RC, vw), F32)],
        compiler_params=_cparams(("arbitrary",)),
        name="ret_bwd",
    )(decay_b, p_qk, p_qk, p_vg, yf, p_vg, gn_w.reshape(1, vw))


_NA_NBLK = S // NQB
_NA_CASES = (0, 1, _NA_NBLK - 1)


def _na_window_row(b):
    return jnp.clip(NQROWS * b - NROWS // 2, 0, GROWS - NWIN_ROWS)


def _na_table_kernel(rpb_ref, o_ref):
    h = pl.program_id(0)
    n_dr, n_dc = 2 * NROWS - 1, 2 * NCOLS - 1
    cq = lax.broadcasted_iota(jnp.int32, (GW, 2 * GW), 0)
    lane = lax.broadcasted_iota(jnp.int32, (GW, 2 * GW), 1)
    right = lane >= GW
    kc = jnp.where(right, lane - GW, lane)
    dc = kc - cq + (NCOLS - 1)
    cs = jnp.clip(cq - NCOLS // 2, 0, GW - NCOLS)
    col_ok = (kc >= cs) & (kc < cs + NCOLS)
    memo = {}

    def row_scalar(a, b):
        return rpb_ref[(h * n_dr + a) * n_dc + b] if 0 <= a < n_dr else 0.0

    def pair(a0):
        if a0 not in memo:
            acc = jnp.zeros((GW, 2 * GW), F32)
            for b in range(n_dc):
                vec = jnp.where(right, row_scalar(a0 + 1, b), row_scalar(a0, b))
                acc = jnp.where(dc == b, vec, acc)
            memo[a0] = acc
        return memo[a0]

    for ci, blk in enumerate(_NA_CASES):
        w = int(np.clip(NQROWS * blk - NROWS // 2, 0, GROWS - NWIN_ROWS))
        for rl in range(NQROWS):
            r = NQROWS * blk + rl
            rs = int(np.clip(r - NROWS // 2, 0, GROWS - NROWS))
            for pr in range(NWIN_ROWS // 2):
                kr = w + 2 * pr
                ok_l = rs <= kr < rs + NROWS
                ok_r = rs <= kr + 1 < rs + NROWS
                if ok_l and ok_r:
                    ok = col_ok
                elif ok_l:
                    ok = col_ok & jnp.logical_not(right)
                elif ok_r:
                    ok = col_ok & right
                else:
                    ok = None
                if ok is None:
                    tile = jnp.full((GW, 2 * GW), NEG, F32)
                else:
                    tile = jnp.where(ok, pair(kr - r + NROWS - 1) * LOG2E, NEG)
                o_ref[ci, 0, rl * GW:(rl + 1) * GW, pr * 2 * GW:(pr + 1) * 2 * GW] = tile


def _na_tables(rpb):
    return pl.pallas_call(
        _na_table_kernel,
        grid=(NH,),
        in_specs=[pl.BlockSpec(memory_space=pltpu.SMEM)],
        out_specs=pl.BlockSpec((len(_NA_CASES), 1, NQB, NWIN), lambda h: (0, h, 0, 0)),
        out_shape=jax.ShapeDtypeStruct((len(_NA_CASES), NH, NQB, NWIN), F32),
        compiler_params=_cparams(("arbitrary",)),
        name="na_tables",
    )(rpb.reshape(-1))


def _na_kernel(q_ref, k_ref, v_ref, t_ref, o_ref):
    b = pl.program_id(1)
    scale2 = (HD ** -0.5) * LOG2E
    start = pl.multiple_of(_na_window_row(b) * GW, NQB)
    nt = (((1,), (1,)), ((), ()))
    for hh in range(NHPS):
        cs = slice(hh * HD, (hh + 1) * HD)
        q = q_ref[:, cs]
        s_loc = lax.dot_general(q, k_ref[pl.ds(start, NWIN), cs], nt,
                                preferred_element_type=F32) * scale2 + t_ref[0, hh]
        s_ctx = lax.dot_general(q, k_ref[S:M, cs], nt, preferred_element_type=F32) * scale2
        m = jnp.maximum(jnp.max(s_loc, axis=-1, keepdims=True), jnp.max(s_ctx, axis=-1, keepdims=True))
        e_loc = jnp.exp2(s_loc - m)
        e_ctx = jnp.exp2(s_ctx - m)
        den = jnp.sum(e_loc, axis=-1, keepdims=True) + jnp.sum(e_ctx, axis=-1, keepdims=True)
        o = (jnp.dot(e_loc.astype(BF16), v_ref[pl.ds(start, NWIN), cs], preferred_element_type=F32)
             + jnp.dot(e_ctx.astype(BF16), v_ref[S:M, cs], preferred_element_type=F32))
        o_ref[:, cs] = (o / den).astype(o_ref.dtype)


def _na(p, rpb):
    gw = NHPS * HD
    qcol = 0
    kcol = qcol + NH // NHPS
    vcol = kcol + NH // NHPS
    tables = _na_tables(rpb)

    def case(b):
        return jnp.minimum(b, 1) + jnp.maximum(b - (_NA_NBLK - 2), 0)

    return pl.pallas_call(
        _na_kernel,
        grid=(NH // NHPS, _NA_NBLK),
        in_specs=[
            pl.BlockSpec((NQB, gw), lambda h, b: (b, qcol + h)),
            pl.BlockSpec((M, gw), lambda h, b: (0, kcol + h), pipeline_mode=pl.Buffered(1)),
            pl.BlockSpec((M, gw), lambda h, b: (0, vcol + h), pipeline_mode=pl.Buffered(1)),
            pl.BlockSpec((1, NHPS, NQB, NWIN), lambda h, b: (case(b), h, 0, 0)),
        ],
        out_specs=pl.BlockSpec((NQB, gw), lambda h, b: (b, h)),
        out_shape=jax.ShapeDtypeStruct((S, NH * HD), BF16),
        compiler_params=_cparams(("arbitrary", "arbitrary")),
        name="na",
    )(p, p, p, tables)


def _tiles(rows):
    if rows == M:
        return dict(up=(2112, 704), small=528, inp=1056)
    assert rows == S
    return dict(up=(2048, 1024), small=512, inp=1024)


def _ffn_half(h, hn, rows, mods, layer, third, wi, wo, nw_next, next_mod, cast_jobs=()):
    t = _tiles(rows)
    a, wob = _mm_swiglu(hn, wi, wo, layer, tm=t["up"][0], mm=t["up"][1], tn=512)
    return _ffn_down(h, rows, mods, layer, 3 * third + 2, a, wob, nw_next, next_mod, cast_jobs)


IN_TN = 1536


def kernel(x, c, ctx, c_ctx, ada_w, ada_b, norm_w, ffn_a_wi, ffn_a_wo, ffn_b_wi, ffn_b_wo,
           ev_w_in, ev_w_out, ev_sink, ev_conv_w, ev_conv_b,
           od_w_in, od_w_out, od_decay_f, od_decay_b, od_gn_w, od_rpb, final_norm_w):
    assert x.shape == (1, S, D) and ctx.shape == (1, LC, D) and ada_w.shape[0] == 2
    assert ffn_a_wi.shape == (2, D, 2 * DFF) and ev_w_in.shape == (1, D, EV_IN)
    assert od_w_in.shape == (1, D, OD_IN) and RDK == HD
    cvec = jnp.concatenate([c, c_ctx[None, :], jnp.zeros((6, D), F32)], axis=0)
    mods = _ada(cvec, ada_w, ada_b)
    tabs = _rope_tables()

    tm_in = _tiles(M)["inp"]
    hn = _normmod_first(x[0], ctx[0], norm_w[0, 0], mods)
    (h, hn), (w_in, w_out) = _ffn_half((x[0], ctx[0]), hn, M, mods, 0, 0, ffn_a_wi, ffn_a_wo,
                                       norm_w[0, 1], (0, 1), cast_jobs=((ev_w_in, 0), (ev_w_out, 0)))
    n_att = (AH + 2 * AKV) * HD
    p_att = _mm_rope(hn, w_in, n_att, tabs, AH + AKV, tm=tm_in)
    p_conv = _mm_plain(hn, w_in, n_att, EV_IN - n_att, tm=tm_in, tn=IN_TN, out_dtype=F32)
    att = _attn(p_att, ev_sink[0])
    cnv = _conv(p_conv, ev_conv_w[0], ev_conv_b[0])
    h, hn = _out_proj(h, M, mods, 0, att, cnv, w_out, norm_w[0, 2], tm=_tiles(M)["small"])
    (h, hn), _ = _ffn_half(h, hn, M, mods, 0, 2, ffn_b_wi, ffn_b_wo, norm_w[1, 0], (1, 0))

    (h, hn), (w_in, w_out) = _ffn_half(h, hn, M, mods, 1, 0, ffn_a_wi, ffn_a_wo, norm_w[1, 1], (1, 1),
                                       cast_jobs=((od_w_in, 0), (od_w_out, 0)))
    n_qk, n_vg = 2 * RH * RDK, 2 * RH * RDV
    p_qk = _mm_rope(hn, w_in, n_qk, tabs, 2 * RH, tm=tm_in, scaled=(RH, 2 * RH))
    p_vg = _mm_plain(hn, w_in, n_qk, n_vg, tm=tm_in, tn=n_qk, out_dtype=F32)
    n_na = OD_IN - n_qk - n_vg
    p_na = _mm_plain(hn, w_in, n_qk + n_vg, n_na, tm=tm_in, tn=n_na, out_dtype=BF16)
    ret = _retention(p_qk, p_vg, od_decay_f[0], od_decay_b[0], od_gn_w[0])
    nat = _na(p_na, od_rpb[0])
    h, hn = _out_proj(h, S, mods, 1, ret, nat, w_out, norm_w[1, 2], tm=_tiles(S)["small"])
    (out,), _ = _ffn_half(h, hn, S, mods, 1, 2, ffn_b_wi, ffn_b_wo, final_norm_w, None)
    return out[None]
```
